```python
import math
import jax
import jax.numpy as jnp
from jax import lax
import numpy as np

D_MODEL = 1024
BATCH = 4
SEQ = 4096
DEPTH = 2

GRID_W = 64
CTX_LEN = 256
EPS = 1e-6

MLA_HEADS = 8
Q_LORA = 384
KV_LORA = 256
QK_NOPE = 64
QK_ROPE = 32
V_HEAD = 64
ROPE_BASE = 10000.0
Q_BLOCK = 128
SM_SCALE = (QK_NOPE + QK_ROPE) ** -0.5

LRU_WIDTH = 512
LRU_BLOCKS = 8
LRU_BLOCK = LRU_WIDTH // LRU_BLOCKS
LRU_CONV = 4
LRU_C = 8.0

HY_WIDTH = 512
HY_ORDER = 2
HY_SHORT = 3
HY_EMB = 33
HY_HID = 64
HY_INNER = 2
HY_FAST_DECAY = 0.3
HY_SLOW_DECAY = 1.5
HY_DECAY_TARGET = 1e-2

FFN_HID = ((8 * D_MODEL + 3 * 256 - 1) // (3 * 256)) * 256

N_BRANCH = 3
MLA_IN = Q_LORA + KV_LORA + QK_ROPE
IN_SPLITS = (MLA_IN, MLA_IN + LRU_WIDTH, MLA_IN + 2 * LRU_WIDTH, MLA_IN + 2 * LRU_WIDTH + 3 * HY_WIDTH)
IN_WIDTH = IN_SPLITS[-1] + N_BRANCH * D_MODEL

kernel_name = 'hybrid_mla_rglru_hyena_diffusion_trunk'

F32 = jnp.float32


def rmsnorm(x, g):
    xf = x.astype(F32)
    y = xf * lax.rsqrt(jnp.mean(xf * xf, axis=-1, keepdims=True) + EPS)
    return (y * g.astype(F32)).astype(x.dtype)


def modulate(h, shift, scale):
    return h * (1 + scale) + shift


def ada_mods(cond, w, b):
    m = (jax.nn.silu(cond) @ w + b)[..., None, :]
    return jnp.split(m, 6, axis=-1)


def short_conv(u, w, b):
    k = w.shape[0]
    left = k // 2
    y = lax.conv_general_dilated(u, w[:, None, :].astype(u.dtype), window_strides=(1,),
                                 padding=[(left, k - 1 - left)],
                                 dimension_numbers=('NWC', 'WIO', 'NWC'),
                                 feature_group_count=u.shape[-1])
    return y + b


def axial_rope(x):
    n = x.shape[1]
    rows = n // GRID_W
    row = jnp.repeat(jnp.arange(rows, dtype=F32), GRID_W)
    col = jnp.tile(jnp.arange(GRID_W, dtype=F32), rows)
    seg = QK_ROPE // 2
    inv = 1.0 / (ROPE_BASE ** (jnp.arange(seg // 2, dtype=F32) * 2.0 / seg))

    def rot(xs, pos):
        ang = pos[:, None] * inv
        cos = jnp.cos(ang)[:, None, :]
        sin = jnp.sin(ang)[:, None, :]
        x1, x2 = jnp.split(xs, 2, axis=-1)
        return jnp.concatenate([x1 * cos - x2 * sin, x1 * sin + x2 * cos], axis=-1)

    out = jnp.concatenate([rot(x[..., :seg], row), rot(x[..., seg:], col)], axis=-1)
    return out.astype(x.dtype)


def mla_qkv(z, q_g, w_uq, kv_g, w_ukv, with_pos):
    b, n, _ = z.shape
    cq, ckv, kpe = jnp.split(z, [Q_LORA, Q_LORA + KV_LORA], axis=-1)
    q = (rmsnorm(cq, q_g) @ w_uq).reshape(b, n, MLA_HEADS, QK_NOPE + QK_ROPE)
    kv = (rmsnorm(ckv, kv_g) @ w_ukv).reshape(b, n, MLA_HEADS, QK_NOPE + V_HEAD)
    q_nope, q_pe = jnp.split(q, [QK_NOPE], axis=-1)
    k_nope, v = jnp.split(kv, [QK_NOPE], axis=-1)
    k_pe = kpe[:, :, None, :]
    if with_pos:
        q_pe = axial_rope(q_pe)
        k_pe = axial_rope(k_pe)
    k_pe = jnp.broadcast_to(k_pe, (b, n, MLA_HEADS, QK_ROPE))
    q = jnp.concatenate([q_nope, q_pe], axis=-1)
    k = jnp.concatenate([k_nope, k_pe], axis=-1)
    return q, k, v


def softmax_attend(q, k, v):
    s = jnp.einsum('bqhd,bkhd->bhqk', q, k).astype(F32) * SM_SCALE
    p = jax.nn.softmax(s, axis=-1).astype(v.dtype)
    return jnp.einsum('bhqk,bkhd->bqhd', p, v)


def blocked_attend(q, k, v):
    b, n, h, d = q.shape
    nb = n // Q_BLOCK
    qb = jnp.moveaxis(q.reshape(b, nb, Q_BLOCK, h, d), 1, 0)
    o = lax.map(lambda blk: softmax_attend(blk, k, v), qb)
    return jnp.moveaxis(o, 0, 1).reshape(b, n, h * v.shape[-1])


def block_diag_linear(u, w, bias):
    b, n, c = u.shape
    y = jnp.einsum('bngi,gij->bngj', u.reshape(b, n, LRU_BLOCKS, LRU_BLOCK), w)
    return y.reshape(b, n, c) + bias


def rglru_coeffs(u, w_a, b_a, w_x, b_x, lam):
    r = jax.nn.sigmoid(block_diag_linear(u, w_a, b_a).astype(F32))
    i = jax.nn.sigmoid(block_diag_linear(u, w_x, b_x).astype(F32))
    log_a = -LRU_C * r * jax.nn.softplus(-lam.astype(F32))
    a = jnp.exp(log_a)
    gain = jnp.sqrt(-jnp.expm1(2.0 * log_a))
    return a, gain * (i * u.astype(F32))


def _affine_combine(first, second):
    a1, b1 = first
    a2, b2 = second
    return a1 * a2, a2 * b1 + b2


def scan_forward(a, b, h0):
    b = b.at[:, 0].add(a[:, 0] * h0)
    return lax.associative_scan(_affine_combine, (a, b), axis=1)[1]


def scan_backward(a, b, h0):
    return jnp.flip(scan_forward(jnp.flip(a, 1), jnp.flip(b, 1), h0), 1)


def hyena_filters(n, w1, b1, w2, b2, freq, w_out):
    t = jnp.linspace(0.0, 1.0, n, dtype=F32)[:, None]
    bands = (HY_EMB - 1) // 2
    w = 2.0 * math.pi * jnp.arange(n, dtype=F32) / n
    f = jnp.linspace(1e-4, bands - 1, bands, dtype=F32)
    ang = w[:, None] * f[None, :]
    z = jnp.concatenate([t, jnp.cos(ang), -jnp.sin(ang)], axis=-1)
    fr = freq.astype(F32)
    h = jnp.sin(fr * (z @ w1.astype(F32) + b1.astype(F32)))
    for j in range(HY_INNER):
        h = jnp.sin(fr * (h @ w2[j].astype(F32) + b2[j].astype(F32)))
    h = (h @ w_out.astype(F32)).reshape(n, 2, HY_ORDER, HY_WIDTH)
    max_decay = math.log(HY_DECAY_TARGET) / HY_FAST_DECAY
    min_decay = math.log(HY_DECAY_TARGET) / HY_SLOW_DECAY
    deltas = jnp.abs(jnp.linspace(min_decay, max_decay, HY_WIDTH, dtype=F32))
    return h * jnp.exp(-t * deltas)[:, None, None, :]


def bidir_long_conv(u, h_fwd, h_bwd, skip):
    n = u.shape[1]
    k = jnp.concatenate([h_fwd[:1] + h_bwd[:1], h_fwd[1:], jnp.zeros_like(h_fwd[:1]), h_bwd[:0:-1]], axis=0)
    kf = jnp.fft.rfft(k, axis=0)
    uf = jnp.fft.rfft(u.astype(F32), n=2 * n, axis=1)
    y = jnp.fft.irfft(uf * kf[None], n=2 * n, axis=1)[:, :n]
    return y + u.astype(F32) * skip.astype(F32)


def hyena_mix(z, conv_w, conv_b, filt, skip):
    z = short_conv(z, conv_w, conv_b)
    v, x1, x2 = jnp.split(z, 3, axis=-1)
    y = x1.astype(F32) * bidir_long_conv(v, filt[:, 0, 0], filt[:, 1, 0], skip[0])
    y = x2.astype(F32) * bidir_long_conv(y, filt[:, 0, 1], filt[:, 1, 1], skip[1])
    return y.astype(z.dtype)


def merge_branches(gate_logits, ya, yb, yc, p):
    g = jax.nn.sigmoid(gate_logits.astype(F32)).astype(ya.dtype)
    ga, gb, gc = jnp.split(g, N_BRANCH, axis=-1)
    y = ga * (ya @ p['w_br_a']) + gb * (yb @ p['w_br_b']) + gc * (yc @ p['w_br_c'])
    return y @ p['w_out']


def swiglu(h, p):
    return (jax.nn.silu(h @ p['ffn_w_gate']) * (h @ p['ffn_w_up'])) @ p['ffn_w_down']


def trunk_layer(x, xc, c, c_ctx, p, ctx_out):
    sh1, sc1, g1, sh2, sc2, g2 = ada_mods(c, p['ada_w'], p['ada_b'])
    csh1, csc1, cg1, csh2, csc2, cg2 = ada_mods(c_ctx, p['ada_w'], p['ada_b'])
    z_lat = modulate(rmsnorm(x, p['norm1_g']), sh1, sc1) @ p['w_in']
    z_ctx = modulate(rmsnorm(xc, p['norm1_g']), csh1, csc1) @ p['w_in']
    mla_l, lx_l, lg_l, hy_l, gt_l = jnp.split(z_lat, IN_SPLITS, axis=-1)
    mla_c, lx_c, lg_c, hy_c, gt_c = jnp.split(z_ctx, IN_SPLITS, axis=-1)

    qc, kc, vc = mla_qkv(mla_c, p['q_norm_g'], p['w_uq'], p['kv_norm_g'], p['w_ukv'], False)
    ql, kl, vl = mla_qkv(mla_l, p['q_norm_g'], p['w_uq'], p['kv_norm_g'], p['w_ukv'], True)
    ya_l = blocked_attend(ql, jnp.concatenate([kc, kl], axis=1), jnp.concatenate([vc, vl], axis=1))

    uc = short_conv(lx_c, p['lru_conv_w'], p['lru_conv_b'])
    ul = short_conv(lx_l, p['lru_conv_w'], p['lru_conv_b'])
    fwd = (p['lru_wa'][0], p['lru_ba'][0], p['lru_wx'][0], p['lru_bx'][0], p['lru_lam'][0])
    bwd = (p['lru_wa'][1], p['lru_ba'][1], p['lru_wx'][1], p['lru_bx'][1], p['lru_lam'][1])
    h0 = jnp.zeros((xc.shape[0], LRU_WIDTH), F32)
    hcf = scan_forward(*rglru_coeffs(uc, *fwd), h0)
    hcb = scan_backward(*rglru_coeffs(uc, *bwd), h0)
    hlf = scan_forward(*rglru_coeffs(ul, *fwd), hcf[:, -1])
    hlb = scan_backward(*rglru_coeffs(ul, *bwd), hcb[:, 0])
    yb_l = (hlf + hlb).astype(lg_l.dtype) * jax.nn.gelu(lg_l)

    filt_args = (p['hy_w1'], p['hy_b1'], p['hy_w2'], p['hy_b2'], p['hy_freq'], p['hy_w_out'])
    yc_l = hyena_mix(hy_l, p['hy_conv_w'], p['hy_conv_b'], hyena_filters(x.shape[1], *filt_args), p['hy_skip'])

    x = x + g1 * merge_branches(gt_l, ya_l, yb_l, yc_l, p)
    x = x + g2 * swiglu(modulate(rmsnorm(x, p['norm2_g']), sh2, sc2), p)

    if ctx_out:
        ya_c = softmax_attend(qc, kc, vc).reshape(xc.shape[0], xc.shape[1], MLA_HEADS * V_HEAD)
        yb_c = (hcf + hcb).astype(lg_c.dtype) * jax.nn.gelu(lg_c)
        yc_c = hyena_mix(hy_c, p['hy_conv_w'], p['hy_conv_b'], hyena_filters(xc.shape[1], *filt_args), p['hy_skip'])
        xc = xc + cg1 * merge_branches(gt_c, ya_c, yb_c, yc_c, p)
        xc = xc + cg2 * swiglu(modulate(rmsnorm(xc, p['norm2_g']), csh2, csc2), p)
    return x, xc


def setup_inputs(seed: int = 0) -> dict:
    key = jax.random.key(seed)
    keys = iter(jax.random.split(key, 48))
    L, D = DEPTH, D_MODEL

    def nrm(shape, scale):
        return jax.random.normal(next(keys), shape, F32) * scale

    def gain(shape):
        return 1.0 + nrm(shape, 0.02)

    a_c = jax.random.uniform(next(keys), (L, 2, LRU_WIDTH), F32, 0.9, 0.999)
    a = a_c ** (1.0 / LRU_C)
    lam = jnp.log(a) - jnp.log1p(-a)
    return {
        'x': nrm((BATCH, SEQ, D), 1.0),
        'c': nrm((BATCH, D), 1.0),
        'ctx': nrm((BATCH, CTX_LEN, D), 1.0),
        'c_ctx': nrm((D,), 1.0),
        'ada_w': nrm((L, D, 6 * D), 0.5 * D ** -0.5),
        'ada_b': nrm((L, 6 * D), 0.02),
        'norm1_g': gain((L, D)),
        'norm2_g': gain((L, D)),
        'w_in': nrm((L, D, IN_WIDTH), D ** -0.5),
        'q_norm_g': gain((L, Q_LORA)),
        'w_uq': nrm((L, Q_LORA, MLA_HEADS * (QK_NOPE + QK_ROPE)), Q_LORA ** -0.5),
        'kv_norm_g': gain((L, KV_LORA)),
        'w_ukv': nrm((L, KV_LORA, MLA_HEADS * (QK_NOPE + V_HEAD)), KV_LORA ** -0.5),
        'lru_conv_w': nrm((L, LRU_CONV, LRU_WIDTH), LRU_CONV ** -0.5),
        'lru_conv_b': nrm((L, LRU_WIDTH), 0.02),
        'lru_wa': nrm((L, 2, LRU_BLOCKS, LRU_BLOCK, LRU_BLOCK), LRU_BLOCK ** -0.5),
        'lru_ba': nrm((L, 2, LRU_WIDTH), 0.02),
        'lru_wx': nrm((L, 2, LRU_BLOCKS, LRU_BLOCK, LRU_BLOCK), LRU_BLOCK ** -0.5),
        'lru_bx': nrm((L, 2, LRU_WIDTH), 0.02),
        'lru_lam': lam,
        'hy_conv_w': nrm((L, HY_SHORT, 3 * HY_WIDTH), HY_SHORT ** -0.5),
        'hy_conv_b': nrm((L, 3 * HY_WIDTH), 0.02),
        'hy_w1': nrm((L, HY_EMB, HY_HID), HY_EMB ** -0.5),
        'hy_b1': nrm((L, HY_HID), 0.02),
        'hy_w2': nrm((L, HY_INNER, HY_HID, HY_HID), HY_HID ** -0.5),
        'hy_b2': nrm((L, HY_INNER, HY_HID), 0.02),
        'hy_freq': gain((L, HY_HID)),
        'hy_w_out': nrm((L, HY_HID, 2 * HY_ORDER * HY_WIDTH), 0.005),
        'hy_skip': nrm((L, HY_ORDER, HY_WIDTH), 0.3),
        'w_br_a': nrm((L, MLA_HEADS * V_HEAD, D), (MLA_HEADS * V_HEAD) ** -0.5),
        'w_br_b': nrm((L, LRU_WIDTH, D), LRU_WIDTH ** -0.5),
        'w_br_c': nrm((L, HY_WIDTH, D), HY_WIDTH ** -0.5),
        'w_out': nrm((L, D, D), D ** -0.5),
        'ffn_w_gate': nrm((L, D, FFN_HID), D ** -0.5),
        'ffn_w_up': nrm((L, D, FFN_HID), D ** -0.5),
        'ffn_w_down': nrm((L, FFN_HID, D), FFN_HID ** -0.5),
        'final_norm_g': gain((D,)),
    }


def reference(x, c, ctx, c_ctx, ada_w, ada_b, norm1_g, norm2_g, w_in, q_norm_g, w_uq, kv_norm_g, w_ukv,
              lru_conv_w, lru_conv_b, lru_wa, lru_ba, lru_wx, lru_bx, lru_lam,
              hy_conv_w, hy_conv_b, hy_w1, hy_b1, hy_w2, hy_b2, hy_freq, hy_w_out, hy_skip,
              w_br_a, w_br_b, w_br_c, w_out, ffn_w_gate, ffn_w_up, ffn_w_down, final_norm_g):
    xc = ctx
    for l in range(DEPTH):
        p = dict(ada_w=ada_w[l], ada_b=ada_b[l], norm1_g=norm1_g[l], norm2_g=norm2_g[l], w_in=w_in[l],
                 q_norm_g=q_norm_g[l], w_uq=w_uq[l], kv_norm_g=kv_norm_g[l], w_ukv=w_ukv[l],
                 lru_conv_w=lru_conv_w[l], lru_conv_b=lru_conv_b[l], lru_wa=lru_wa[l], lru_ba=lru_ba[l],
                 lru_wx=lru_wx[l], lru_bx=lru_bx[l], lru_lam=lru_lam[l],
                 hy_conv_w=hy_conv_w[l], hy_conv_b=hy_conv_b[l], hy_w1=hy_w1[l], hy_b1=hy_b1[l],
                 hy_w2=hy_w2[l], hy_b2=hy_b2[l], hy_freq=hy_freq[l], hy_w_out=hy_w_out[l], hy_skip=hy_skip[l],
                 w_br_a=w_br_a[l], w_br_b=w_br_b[l], w_br_c=w_br_c[l], w_out=w_out[l],
                 ffn_w_gate=ffn_w_gate[l], ffn_w_up=ffn_w_up[l], ffn_w_down=ffn_w_down[l])
        x, xc = trunk_layer(x, xc, c, c_ctx, p, l < DEPTH - 1)
    return rmsnorm(x, final_norm_g)
```

```python
import functools
import math

import numpy as np
import jax
import jax.numpy as jnp
from jax import lax
from jax.experimental import pallas as pl
from jax.experimental.pallas import tpu as pltpu

F32 = jnp.float32
BF16 = jnp.bfloat16

D_MODEL = 1024
GRID_W = 64
EPS = 1e-6

MLA_HEADS = 8
Q_LORA = 384
KV_LORA = 256
QK_NOPE = 64
QK_ROPE = 32
V_HEAD = 64
ROPE_BASE = 10000.0
SM_SCALE = (QK_NOPE + QK_ROPE) ** -0.5
HEAD_PAD = 128
MLA_Z = Q_LORA + KV_LORA + 2 * HEAD_PAD

LRU_WIDTH = 512
LRU_BLOCKS = 8
LRU_CONV = 4
LRU_C = 8.0

HY_WIDTH = 512
HY_ORDER = 2
HY_SHORT = 3
HY_EMB = 33
HY_EMB_PAD = 40
HY_HID = 64
HY_INNER = 2
HY_FAST_DECAY = 0.3
HY_SLOW_DECAY = 1.5
HY_DECAY_TARGET = 1e-2
HY_N2 = 128
HY_CB = 128
HY_GROUP = 4

FFN_HID = 2816
N_BRANCH = 3
MLA_IN = Q_LORA + KV_LORA + QK_ROPE
IN_SPLITS = (MLA_IN, MLA_IN + LRU_WIDTH, MLA_IN + 2 * LRU_WIDTH, MLA_IN + 2 * LRU_WIDTH + 3 * HY_WIDTH)

Q_PRESCALE = SM_SCALE * math.log2(math.e)
FLASH_Q_TILE = 1024
FLASH_KEY_CHUNK = 512
TOKEN_TILE = 256
HALO = 8
VMEM_LIMIT = 56 * 1024 * 1024


def _params(*sem):
    return pltpu.CompilerParams(dimension_semantics=sem, vmem_limit_bytes=VMEM_LIMIT)


def _dot(a, b):
    return jnp.dot(a, b, preferred_element_type=F32)


def _rms(x, g):
    return x * lax.rsqrt(jnp.mean(x * x, axis=-1, keepdims=True) + EPS) * g


def _norm_mod(x, g, shift, scale):
    return _rms(x, g) * (1.0 + scale) + shift


def _full_spec(arr):
    nd = arr.ndim
    return pl.BlockSpec(arr.shape, lambda *_: (0,) * nd)


def _mod_spec(mods, j):
    if mods.shape[0] > 1:
        return pl.BlockSpec((1, 1, D_MODEL), lambda b, i, j=j: (b, 0, j))
    return pl.BlockSpec((1, 1, D_MODEL), lambda b, i, j=j: (0, 0, j))


def _ada_kernel(c_ref, w_ref, b_ref, o_ref):
    c = c_ref[...]
    s = (c * jax.nn.sigmoid(c)).astype(BF16)
    o_ref[0] = _dot(s, w_ref[0].astype(BF16)) + b_ref[0]


def _ada_mods(cond, ada_w, ada_b):
    depth, _, width = ada_w.shape
    tn = 1536
    return pl.pallas_call(
        _ada_kernel,
        grid=(depth, width // tn),
        in_specs=[pl.BlockSpec((8, D_MODEL), lambda l, j: (0, 0)),
                  pl.BlockSpec((1, D_MODEL, tn), lambda l, j: (l, 0, j)),
                  pl.BlockSpec((1, 1, tn), lambda l, j: (l, 0, j))],
        out_specs=pl.BlockSpec((1, 8, tn), lambda l, j: (l, 0, j)),
        out_shape=jax.ShapeDtypeStruct((depth, 8, width), F32),
        compiler_params=_params("arbitrary", "arbitrary"),
        name="ada_mods",
    )(cond, ada_w, ada_b.reshape(depth, 1, width))


def _inproj_kernel(x_ref, sh_ref, sc_ref, g_ref, wm_ref, wx_ref, wg_ref, wh_ref,
                   zm_ref, lx_ref, lg_ref, hy_ref):
    h = _norm_mod(x_ref[0], g_ref[...], sh_ref[0], sc_ref[0]).astype(BF16)
    zm_ref[0] = _dot(h, wm_ref[...])
    lx_ref[0] = _dot(h, wx_ref[...])
    lg_ref[0] = _dot(h, wg_ref[...])
    hy_ref[0] = _dot(h, wh_ref[...])


def _in_proj(x, mods, norm_g, wm, wx, wg, wh):
    bx, t, _ = x.shape
    tm = TOKEN_TILE
    tile = lambda w: pl.BlockSpec((1, tm, w), lambda b, i: (b, i, 0))
    widths = (MLA_Z, LRU_WIDTH, LRU_WIDTH, 3 * HY_WIDTH)
    return pl.pallas_call(
        _inproj_kernel,
        grid=(bx, t // tm),
        in_specs=[tile(D_MODEL), _mod_spec(mods, 0), _mod_spec(mods, 1), _full_spec(norm_g),
                  _full_spec(wm), _full_spec(wx), _full_spec(wg), _full_spec(wh)],
        out_specs=[tile(w) for w in widths],
        out_shape=[jax.ShapeDtypeStruct((bx, t, w), F32) for w in widths],
        compiler_params=_params("parallel", "parallel"),
        name="in_proj",
    )(x, mods, mods, norm_g, wm, wx, wg, wh)


def _mla_prep_kernel(z_ref, qg_ref, kvg_ref, wq_ref, wqs_ref, wk_ref, wv_ref, cos_ref, sin_ref,
                     q_ref, k_ref, v_ref):
    z = z_ref[0]
    nq = _rms(z[:, :Q_LORA], qg_ref[...]).astype(BF16)
    nkv = _rms(z[:, Q_LORA:Q_LORA + KV_LORA], kvg_ref[...]).astype(BF16)
    pe = z[:, Q_LORA + KV_LORA:Q_LORA + KV_LORA + HEAD_PAD]
    pes = z[:, Q_LORA + KV_LORA + HEAD_PAD:]
    cos = cos_ref[...]
    sin = sin_ref[...]
    q = _dot(nq, wq_ref[...])
    qs = _dot(nq, wqs_ref[...])
    kn = _dot(nkv, wk_ref[...])
    k_pe = pe * cos + pes * sin
    for h in range(MLA_HEADS):
        sl = slice(h * HEAD_PAD, (h + 1) * HEAD_PAD)
        q_ref[0, :, sl] = ((q[:, sl] * cos + qs[:, sl] * sin) * Q_PRESCALE).astype(BF16)
        k_ref[0, :, sl] = (kn[:, sl] + k_pe).astype(BF16)
    v_ref[0] = _dot(nkv, wv_ref[...]).astype(BF16)


def _mla_prep(z, qg, kvg, wq, wqs, wk, wv, cos, sin):
    bx, t, _ = z.shape
    tm = TOKEN_TILE
    tile = lambda w: pl.BlockSpec((1, tm, w), lambda b, i: (b, i, 0))
    if cos.shape[0] == tm:
        tab = pl.BlockSpec((tm, HEAD_PAD), lambda b, i: (0, 0))
    else:
        tab = pl.BlockSpec((tm, HEAD_PAD), lambda b, i: (i, 0))
    widths = (MLA_HEADS * HEAD_PAD, MLA_HEADS * HEAD_PAD, MLA_HEADS * V_HEAD)
    return pl.pallas_call(
        _mla_prep_kernel,
        grid=(bx, t // tm),
        in_specs=[tile(MLA_Z), _full_spec(qg), _full_spec(kvg), _full_spec(wq), _full_spec(wqs),
                  _full_spec(wk), _full_spec(wv), tab, tab],
        out_specs=[tile(w) for w in widths],
        out_shape=[jax.ShapeDtypeStruct((bx, t, w), BF16) for w in widths],
        compiler_params=_params("parallel", "parallel"),
        name="mla_prep",
    )(z, qg, kvg, wq, wqs, wk, wv, cos, sin)


def _qk(q, k):
    return lax.dot_general(q, k, (((1,), (1,)), ((), ())), preferred_element_type=F32)


def _flash_kernel(*refs, chunks):
    q_ref, o_ref = refs[0], refs[-1]
    kv = refs[1:-1]
    state = [None, None]
    for src, off, size in chunks:
        k_ref, v_ref = kv[2 * src], kv[2 * src + 1]
        v = v_ref[0, off:off + size, :]
        for h in range(2):
            sl = slice(h * HEAD_PAD, (h + 1) * HEAD_PAD)
            s = _qk(q_ref[0, :, sl], k_ref[0, off:off + size, sl])
            m_blk = jnp.max(s, axis=-1, keepdims=True)
            if state[h] is None:
                m = m_blk
                p = jnp.exp2(s - m)
                l = jnp.sum(p, axis=-1, keepdims=True)
                acc = _dot(p.astype(BF16), v)
            else:
                m_old, l, acc = state[h]
                m = jnp.maximum(m_old, m_blk)
                alpha = jnp.exp2(m_old - m)
                p = jnp.exp2(s - m)
                l = alpha * l + jnp.sum(p, axis=-1, keepdims=True)
                acc = alpha * acc + _dot(p.astype(BF16), v)
            state[h] = (m, l, acc)
    outs = [acc / l for _, l, acc in state]
    lane = lax.broadcasted_iota(jnp.int32, outs[0].shape, 1)
    o_ref[0] = jnp.where(lane < V_HEAD, outs[0], outs[1]).astype(o_ref.dtype)


def _flash(q, kc, vc, kl=None, vl=None):
    bx, t, _ = q.shape
    tq = min(FLASH_Q_TILE, t)
    sc = kc.shape[1]
    pairs = MLA_HEADS // 2
    qspec = pl.BlockSpec((1, tq, 2 * HEAD_PAD), lambda b, hp, i: (b, i, hp))
    kspec = lambda n: pl.BlockSpec((1, n, 2 * HEAD_PAD), lambda b, hp, i: (b, 0, hp))
    vspec = lambda n: pl.BlockSpec((1, n, 2 * V_HEAD), lambda b, hp, i: (b, 0, hp))
    in_specs = [qspec, kspec(sc), vspec(sc)]
    args = [q, kc, vc]
    chunks = [(0, 0, sc)]
    if kl is not None:
        sl = kl.shape[1]
        size = min(FLASH_KEY_CHUNK, sl)
        chunks += [(1, off, size) for off in range(0, sl, size)]
        in_specs += [kspec(sl), vspec(sl)]
        args += [kl, vl]
    return pl.pallas_call(
        functools.partial(_flash_kernel, chunks=tuple(chunks)),
        grid=(bx, pairs, t // tq),
        in_specs=in_specs,
        out_specs=pl.BlockSpec((1, tq, 2 * V_HEAD), lambda b, hp, i: (b, i, hp)),
        out_shape=jax.ShapeDtypeStruct((bx, t, MLA_HEADS * V_HEAD), BF16),
        compiler_params=_params("parallel", "parallel", "arbitrary"),
        name="flash",
    )(*args)


def _lru_kernel(xf_ref, xfp_ref, xfn_ref, xb_ref, xbp_ref, xbn_ref, cw_ref, cb_ref,
                wf_ref, bf_ref, wb_ref, bb_ref, lam_ref, h0_ref,
                hf_ref, hb_ref, hl_ref, af_s, bfw_s, ab_s, bbw_s, pad_s, carry_s):
    t = pl.program_id(0)
    nt = pl.num_programs(0)
    bx, tm, _ = xf_ref.shape

    @pl.when(t == 0)
    def _():
        carry_s[...] = h0_ref[...]

    def coeffs(b, x_ref, prev_ref, next_ref, tile, w_ref, bias_ref, lam, a_s, b_s):
        x = x_ref[b]
        pad_s[0:HALO, :] = jnp.where(tile > 0, prev_ref[b], 0.0)
        pad_s[HALO:HALO + tm, :] = x
        pad_s[HALO + tm:, :] = jnp.where(tile < nt - 1, next_ref[b], 0.0)
        u = (cw_ref[0:1, :] * pad_s[HALO - 2:HALO - 2 + tm, :] + cw_ref[1:2, :] * pad_s[HALO - 1:HALO - 1 + tm, :]
             + cw_ref[2:3, :] * x + cw_ref[3:4, :] * pad_s[HALO + 1:HALO + 1 + tm, :] + cb_ref[...])
        y = _dot(u.astype(BF16), w_ref[...]) + bias_ref[...]
        r = jax.nn.sigmoid(y[:, :LRU_WIDTH])
        i = jax.nn.sigmoid(y[:, LRU_WIDTH:])
        neg = -lam
        softplus = jnp.maximum(neg, 0.0) + jnp.log1p(jnp.exp(-jnp.abs(neg)))
        a = jnp.exp((-LRU_C) * r * softplus)
        a_s[b] = a
        b_s[b] = jnp.sqrt(1.0 - a * a) * (i * u)

    for b in range(bx):
        coeffs(b, xf_ref, xfp_ref, xfn_ref, t, wf_ref, bf_ref, lam_ref[0:1, :], af_s, bfw_s)
        coeffs(b, xb_ref, xbp_ref, xbn_ref, nt - 1 - t, wb_ref, bb_ref, lam_ref[1:2, :], ab_s, bbw_s)

    def body(j, hs):
        jb = tm - 1 - j
        out = []
        for b in range(bx):
            hf = af_s[b, pl.ds(j, 1), :] * hs[2 * b] + bfw_s[b, pl.ds(j, 1), :]
            hf_ref[b, pl.ds(j, 1), :] = hf
            hb = ab_s[b, pl.ds(jb, 1), :] * hs[2 * b + 1] + bbw_s[b, pl.ds(jb, 1), :]
            hb_ref[b, pl.ds(jb, 1), :] = hb
            out += [hf, hb]
        return tuple(out)

    init = tuple(carry_s[b, d:d + 1, :] for b in range(bx) for d in range(2))
    hs = lax.fori_loop(0, tm, body, init, unroll=8)
    for b in range(bx):
        for d in range(2):
            carry_s[b, d:d + 1, :] = hs[2 * b + d]
    hl_ref[...] = carry_s[...]


def _lru(lx, conv_w, conv_b, wf, biasf, wb, biasb, lam, h0):
    bx, t, w = lx.shape
    tm = TOKEN_TILE
    nt = t // tm
    hb_per = tm // HALO
    last_halo = t // HALO - 1
    fwd = lambda i: (0, i, 0)
    bwd = lambda i: (0, nt - 1 - i, 0)
    prev_of = lambda f: (lambda i: (0, jnp.maximum(f(i)[1] * hb_per - 1, 0), 0))
    next_of = lambda f: (lambda i: (0, jnp.minimum((f(i)[1] + 1) * hb_per, last_halo), 0))
    tile = lambda f: pl.BlockSpec((bx, tm, w), f)
    halo = lambda f: pl.BlockSpec((bx, HALO, w), f)
    state = pl.BlockSpec((bx, 2, w), lambda i: (0, 0, 0))
    return pl.pallas_call(
        _lru_kernel,
        grid=(nt,),
        in_specs=[tile(fwd), halo(prev_of(fwd)), halo(next_of(fwd)),
                  tile(bwd), halo(prev_of(bwd)), halo(next_of(bwd)),
                  _full_spec(conv_w), _full_spec(conv_b), _full_spec(wf), _full_spec(biasf),
                  _full_spec(wb), _full_spec(biasb), _full_spec(lam), state],
        out_specs=[tile(fwd), tile(bwd), state],
        out_shape=[jax.ShapeDtypeStruct((bx, t, w), F32), jax.ShapeDtypeStruct((bx, t, w), F32),
                   jax.ShapeDtypeStruct((bx, 2, w), F32)],
        scratch_shapes=[pltpu.VMEM((bx, tm, w), F32)] * 4
                       + [pltpu.VMEM((tm + 2 * HALO, w), F32), pltpu.VMEM((bx, 2, w), F32)],
        compiler_params=_params("arbitrary"),
        name="rglru",
    )(lx, lx, lx, lx, lx, lx, conv_w, conv_b, wf, biasf, wb, biasb, lam, h0)


def _hyfilt_kernel(z_ref, w1_ref, b1_ref, w2_ref, b2_ref, fr_ref, wo_ref, dec_ref, o_ref):
    hi = lax.Precision.HIGHEST
    fr = fr_ref[...]
    h = jnp.sin(fr * (jnp.dot(z_ref[...], w1_ref[...], precision=hi, preferred_element_type=F32) + b1_ref[...]))
    for j in range(HY_INNER):
        h = jnp.sin(fr * (jnp.dot(h, w2_ref[j], precision=hi, preferred_element_type=F32) + b2_ref[j:j + 1, :]))
    dec = dec_ref[...]
    for g in range(2 * HY_ORDER):
        sl = slice(g * HY_WIDTH, (g + 1) * HY_WIDTH)
        o_ref[:, sl] = jnp.dot(h, wo_ref[:, sl], precision=hi, preferred_element_type=F32) * dec


def _hy_filters(z, w1, b1, w2, b2, freq, w_out, decay):
    n = z.shape[0]
    tn = min(n, 512)
    width = 2 * HY_ORDER * HY_WIDTH
    return pl.pallas_call(
        _hyfilt_kernel,
        grid=(n // tn,),
        in_specs=[pl.BlockSpec((tn, HY_EMB_PAD), lambda i: (i, 0)), _full_spec(w1), _full_spec(b1),
                  _full_spec(w2), _full_spec(b2), _full_spec(freq), _full_spec(w_out),
                  pl.BlockSpec((tn, HY_WIDTH), lambda i: (i, 0))],
        out_specs=pl.BlockSpec((tn, width), lambda i: (i, 0)),
        out_shape=jax.ShapeDtypeStruct((n, width), F32),
        compiler_params=_params("parallel"),
        name="hyena_filters",
    )(z, w1, b1, w2, b2, freq, w_out, decay)


def _bitrev(k, bits):
    r = jnp.zeros_like(k)
    for b in range(bits):
        r = r | (((k >> b) & 1) << (bits - 1 - b))
    return r


def _fft_lead_fwd(sre, sim, wr_ref, wi_ref, n1):
    half = n1 // 2
    m = half
    first = True
    while m >= 1:
        shift = int(math.log2(m))
        stride = half // m

        def body(q, c, m=m, shift=shift, stride=stride, first=first):
            grp = q >> shift
            j = q - (grp << shift)
            i0 = (grp << (shift + 1)) + j
            i1 = i0 + m
            wr = wr_ref[j * stride]
            wi = wi_ref[j * stride]
            ar, ai = sre[i0], sim[i0]
            if first:
                dr, di = ar, ai
            else:
                br, bi = sre[i1], sim[i1]
                sre[i0] = ar + br
                sim[i0] = ai + bi
                dr, di = ar - br, ai - bi
            sre[i1] = dr * wr - di * wi
            sim[i1] = dr * wi + di * wr
            return c

        lax.fori_loop(0, half, body, 0)
        first = False
        m //= 2


def _fft_lead_inv(sre, sim, wr_ref, wi_ref, n1):
    half = n1 // 2
    m = 1
    while m <= half:
        shift = int(math.log2(m))
        stride = half // m
        last = m == half

        def body(q, c, m=m, shift=shift, stride=stride, last=last):
            grp = q >> shift
            j = q - (grp << shift)
            i0 = (grp << (shift + 1)) + j
            i1 = i0 + m
            wr = wr_ref[j * stride]
            wi = wi_ref[j * stride]
            ar, ai = sre[i0], sim[i0]
            br, bi = sre[i1], sim[i1]
            tr = br * wr + bi * wi
            ti = bi * wr - br * wi
            sre[i0] = ar + tr
            sim[i0] = ai + ti
            if not last:
                sre[i1] = ar - tr
                sim[i1] = ai - ti
            return c

        lax.fori_loop(0, half, body, 0)
        m *= 2


def _spectrum_loop(sre, sim, f2_s, tw, n1, emit):
    tw0r_ref, tw0i_ref, wgr_ref, wgi_ref = tw
    bits = int(math.log2(n1))
    group = min(HY_GROUP, n1)

    def body(kb, tws):
        ks, xs = [], []
        for g in range(group):
            k1 = kb * group + g
            blk = _bitrev(k1, bits)
            tr, ti = tws[2 * g], tws[2 * g + 1]
            ar, ai = sre[blk], sim[blk]
            xs.append(jnp.concatenate([ar * tr - ai * ti, ar * ti + ai * tr], axis=0).astype(BF16))
            ks.append((k1, blk))
        b2 = _dot(f2_s[...], jnp.concatenate(xs, axis=1))
        emit(ks, b2, tws)
        wgr, wgi = wgr_ref[...], wgi_ref[...]
        nxt = []
        for g in range(group):
            tr, ti = tws[2 * g], tws[2 * g + 1]
            nxt += [tr * wgr - ti * wgi, tr * wgi + ti * wgr]
        return tuple(nxt)

    init = tuple(r[g] for g in range(group) for r in (tw0r_ref, tw0i_ref))
    lax.fori_loop(0, n1 // group, body, init)


def _group_cols(b2, g, cb):
    return b2[:HY_N2, g * cb:(g + 1) * cb], b2[HY_N2:, g * cb:(g + 1) * cb]


def _hyspec_kernel(wr_ref, wi_ref, hf_ref, hb_ref, skip_ref, f2_ref, tw0r_ref, tw0i_ref, wgr_ref, wgi_ref,
                   k_ref, sre, sim, f2_s):
    n = hf_ref.shape[0]
    n1 = 2 * n // HY_N2
    half = n1 // 2
    cb = hf_ref.shape[1]
    f2_s[...] = f2_ref[...].astype(BF16)
    for direction, h_ref in enumerate((hf_ref, hb_ref)):
        sre[0:half] = h_ref[...].reshape(half, HY_N2, cb)
        sim[0:half] = jnp.zeros((half, HY_N2, cb), F32)
        _fft_lead_fwd(sre, sim, wr_ref, wi_ref, n1)

        def emit(ks, b2, tws, direction=direction):
            for g, (k1, _) in enumerate(ks):
                br, bi = _group_cols(b2, g, cb)
                if direction == 0:
                    k_ref[0, k1, 0:HY_N2, :] = br + skip_ref[0]
                    k_ref[0, k1, HY_N2:, :] = bi
                else:
                    k_ref[0, k1, 0:HY_N2, :] = k_ref[0, k1, 0:HY_N2, :] + br
                    k_ref[0, k1, HY_N2:, :] = k_ref[0, k1, HY_N2:, :] - bi

        _spectrum_loop(sre, sim, f2_s, (tw0r_ref, tw0i_ref, wgr_ref, wgi_ref), n1, emit)


def _hy_spectra(filt, skip, consts):
    n = filt.shape[0]
    n1 = 2 * n // HY_N2
    cbn = HY_WIDTH // HY_CB
    smem = pl.BlockSpec(memory_space=pltpu.SMEM)
    col = lambda direction: (lambda o, c: (0, (direction * HY_ORDER + o) * cbn + c))
    tw_names = ("tw0r", "tw0i", "wgr", "wgi")
    return pl.pallas_call(
        _hyspec_kernel,
        grid=(HY_ORDER, cbn),
        in_specs=[smem, smem,
                  pl.BlockSpec((n, HY_CB), col(0)), pl.BlockSpec((n, HY_CB), col(1)),
                  pl.BlockSpec((1, 1, HY_CB), lambda o, c: (o, 0, c)),
                  _full_spec(consts["f2"])] + [_full_spec(consts[k]) for k in tw_names],
        out_specs=pl.BlockSpec((1, n1, 2 * HY_N2, HY_CB), lambda o, c: (o, 0, 0, c)),
        out_shape=jax.ShapeDtypeStruct((HY_ORDER, n1, 2 * HY_N2, HY_WIDTH), F32),
        scratch_shapes=[pltpu.VMEM((n1, HY_N2, HY_CB), F32)] * 2 + [pltpu.VMEM((2 * HY_N2, 2 * HY_N2), BF16)],
        compiler_params=_params("parallel", "parallel"),
        name="hyena_spectra",
    )(consts["wr"], consts["wi"], filt, filt, skip[:, None, :], consts["f2"], *[consts[k] for k in tw_names])


def _short_conv3(x, pad_s, w_ref, b_ref):
    n = x.shape[0]
    pad_s[HALO:HALO + n, :] = x
    return (w_ref[0:1, :] * pad_s[HALO - 1:HALO - 1 + n, :] + w_ref[1:2, :] * x
            + w_ref[2:3, :] * pad_s[HALO + 1:HALO + 1 + n, :] + b_ref[...])


def _hyconv_kernel(wr_ref, wi_ref, u_ref, g_ref, ucw_ref, ucb_ref, gcw_ref, gcb_ref, k_ref,
                   f2_ref, f2i_ref, tw0r_ref, tw0i_ref, wgr_ref, wgi_ref, o_ref,
                   sre, sim, pad_s, f2_s, f2i_s, *, conv_u):
    n = u_ref.shape[1]
    cb = u_ref.shape[2]
    n1 = 2 * n // HY_N2
    half = n1 // 2
    f2_s[...] = f2_ref[...].astype(BF16)
    f2i_s[...] = f2i_ref[...].astype(BF16)
    margin = jnp.zeros((HALO, cb), F32)
    pad_s[0:HALO, :] = margin
    pad_s[HALO + n:, :] = margin

    for b, s in enumerate((sre, sim)):
        u = u_ref[b].astype(F32)
        if conv_u:
            u = _short_conv3(u, pad_s, ucw_ref, ucb_ref)
        s[0:half] = u.reshape(half, HY_N2, cb)
    _fft_lead_fwd(sre, sim, wr_ref, wi_ref, n1)

    def emit(ks, b2, tws):
        ps = []
        for g, (k1, _) in enumerate(ks):
            br, bi = _group_cols(b2, g, cb)
            kr = k_ref[0, k1, 0:HY_N2, :]
            ki = k_ref[0, k1, HY_N2:, :]
            ps.append(jnp.concatenate([br * kr - bi * ki, br * ki + bi * kr], axis=0).astype(BF16))
        c2 = _dot(f2i_s[...], jnp.concatenate(ps, axis=1))
        for g, (_, blk) in enumerate(ks):
            cr, ci = _group_cols(c2, g, cb)
            tr, ti = tws[2 * g], tws[2 * g + 1]
            sre[blk] = cr * tr + ci * ti
            sim[blk] = ci * tr - cr * ti

    _spectrum_loop(sre, sim, f2_s, (tw0r_ref, tw0i_ref, wgr_ref, wgi_ref), n1, emit)
    _fft_lead_inv(sre, sim, wr_ref, wi_ref, n1)

    for b, s in enumerate((sre, sim)):
        gate = _short_conv3(g_ref[b], pad_s, gcw_ref, gcb_ref)
        o_ref[b] = (gate * s[0:half].reshape(n, cb)).astype(o_ref.dtype)


def _hy_conv(u, u_col, g, g_col, conv_w, conv_b, spectra, order, consts, conv_u):
    bx, n, _ = u.shape
    n1 = 2 * n // HY_N2
    cbn = HY_WIDTH // HY_CB
    smem = pl.BlockSpec(memory_space=pltpu.SMEM)
    data = lambda col: pl.BlockSpec((2, n, HY_CB), lambda c, p, col=col: (p, 0, col + c))
    wrow = lambda rows, col: pl.BlockSpec((rows, HY_CB), lambda c, p, col=col: (0, col + c))
    ucol = u_col if conv_u else g_col
    const_names = ("f2", "f2i", "tw0r", "tw0i", "wgr", "wgi")
    dft = pltpu.VMEM((2 * HY_N2, 2 * HY_N2), BF16)
    return pl.pallas_call(
        functools.partial(_hyconv_kernel, conv_u=conv_u),
        grid=(cbn, bx // 2),
        in_specs=[smem, smem, data(u_col), data(g_col),
                  wrow(HY_SHORT, ucol), wrow(1, ucol), wrow(HY_SHORT, g_col), wrow(1, g_col),
                  pl.BlockSpec((1, n1, 2 * HY_N2, HY_CB), lambda c, p: (order, 0, 0, c))]
                 + [_full_spec(consts[k]) for k in const_names],
        out_specs=pl.BlockSpec((2, n, HY_CB), lambda c, p: (p, 0, c)),
        out_shape=jax.ShapeDtypeStruct((bx, n, HY_WIDTH), BF16),
        scratch_shapes=[pltpu.VMEM((n1, HY_N2, HY_CB), F32)] * 2
                       + [pltpu.VMEM((n + 2 * HALO, HY_CB), F32), dft, dft],
        compiler_params=_params("parallel", "arbitrary"),
        name="hyena_conv",
    )(consts["wr"], consts["wi"], u, g, conv_w, conv_b, conv_w, conv_b, spectra,
      *[consts[k] for k in const_names])


def _hy_consts(n):
    big_n = 2 * n
    n1 = big_n // HY_N2
    group = min(HY_GROUP, n1)
    q = np.arange(max(n1 // 2, 1), dtype=np.float64)
    ang1 = 2.0 * np.pi * q / n1
    idx = np.arange(HY_N2, dtype=np.float64)
    ang2 = 2.0 * np.pi * np.outer(idx, idx) / HY_N2
    c, s = np.cos(ang2), np.sin(ang2)
    f2 = np.block([[c, s], [-s, c]])
    f2i = np.block([[c, -s], [s, c]]) / big_n
    lane = np.ones((1, 1, HY_CB))
    ang0 = 2.0 * np.pi * np.arange(group)[:, None, None] * idx[None, :, None] / big_n
    angg = 2.0 * np.pi * group * idx[:, None] / big_n
    return {
        "wr": jnp.asarray(np.cos(ang1), F32), "wi": jnp.asarray(-np.sin(ang1), F32),
        "f2": jnp.asarray(f2, F32), "f2i": jnp.asarray(f2i, F32),
        "tw0r": jnp.asarray(np.cos(ang0) * lane, F32), "tw0i": jnp.asarray(-np.sin(ang0) * lane, F32),
        "wgr": jnp.asarray(np.cos(angg) * lane[0], F32), "wgi": jnp.asarray(-np.sin(angg) * lane[0], F32),
    }


def _hy_tables(n):
    t = np.linspace(0.0, 1.0, n, dtype=np.float32)[:, None].astype(np.float64)
    bands = (HY_EMB - 1) // 2
    w = 2.0 * np.pi * np.arange(n, dtype=np.float64) / n
    f = np.linspace(1e-4, bands - 1, bands, dtype=np.float32).astype(np.float64)
    ang = w[:, None] * f[None, :]
    z = np.concatenate([t, np.cos(ang), -np.sin(ang), np.zeros((n, HY_EMB_PAD - HY_EMB))], axis=-1)
    max_decay = math.log(HY_DECAY_TARGET) / HY_FAST_DECAY
    min_decay = math.log(HY_DECAY_TARGET) / HY_SLOW_DECAY
    deltas = np.abs(np.linspace(min_decay, max_decay, HY_WIDTH, dtype=np.float32).astype(np.float64))
    return jnp.asarray(z, F32), jnp.asarray(np.exp(-t * deltas), F32)


def _gelu_tanh(x):
    return x * (0.5 * (1.0 + jnp.tanh(math.sqrt(2.0 / math.pi) * (x + 0.044715 * (x * x * x)))))


def _merge_kernel(x_ref, sh_ref, sc_ref, gt_ref, g_ref, ya_ref, hf_ref, hb_ref, lg_ref, yc_ref,
                  wgate_ref, wa_ref, wb_ref, wc_ref, wo_ref, o_ref):
    x = x_ref[0]
    h = _norm_mod(x, g_ref[...], sh_ref[0], sc_ref[0]).astype(BF16)
    yb = ((hf_ref[0] + hb_ref[0]) * _gelu_tanh(lg_ref[0])).astype(BF16)
    y = None
    for j, (br, w_ref) in enumerate(((ya_ref[0], wa_ref), (yb, wb_ref), (yc_ref[0], wc_ref))):
        gate = jax.nn.sigmoid(_dot(h, wgate_ref[:, j * D_MODEL:(j + 1) * D_MODEL]))
        term = gate * _dot(br, w_ref[...])
        y = term if y is None else y + term
    o_ref[0] = x + gt_ref[0] * _dot(y.astype(BF16), wo_ref[...])


def _merge(x, mods, norm_g, ya, hf, hb, lg, yc, wgate, wa, wb, wc, wo):
    bx, t, _ = x.shape
    tm = TOKEN_TILE
    tile = lambda w: pl.BlockSpec((1, tm, w), lambda b, i: (b, i, 0))
    return pl.pallas_call(
        _merge_kernel,
        grid=(bx, t // tm),
        in_specs=[tile(D_MODEL), _mod_spec(mods, 0), _mod_spec(mods, 1), _mod_spec(mods, 2), _full_spec(norm_g),
                  tile(MLA_HEADS * V_HEAD), tile(LRU_WIDTH), tile(LRU_WIDTH), tile(LRU_WIDTH), tile(HY_WIDTH),
                  _full_spec(wgate), _full_spec(wa), _full_spec(wb), _full_spec(wc), _full_spec(wo)],
        out_specs=tile(D_MODEL),
        out_shape=jax.ShapeDtypeStruct(x.shape, F32),
        compiler_params=_params("parallel", "parallel"),
        name="merge",
    )(x, mods, mods, mods, norm_g, ya, hf, hb, lg, yc, wgate, wa, wb, wc, wo)


def _ffn_kernel(x_ref, sh_ref, sc_ref, gt_ref, g_ref, wg_ref, wu_ref, wd_ref, fg_ref, o_ref, *, final):
    x = x_ref[0]
    h = _norm_mod(x, g_ref[...], sh_ref[0], sc_ref[0]).astype(BF16)
    gate = _dot(h, wg_ref[...])
    act = (gate * jax.nn.sigmoid(gate) * _dot(h, wu_ref[...])).astype(BF16)
    y = x + gt_ref[0] * _dot(act, wd_ref[...])
    o_ref[0] = _rms(y, fg_ref[...]) if final else y


def _ffn(x, mods, norm_g, wg, wu, wd, final_g, final):
    bx, t, _ = x.shape
    tm = TOKEN_TILE
    tile = pl.BlockSpec((1, tm, D_MODEL), lambda b, i: (b, i, 0))
    return pl.pallas_call(
        functools.partial(_ffn_kernel, final=final),
        grid=(bx, t // tm),
        in_specs=[tile, _mod_spec(mods, 3), _mod_spec(mods, 4), _mod_spec(mods, 5), _full_spec(norm_g),
                  _full_spec(wg), _full_spec(wu), _full_spec(wd), _full_spec(final_g)],
        out_specs=tile,
        out_shape=jax.ShapeDtypeStruct(x.shape, F32),
        compiler_params=_params("parallel", "parallel"),
        name="ffn",
    )(x, mods, mods, mods, norm_g, wg, wu, wd, final_g)


_ROPE_SWAP = np.array([8, 9, 10, 11, 12, 13, 14, 15, 0, 1, 2, 3, 4, 5, 6, 7,
                       24, 25, 26, 27, 28, 29, 30, 31, 16, 17, 18, 19, 20, 21, 22, 23])


def _rope_tables(n):
    cos = np.zeros((n, HEAD_PAD))
    sin = np.zeros((n, HEAD_PAD))
    cos[:, :QK_NOPE + QK_ROPE] = 1.0
    if n % GRID_W == 0 and n > 0:
        pos = np.arange(n)
        seg = QK_ROPE // 2
        inv = 1.0 / (ROPE_BASE ** (np.arange(seg // 2, dtype=np.float64) * 2.0 / seg))
        for s, p in enumerate((pos // GRID_W, pos % GRID_W)):
            ang = p[:, None] * inv[None, :]
            base = QK_NOPE + s * seg
            cos[:, base:base + seg] = np.concatenate([np.cos(ang), np.cos(ang)], axis=-1)
            sin[:, base:base + seg] = np.concatenate([-np.sin(ang), np.sin(ang)], axis=-1)
    return jnp.asarray(cos, F32), jnp.asarray(sin, F32)


def _identity_rope_tables():
    cos = np.zeros((TOKEN_TILE, HEAD_PAD))
    cos[:, :QK_NOPE + QK_ROPE] = 1.0
    return jnp.asarray(cos, F32), jnp.zeros((TOKEN_TILE, HEAD_PAD), F32)


def _block_diag(w):
    g, i, j = w.shape
    return jnp.einsum("gij,gh->gihj", w, jnp.eye(g, dtype=w.dtype)).reshape(g * i, g * j)


def _prep_layer(p):
    w_in = p["w_in"]
    kpe = w_in[:, Q_LORA + KV_LORA:MLA_IN]
    z64 = jnp.zeros((D_MODEL, QK_NOPE), F32)
    z32 = jnp.zeros((D_MODEL, HEAD_PAD - QK_NOPE - QK_ROPE), F32)
    out = {
        "w_mla": jnp.concatenate([w_in[:, :Q_LORA + KV_LORA], z64, kpe, z32, z64, kpe[:, _ROPE_SWAP], z32],
                                 axis=1).astype(BF16),
        "w_lx": w_in[:, IN_SPLITS[0]:IN_SPLITS[1]].astype(BF16),
        "w_lg": w_in[:, IN_SPLITS[1]:IN_SPLITS[2]].astype(BF16),
        "w_hy": w_in[:, IN_SPLITS[2]:IN_SPLITS[3]].astype(BF16),
        "w_gate": w_in[:, IN_SPLITS[3]:].astype(BF16),
    }
    wq = p["w_uq"].reshape(Q_LORA, MLA_HEADS, QK_NOPE + QK_ROPE)
    pad = jnp.zeros((Q_LORA, MLA_HEADS, HEAD_PAD - QK_NOPE - QK_ROPE), F32)
    out["w_q"] = jnp.concatenate([wq, pad], axis=-1).reshape(Q_LORA, -1).astype(BF16)
    out["w_qs"] = jnp.concatenate([jnp.zeros((Q_LORA, MLA_HEADS, QK_NOPE), F32),
                                   wq[:, :, QK_NOPE:][:, :, _ROPE_SWAP], pad], axis=-1
                                  ).reshape(Q_LORA, -1).astype(BF16)
    wkv = p["w_ukv"].reshape(KV_LORA, MLA_HEADS, QK_NOPE + V_HEAD)
    out["w_k"] = jnp.concatenate([wkv[:, :, :QK_NOPE], jnp.zeros((KV_LORA, MLA_HEADS, HEAD_PAD - QK_NOPE), F32)],
                                 axis=-1).reshape(KV_LORA, -1).astype(BF16)
    out["w_v"] = wkv[:, :, QK_NOPE:].reshape(KV_LORA, -1).astype(BF16)
    for d, name in enumerate(("f", "b")):
        out["lru_w" + name] = jnp.concatenate([_block_diag(p["lru_wa"][d]), _block_diag(p["lru_wx"][d])],
                                              axis=1).astype(BF16)
        out["lru_bias" + name] = jnp.concatenate([p["lru_ba"][d], p["lru_bx"][d]])[None, :]
    out["hy_w1"] = jnp.concatenate([p["hy_w1"], jnp.zeros((HY_EMB_PAD - HY_EMB, HY_HID), F32)], axis=0)
    for name in ("w_br_a", "w_br_b", "w_br_c", "w_out", "ffn_w_gate", "ffn_w_up", "ffn_w_down"):
        out[name] = p[name].astype(BF16)
    return out


def _mixers(z_mla, lx, hy, p, w, rope, kv_ctx=None, h0=None, branch_out=True):
    bx, n, _ = lx.shape
    q, k, v = _mla_prep(z_mla, p["q_norm_g"][None], p["kv_norm_g"][None], w["w_q"], w["w_qs"], w["w_k"], w["w_v"],
                        *rope)
    if h0 is None:
        h0 = jnp.zeros((bx, 2, LRU_WIDTH), F32)
    hf, hb, hlast = _lru(lx, p["lru_conv_w"], p["lru_conv_b"][None], w["lru_wf"], w["lru_biasf"],
                         w["lru_wb"], w["lru_biasb"], p["lru_lam"], h0)
    if not branch_out:
        return None, (k, v), hlast
    if kv_ctx is None:
        ya = _flash(q, k, v)
    else:
        ya = _flash(q, kv_ctx[0], kv_ctx[1], k, v)
    z, decay = _hy_tables(n)
    consts = _hy_consts(n)
    filt = _hy_filters(z, w["hy_w1"], p["hy_b1"][None], p["hy_w2"], p["hy_b2"], p["hy_freq"][None],
                       p["hy_w_out"], decay)
    spectra = _hy_spectra(filt, p["hy_skip"], consts)
    cbn = HY_WIDTH // HY_CB
    cw, cb = p["hy_conv_w"], p["hy_conv_b"][None]
    y1 = _hy_conv(hy, 0, hy, cbn, cw, cb, spectra, 0, consts, True)
    yc = _hy_conv(y1, 0, hy, 2 * cbn, cw, cb, spectra, 1, consts, False)
    return (ya, hf, hb, yc), (k, v), hlast


def _layer(x, xc, mods_l, mods_c, p, final_g, last):
    bx, s, _ = x.shape
    sc = xc.shape[1]
    w = _prep_layer(p)
    n1g = p["norm1_g"][None]
    n2g = p["norm2_g"][None]
    flat = lambda a: a.reshape(1, bx * sc, a.shape[-1])
    unflat = lambda a: a.reshape(bx, sc, a.shape[-1])

    zc = [unflat(a) for a in _in_proj(flat(xc), mods_c, n1g, w["w_mla"], w["w_lx"], w["w_lg"], w["w_hy"])]
    zl = _in_proj(x, mods_l, n1g, w["w_mla"], w["w_lx"], w["w_lg"], w["w_hy"])

    br_c, kv_c, h_c = _mixers(zc[0], zc[1], zc[3], p, w, _identity_rope_tables(), branch_out=not last)
    br_l, _, _ = _mixers(zl[0], zl[1], zl[3], p, w, _rope_tables(s), kv_ctx=kv_c, h0=h_c)

    merge_w = (w["w_gate"], w["w_br_a"], w["w_br_b"], w["w_br_c"], w["w_out"])
    ffn_w = (w["ffn_w_gate"], w["ffn_w_up"], w["ffn_w_down"])
    x = _merge(x, mods_l, n1g, br_l[0], br_l[1], br_l[2], zl[2], br_l[3], *merge_w)
    x = _ffn(x, mods_l, n2g, *ffn_w, final_g, last)
    if not last:
        xcf = _merge(flat(xc), mods_c, n1g, flat(br_c[0]), flat(br_c[1]), flat(br_c[2]), flat(zc[2]),
                     flat(br_c[3]), *merge_w)
        xc = unflat(_ffn(xcf, mods_c, n2g, *ffn_w, final_g, False))
    return x, xc


def kernel(x, c, ctx, c_ctx, ada_w, ada_b, norm1_g, norm2_g, w_in, q_norm_g, w_uq, kv_norm_g, w_ukv,
           lru_conv_w, lru_conv_b, lru_wa, lru_ba, lru_wx, lru_bx, lru_lam,
           hy_conv_w, hy_conv_b, hy_w1, hy_b1, hy_w2, hy_b2, hy_freq, hy_w_out, hy_skip,
           w_br_a, w_br_b, w_br_c, w_out, ffn_w_gate, ffn_w_up, ffn_w_down, final_norm_g):
    depth = ada_w.shape[0]
    bx = x.shape[0]
    assert bx % 2 == 0 and bx + 1 <= 8
    cond = jnp.concatenate([c, c_ctx[None], jnp.zeros((8 - bx - 1, D_MODEL), F32)], axis=0)
    mods = _ada_mods(cond, ada_w, ada_b)
    stacked = dict(norm1_g=norm1_g, norm2_g=norm2_g, w_in=w_in, q_norm_g=q_norm_g, w_uq=w_uq,
                   kv_norm_g=kv_norm_g, w_ukv=w_ukv, lru_conv_w=lru_conv_w, lru_conv_b=lru_conv_b,
                   lru_wa=lru_wa, lru_ba=lru_ba, lru_wx=lru_wx, lru_bx=lru_bx, lru_lam=lru_lam,
                   hy_conv_w=hy_conv_w, hy_conv_b=hy_conv_b, hy_w1=hy_w1, hy_b1=hy_b1, hy_w2=hy_w2, hy_b2=hy_b2,
                   hy_freq=hy_freq, hy_w_out=hy_w_out, hy_skip=hy_skip, w_br_a=w_br_a, w_br_b=w_br_b,
                   w_br_c=w_br_c, w_out=w_out, ffn_w_gate=ffn_w_gate, ffn_w_up=ffn_w_up, ffn_w_down=ffn_w_down)
    xc = ctx
    fg = final_norm_g[None]
    for l in range(depth):
        p = {k: v[l] for k, v in stacked.items()}
        mods_l = mods[l, :bx][:, None, :]
        mods_c = mods[l, bx:bx + 1][:, None, :]
        x, xc = _layer(x, xc, mods_l, mods_c, p, fg, l == depth - 1)
    return x
```

```python
import functools
import math

import numpy as np
import jax
import jax.numpy as jnp
from jax import lax
from jax.experimental import pallas as pl
from jax.experimental.pallas import tpu as pltpu

F32 = jnp.float32
BF16 = jnp.bfloat16

D_MODEL = 1024
GRID_W = 64
EPS = 1e-6

MLA_HEADS = 8
Q_LORA = 384
KV_LORA = 256
QK_NOPE = 64
QK_ROPE = 32
V_HEAD = 64
ROPE_BASE = 10000.0
SM_SCALE = (QK_NOPE + QK_ROPE) ** -0.5
HEAD_PAD = 128
MLA_Z = Q_LORA + KV_LORA + 2 * HEAD_PAD

LRU_WIDTH = 512
LRU_BLOCKS = 8
LRU_CONV = 4
LRU_C = 8.0

HY_WIDTH = 512
HY_ORDER = 2
HY_SHORT = 3
HY_EMB = 33
HY_EMB_PAD = 40
HY_HID = 64
HY_INNER = 2
HY_FAST_DECAY = 0.3
HY_SLOW_DECAY = 1.5
HY_DECAY_TARGET = 1e-2
HY_N2 = 128
HY_CB = 128
HY_GROUP = 4
HY_CH_GROUP = 8

FFN_HID = 2816
N_BRANCH = 3
MLA_IN = Q_LORA + KV_LORA + QK_ROPE
IN_SPLITS = (MLA_IN, MLA_IN + LRU_WIDTH, MLA_IN + 2 * LRU_WIDTH, MLA_IN + 2 * LRU_WIDTH + 3 * HY_WIDTH)

Q_PRESCALE = SM_SCALE * math.log2(math.e)
FLASH_Q_TILE = 1024
FLASH_KEY_CHUNK = 1024
TOKEN_TILE = 256
HALO = 8
VMEM_LIMIT = 56 * 1024 * 1024


def _params(*sem):
    return pltpu.CompilerParams(dimension_semantics=sem, vmem_limit_bytes=VMEM_LIMIT)


def _dot(a, b):
    return jnp.dot(a, b, preferred_element_type=F32)


def _rms(x, g):
    return x * lax.rsqrt(jnp.mean(x * x, axis=-1, keepdims=True) + EPS) * g


def _norm_mod(x, g, shift, scale):
    return _rms(x, g) * (1.0 + scale) + shift


def _full_spec(arr):
    nd = arr.ndim
    return pl.BlockSpec(arr.shape, lambda *_: (0,) * nd)


def _mod_spec(mods, j):
    if mods.shape[0] > 1:
        return pl.BlockSpec((1, 1, D_MODEL), lambda b, i, j=j: (b, 0, j))
    return pl.BlockSpec((1, 1, D_MODEL), lambda b, i, j=j: (0, 0, j))


def _ada_kernel(c_ref, w_ref, b_ref, o_ref):
    c = c_ref[...]
    s = (c * jax.nn.sigmoid(c)).astype(BF16)
    o_ref[0] = _dot(s, w_ref[0].astype(BF16)) + b_ref[0]


def _ada_mods(cond, ada_w, ada_b):
    depth, _, width = ada_w.shape
    tn = 1536
    return pl.pallas_call(
        _ada_kernel,
        grid=(depth, width // tn),
        in_specs=[pl.BlockSpec((8, D_MODEL), lambda l, j: (0, 0)),
                  pl.BlockSpec((1, D_MODEL, tn), lambda l, j: (l, 0, j)),
                  pl.BlockSpec((1, 1, tn), lambda l, j: (l, 0, j))],
        out_specs=pl.BlockSpec((1, 8, tn), lambda l, j: (l, 0, j)),
        out_shape=jax.ShapeDtypeStruct((depth, 8, width), F32),
        compiler_params=_params("arbitrary", "arbitrary"),
        name="ada_mods",
    )(cond, ada_w, ada_b.reshape(depth, 1, width))


def _inproj_kernel(x_ref, sh_ref, sc_ref, g_ref, wm_ref, wx_ref, wg_ref, wh_ref,
                   zm_ref, lx_ref, lg_ref, hy_ref):
    h = _norm_mod(x_ref[0], g_ref[...], sh_ref[0], sc_ref[0]).astype(BF16)
    zm_ref[0] = _dot(h, wm_ref[...])
    lx_ref[0] = _dot(h, wx_ref[...])
    lg_ref[0] = _dot(h, wg_ref[...])
    hy_ref[0] = _dot(h, wh_ref[...])


def _in_proj(x, mods, norm_g, wm, wx, wg, wh):
    bx, t, _ = x.shape
    tm = TOKEN_TILE
    tile = lambda w: pl.BlockSpec((1, tm, w), lambda b, i: (b, i, 0))
    widths = (MLA_Z, LRU_WIDTH, LRU_WIDTH, 3 * HY_WIDTH)
    return pl.pallas_call(
        _inproj_kernel,
        grid=(bx, t // tm),
        in_specs=[tile(D_MODEL), _mod_spec(mods, 0), _mod_spec(mods, 1), _full_spec(norm_g),
                  _full_spec(wm), _full_spec(wx), _full_spec(wg), _full_spec(wh)],
        out_specs=[tile(w) for w in widths],
        out_shape=[jax.ShapeDtypeStruct((bx, t, w), F32) for w in widths],
        compiler_params=_params("parallel", "parallel"),
        name="in_proj",
    )(x, mods, mods, norm_g, wm, wx, wg, wh)


def _mla_prep_kernel(z_ref, qg_ref, kvg_ref, wq_ref, wqs_ref, wk_ref, wv_ref, cos_ref, sin_ref,
                     q_ref, k_ref, v_ref):
    z = z_ref[0]
    nq = _rms(z[:, :Q_LORA], qg_ref[...]).astype(BF16)
    nkv = _rms(z[:, Q_LORA:Q_LORA + KV_LORA], kvg_ref[...]).astype(BF16)
    pe = z[:, Q_LORA + KV_LORA:Q_LORA + KV_LORA + HEAD_PAD]
    pes = z[:, Q_LORA + KV_LORA + HEAD_PAD:]
    cos = cos_ref[...]
    sin = sin_ref[...]
    q = _dot(nq, wq_ref[...])
    qs = _dot(nq, wqs_ref[...])
    kn = _dot(nkv, wk_ref[...])
    k_pe = pe * cos + pes * sin
    for h in range(MLA_HEADS):
        sl = slice(h * HEAD_PAD, (h + 1) * HEAD_PAD)
        q_ref[0, :, sl] = ((q[:, sl] * cos + qs[:, sl] * sin) * Q_PRESCALE).astype(BF16)
        k_ref[0, :, sl] = (kn[:, sl] + k_pe).astype(BF16)
    v_ref[0] = _dot(nkv, wv_ref[...]).astype(BF16)


def _mla_prep(z, qg, kvg, wq, wqs, wk, wv, cos, sin):
    bx, t, _ = z.shape
    tm = TOKEN_TILE
    tile = lambda w: pl.BlockSpec((1, tm, w), lambda b, i: (b, i, 0))
    if cos.shape[0] == tm:
        tab = pl.BlockSpec((tm, HEAD_PAD), lambda b, i: (0, 0))
    else:
        tab = pl.BlockSpec((tm, HEAD_PAD), lambda b, i: (i, 0))
    widths = (MLA_HEADS * HEAD_PAD, MLA_HEADS * HEAD_PAD, MLA_HEADS * V_HEAD)
    return pl.pallas_call(
        _mla_prep_kernel,
        grid=(bx, t // tm),
        in_specs=[tile(MLA_Z), _full_spec(qg), _full_spec(kvg), _full_spec(wq), _full_spec(wqs),
                  _full_spec(wk), _full_spec(wv), tab, tab],
        out_specs=[tile(w) for w in widths],
        out_shape=[jax.ShapeDtypeStruct((bx, t, w), BF16) for w in widths],
        compiler_params=_params("parallel", "parallel"),
        name="mla_prep",
    )(z, qg, kvg, wq, wqs, wk, wv, cos, sin)


def _qk(q, k):
    return lax.dot_general(q, k, (((1,), (1,)), ((), ())), preferred_element_type=F32)


def _flash_kernel(*refs, chunks):
    q_ref, o_ref = refs[0], refs[-1]
    kv = refs[1:-1]
    state = [None, None]
    for src, off, size in chunks:
        k_ref, v_ref = kv[2 * src], kv[2 * src + 1]
        v = v_ref[0, off:off + size, :]
        for h in range(2):
            sl = slice(h * HEAD_PAD, (h + 1) * HEAD_PAD)
            s = _qk(q_ref[0, :, sl], k_ref[0, off:off + size, sl])
            m_blk = jnp.max(s, axis=-1, keepdims=True)
            if state[h] is None:
                m = m_blk
                p = jnp.exp2(s - m)
                l = jnp.sum(p, axis=-1, keepdims=True)
                acc = _dot(p.astype(BF16), v)
            else:
                m_old, l, acc = state[h]
                m = jnp.maximum(m_old, m_blk)
                alpha = jnp.exp2(m_old - m)
                p = jnp.exp2(s - m)
                l = alpha * l + jnp.sum(p, axis=-1, keepdims=True)
                acc = alpha * acc + _dot(p.astype(BF16), v)
            state[h] = (m, l, acc)
    outs = [acc / l for _, l, acc in state]
    lane = lax.broadcasted_iota(jnp.int32, outs[0].shape, 1)
    o_ref[0] = jnp.where(lane < V_HEAD, outs[0], outs[1]).astype(o_ref.dtype)


def _flash(q, kc, vc, kl=None, vl=None):
    bx, t, _ = q.shape
    tq = min(FLASH_Q_TILE, t)
    sc = kc.shape[1]
    pairs = MLA_HEADS // 2
    qspec = pl.BlockSpec((1, tq, 2 * HEAD_PAD), lambda b, hp, i: (b, i, hp))
    kspec = lambda n: pl.BlockSpec((1, n, 2 * HEAD_PAD), lambda b, hp, i: (b, 0, hp))
    vspec = lambda n: pl.BlockSpec((1, n, 2 * V_HEAD), lambda b, hp, i: (b, 0, hp))
    in_specs = [qspec, kspec(sc), vspec(sc)]
    args = [q, kc, vc]
    chunks = [(0, 0, sc)]
    if kl is not None:
        sl = kl.shape[1]
        size = min(FLASH_KEY_CHUNK, sl)
        chunks += [(1, off, size) for off in range(0, sl, size)]
        in_specs += [kspec(sl), vspec(sl)]
        args += [kl, vl]
    return pl.pallas_call(
        functools.partial(_flash_kernel, chunks=tuple(chunks)),
        grid=(bx, pairs, t // tq),
        in_specs=in_specs,
        out_specs=pl.BlockSpec((1, tq, 2 * V_HEAD), lambda b, hp, i: (b, i, hp)),
        out_shape=jax.ShapeDtypeStruct((bx, t, MLA_HEADS * V_HEAD), BF16),
        compiler_params=_params("parallel", "parallel", "arbitrary"),
        name="flash",
    )(*args)


def _lru_kernel(xf_ref, xfp_ref, xfn_ref, xb_ref, xbp_ref, xbn_ref, cw_ref, cb_ref,
                wf_ref, bf_ref, wb_ref, bb_ref, lam_ref, h0_ref,
                hf_ref, hb_ref, hl_ref, af_s, bfw_s, ab_s, bbw_s, pad_s, carry_s):
    t = pl.program_id(0)
    nt = pl.num_programs(0)
    bx, tm, _ = xf_ref.shape

    @pl.when(t == 0)
    def _():
        carry_s[...] = h0_ref[...]

    def coeffs(b, x_ref, prev_ref, next_ref, tile, w_ref, bias_ref, lam, a_s, b_s):
        x = x_ref[b]
        pad_s[0:HALO, :] = jnp.where(tile > 0, prev_ref[b], 0.0)
        pad_s[HALO:HALO + tm, :] = x
        pad_s[HALO + tm:, :] = jnp.where(tile < nt - 1, next_ref[b], 0.0)
        u = (cw_ref[0:1, :] * pad_s[HALO - 2:HALO - 2 + tm, :] + cw_ref[1:2, :] * pad_s[HALO - 1:HALO - 1 + tm, :]
             + cw_ref[2:3, :] * x + cw_ref[3:4, :] * pad_s[HALO + 1:HALO + 1 + tm, :] + cb_ref[...])
        y = _dot(u.astype(BF16), w_ref[...]) + bias_ref[...]
        r = jax.nn.sigmoid(y[:, :LRU_WIDTH])
        i = jax.nn.sigmoid(y[:, LRU_WIDTH:])
        neg = -lam
        softplus = jnp.maximum(neg, 0.0) + jnp.log1p(jnp.exp(-jnp.abs(neg)))
        a = jnp.exp((-LRU_C) * r * softplus)
        a_s[b] = a
        b_s[b] = jnp.sqrt(1.0 - a * a) * (i * u)

    for b in range(bx):
        coeffs(b, xf_ref, xfp_ref, xfn_ref, t, wf_ref, bf_ref, lam_ref[0:1, :], af_s, bfw_s)
        coeffs(b, xb_ref, xbp_ref, xbn_ref, nt - 1 - t, wb_ref, bb_ref, lam_ref[1:2, :], ab_s, bbw_s)

    def body(j, hs):
        jb = tm - 1 - j
        out = []
        for b in range(bx):
            hf = af_s[b, pl.ds(j, 1), :] * hs[2 * b] + bfw_s[b, pl.ds(j, 1), :]
            hf_ref[b, pl.ds(j, 1), :] = hf
            hb = ab_s[b, pl.ds(jb, 1), :] * hs[2 * b + 1] + bbw_s[b, pl.ds(jb, 1), :]
            hb_ref[b, pl.ds(jb, 1), :] = hb
            out += [hf, hb]
        return tuple(out)

    init = tuple(carry_s[b, d:d + 1, :] for b in range(bx) for d in range(2))
    hs = lax.fori_loop(0, tm, body, init, unroll=8)
    for b in range(bx):
        for d in range(2):
            carry_s[b, d:d + 1, :] = hs[2 * b + d]
    hl_ref[...] = carry_s[...]


def _lru(lx, conv_w, conv_b, wf, biasf, wb, biasb, lam, h0):
    bx, t, w = lx.shape
    tm = TOKEN_TILE
    nt = t // tm
    hb_per = tm // HALO
    last_halo = t // HALO - 1
    fwd = lambda i: (0, i, 0)
    bwd = lambda i: (0, nt - 1 - i, 0)
    prev_of = lambda f: (lambda i: (0, jnp.maximum(f(i)[1] * hb_per - 1, 0), 0))
    next_of = lambda f: (lambda i: (0, jnp.minimum((f(i)[1] + 1) * hb_per, last_halo), 0))
    tile = lambda f: pl.BlockSpec((bx, tm, w), f)
    halo = lambda f: pl.BlockSpec((bx, HALO, w), f)
    state = pl.BlockSpec((bx, 2, w), lambda i: (0, 0, 0))
    return pl.pallas_call(
        _lru_kernel,
        grid=(nt,),
        in_specs=[tile(fwd), halo(prev_of(fwd)), halo(next_of(fwd)),
                  tile(bwd), halo(prev_of(bwd)), halo(next_of(bwd)),
                  _full_spec(conv_w), _full_spec(conv_b), _full_spec(wf), _full_spec(biasf),
                  _full_spec(wb), _full_spec(biasb), _full_spec(lam), state],
        out_specs=[tile(fwd), tile(bwd), state],
        out_shape=[jax.ShapeDtypeStruct((bx, t, w), F32), jax.ShapeDtypeStruct((bx, t, w), F32),
                   jax.ShapeDtypeStruct((bx, 2, w), F32)],
        scratch_shapes=[pltpu.VMEM((bx, tm, w), F32)] * 4
                       + [pltpu.VMEM((tm + 2 * HALO, w), F32), pltpu.VMEM((bx, 2, w), F32)],
        compiler_params=_params("arbitrary"),
        name="rglru",
    )(lx, lx, lx, lx, lx, lx, conv_w, conv_b, wf, biasf, wb, biasb, lam, h0)


def _hyfilt_kernel(z_ref, w1_ref, b1_ref, w2_ref, b2_ref, fr_ref, wo_ref, dec_ref, o_ref):
    hi = lax.Precision.HIGHEST
    fr = fr_ref[...]
    h = jnp.sin(fr * (jnp.dot(z_ref[...], w1_ref[...], precision=hi, preferred_element_type=F32) + b1_ref[...]))
    for j in range(HY_INNER):
        h = jnp.sin(fr * (jnp.dot(h, w2_ref[j], precision=hi, preferred_element_type=F32) + b2_ref[j:j + 1, :]))
    dec = dec_ref[...]
    for g in range(2 * HY_ORDER):
        sl = slice(g * HY_WIDTH, (g + 1) * HY_WIDTH)
        o_ref[:, sl] = jnp.dot(h, wo_ref[:, sl], precision=hi, preferred_element_type=F32) * dec


def _hy_filters(z, w1, b1, w2, b2, freq, w_out, decay):
    n = z.shape[0]
    tn = min(n, 512)
    width = 2 * HY_ORDER * HY_WIDTH
    return pl.pallas_call(
        _hyfilt_kernel,
        grid=(n // tn,),
        in_specs=[pl.BlockSpec((tn, HY_EMB_PAD), lambda i: (i, 0)), _full_spec(w1), _full_spec(b1),
                  _full_spec(w2), _full_spec(b2), _full_spec(freq), _full_spec(w_out),
                  pl.BlockSpec((tn, HY_WIDTH), lambda i: (i, 0))],
        out_specs=pl.BlockSpec((tn, width), lambda i: (i, 0)),
        out_shape=jax.ShapeDtypeStruct((n, width), F32),
        compiler_params=_params("parallel"),
        name="hyena_filters",
    )(z, w1, b1, w2, b2, freq, w_out, decay)


def _bitrev(k, bits):
    r = jnp.zeros_like(k)
    for b in range(bits):
        r = r | (((k >> b) & 1) << (bits - 1 - b))
    return r


def _fft_lead_fwd(sre, sim, wr_ref, wi_ref, n1):
    half = n1 // 2
    m = half
    first = True
    while m >= 1:
        shift = int(math.log2(m))
        stride = half // m

        def body(q, c, m=m, shift=shift, stride=stride, first=first):
            grp = q >> shift
            j = q - (grp << shift)
            i0 = (grp << (shift + 1)) + j
            i1 = i0 + m
            wr = wr_ref[j * stride]
            wi = wi_ref[j * stride]
            ar, ai = sre[i0], sim[i0]
            if first:
                dr, di = ar, ai
            else:
                br, bi = sre[i1], sim[i1]
                sre[i0] = ar + br
                sim[i0] = ai + bi
                dr, di = ar - br, ai - bi
            sre[i1] = dr * wr - di * wi
            sim[i1] = dr * wi + di * wr
            return c

        lax.fori_loop(0, half, body, 0)
        first = False
        m //= 2


def _fft_lead_inv(sre, sim, wr_ref, wi_ref, n1):
    half = n1 // 2
    m = 1
    while m <= half:
        shift = int(math.log2(m))
        stride = half // m
        last = m == half

        def body(q, c, m=m, shift=shift, stride=stride, last=last):
            grp = q >> shift
            j = q - (grp << shift)
            i0 = (grp << (shift + 1)) + j
            i1 = i0 + m
            wr = wr_ref[j * stride]
            wi = wi_ref[j * stride]
            ar, ai = sre[i0], sim[i0]
            br, bi = sre[i1], sim[i1]
            tr = br * wr + bi * wi
            ti = bi * wr - br * wi
            sre[i0] = ar + tr
            sim[i0] = ai + ti
            if not last:
                sre[i1] = ar - tr
                sim[i1] = ai - ti
            return c

        lax.fori_loop(0, half, body, 0)
        m *= 2


def _spectrum_loop(sre, sim, f2_s, tw, n1, emit):
    tw0r_ref, tw0i_ref, wgr_ref, wgi_ref = tw
    bits = int(math.log2(n1))
    group = min(HY_GROUP, n1)

    def body(kb, tws):
        ks, xs = [], []
        for g in range(group):
            k1 = kb * group + g
            blk = _bitrev(k1, bits)
            tr, ti = tws[2 * g], tws[2 * g + 1]
            ar, ai = sre[blk], sim[blk]
            xs.append(jnp.concatenate([ar * tr - ai * ti, ar * ti + ai * tr], axis=0).astype(BF16))
            ks.append((k1, blk))
        b2 = _dot(f2_s[...], jnp.concatenate(xs, axis=1))
        emit(ks, b2, tws)
        wgr, wgi = wgr_ref[...], wgi_ref[...]
        nxt = []
        for g in range(group):
            tr, ti = tws[2 * g], tws[2 * g + 1]
            nxt += [tr * wgr - ti * wgi, tr * wgi + ti * wgr]
        return tuple(nxt)

    init = tuple(r[g] for g in range(group) for r in (tw0r_ref, tw0i_ref))
    lax.fori_loop(0, n1 // group, body, init)


def _group_cols(b2, g, cb):
    return b2[:HY_N2, g * cb:(g + 1) * cb], b2[HY_N2:, g * cb:(g + 1) * cb]


def _hyspec_kernel(wr_ref, wi_ref, hf_ref, hb_ref, skip_ref, f2_ref, tw0r_ref, tw0i_ref, wgr_ref, wgi_ref,
                   k_ref, sre, sim, f2_s):
    n = hf_ref.shape[0]
    n1 = 2 * n // HY_N2
    half = n1 // 2
    cb = hf_ref.shape[1]
    f2_s[...] = f2_ref[...].astype(BF16)
    for direction, h_ref in enumerate((hf_ref, hb_ref)):
        sre[0:half] = h_ref[...].reshape(half, HY_N2, cb)
        sim[0:half] = jnp.zeros((half, HY_N2, cb), F32)
        _fft_lead_fwd(sre, sim, wr_ref, wi_ref, n1)

        def emit(ks, b2, tws, direction=direction):
            for g, (k1, _) in enumerate(ks):
                br, bi = _group_cols(b2, g, cb)
                if direction == 0:
                    k_ref[0, k1, 0:HY_N2, :] = br + skip_ref[0]
                    k_ref[0, k1, HY_N2:, :] = bi
                else:
                    k_ref[0, k1, 0:HY_N2, :] = k_ref[0, k1, 0:HY_N2, :] + br
                    k_ref[0, k1, HY_N2:, :] = k_ref[0, k1, HY_N2:, :] - bi

        _spectrum_loop(sre, sim, f2_s, (tw0r_ref, tw0i_ref, wgr_ref, wgi_ref), n1, emit)


def _hy_spectra(filt, skip, consts):
    n = filt.shape[0]
    n1 = 2 * n // HY_N2
    cbn = HY_WIDTH // HY_CB
    smem = pl.BlockSpec(memory_space=pltpu.SMEM)
    col = lambda direction: (lambda o, c: (0, (direction * HY_ORDER + o) * cbn + c))
    tw_names = ("tw0r", "tw0i", "wgr", "wgi")
    return pl.pallas_call(
        _hyspec_kernel,
        grid=(HY_ORDER, cbn),
        in_specs=[smem, smem,
                  pl.BlockSpec((n, HY_CB), col(0)), pl.BlockSpec((n, HY_CB), col(1)),
                  pl.BlockSpec((1, 1, HY_CB), lambda o, c: (o, 0, c)),
                  _full_spec(consts["f2"])] + [_full_spec(consts[k]) for k in tw_names],
        out_specs=pl.BlockSpec((1, n1, 2 * HY_N2, HY_CB), lambda o, c: (o, 0, 0, c)),
        out_shape=jax.ShapeDtypeStruct((HY_ORDER, n1, 2 * HY_N2, HY_WIDTH), F32),
        scratch_shapes=[pltpu.VMEM((n1, HY_N2, HY_CB), F32)] * 2 + [pltpu.VMEM((2 * HY_N2, 2 * HY_N2), BF16)],
        compiler_params=_params("parallel", "parallel"),
        name="hyena_spectra",
    )(consts["wr"], consts["wi"], filt, filt, skip[:, None, :], consts["f2"], *[consts[k] for k in tw_names])


def _short_conv3(x, pad_s, w_ref, b_ref):
    n = x.shape[0]
    pad_s[HALO:HALO + n, :] = x
    return (w_ref[0:1, :] * pad_s[HALO - 1:HALO - 1 + n, :] + w_ref[1:2, :] * x
            + w_ref[2:3, :] * pad_s[HALO + 1:HALO + 1 + n, :] + b_ref[...])


def _hyconv_kernel(wr_ref, wi_ref, u_ref, g_ref, ucw_ref, ucb_ref, gcw_ref, gcb_ref, k_ref,
                   f2_ref, f2i_ref, tw0r_ref, tw0i_ref, wgr_ref, wgi_ref, o_ref,
                   sre, sim, pad_s, f2_s, f2i_s, *, conv_u):
    n = u_ref.shape[1]
    cb = u_ref.shape[2]
    n1 = 2 * n // HY_N2
    half = n1 // 2
    f2_s[...] = f2_ref[...].astype(BF16)
    f2i_s[...] = f2i_ref[...].astype(BF16)
    margin = jnp.zeros((HALO, cb), F32)
    pad_s[0:HALO, :] = margin
    pad_s[HALO + n:, :] = margin

    for b, s in enumerate((sre, sim)):
        u = u_ref[b].astype(F32)
        if conv_u:
            u = _short_conv3(u, pad_s, ucw_ref, ucb_ref)
        s[0:half] = u.reshape(half, HY_N2, cb)
    _fft_lead_fwd(sre, sim, wr_ref, wi_ref, n1)

    def emit(ks, b2, tws):
        ps = []
        for g, (k1, _) in enumerate(ks):
            br, bi = _group_cols(b2, g, cb)
            kr = k_ref[0, k1, 0:HY_N2, :]
            ki = k_ref[0, k1, HY_N2:, :]
            ps.append(jnp.concatenate([br * kr - bi * ki, br * ki + bi * kr], axis=0).astype(BF16))
        c2 = _dot(f2i_s[...], jnp.concatenate(ps, axis=1))
        for g, (_, blk) in enumerate(ks):
            cr, ci = _group_cols(c2, g, cb)
            tr, ti = tws[2 * g], tws[2 * g + 1]
            sre[blk] = cr * tr + ci * ti
            sim[blk] = ci * tr - cr * ti

    _spectrum_loop(sre, sim, f2_s, (tw0r_ref, tw0i_ref, wgr_ref, wgi_ref), n1, emit)
    _fft_lead_inv(sre, sim, wr_ref, wi_ref, n1)

    for b, s in enumerate((sre, sim)):
        gate = _short_conv3(g_ref[b], pad_s, gcw_ref, gcb_ref)
        o_ref[b] = (gate * s[0:half].reshape(n, cb)).astype(o_ref.dtype)


def _hy_conv(u, u_col, g, g_col, conv_w, conv_b, spectra, order, consts, conv_u):
    bx, n, _ = u.shape
    n1 = 2 * n // HY_N2
    cbn = HY_WIDTH // HY_CB
    smem = pl.BlockSpec(memory_space=pltpu.SMEM)
    data = lambda col: pl.BlockSpec((2, n, HY_CB), lambda c, p, col=col: (p, 0, col + c))
    wrow = lambda rows, col: pl.BlockSpec((rows, HY_CB), lambda c, p, col=col: (0, col + c))
    ucol = u_col if conv_u else g_col
    const_names = ("f2", "f2i", "tw0r", "tw0i", "wgr", "wgi")
    dft = pltpu.VMEM((2 * HY_N2, 2 * HY_N2), BF16)
    return pl.pallas_call(
        functools.partial(_hyconv_kernel, conv_u=conv_u),
        grid=(cbn, bx // 2),
        in_specs=[smem, smem, data(u_col), data(g_col),
                  wrow(HY_SHORT, ucol), wrow(1, ucol), wrow(HY_SHORT, g_col), wrow(1, g_col),
                  pl.BlockSpec((1, n1, 2 * HY_N2, HY_CB), lambda c, p: (order, 0, 0, c))]
                 + [_full_spec(consts[k]) for k in const_names],
        out_specs=pl.BlockSpec((2, n, HY_CB), lambda c, p: (p, 0, c)),
        out_shape=jax.ShapeDtypeStruct((bx, n, HY_WIDTH), BF16),
        scratch_shapes=[pltpu.VMEM((n1, HY_N2, HY_CB), F32)] * 2
                       + [pltpu.VMEM((n + 2 * HALO, HY_CB), F32), dft, dft],
        compiler_params=_params("parallel", "arbitrary"),
        name="hyena_conv",
    )(consts["wr"], consts["wi"], u, g, conv_w, conv_b, conv_w, conv_b, spectra,
      *[consts[k] for k in const_names])


def _hy_consts(n):
    big_n = 2 * n
    n1 = big_n // HY_N2
    group = min(HY_GROUP, n1)
    q = np.arange(max(n1 // 2, 1), dtype=np.float64)
    ang1 = 2.0 * np.pi * q / n1
    idx = np.arange(HY_N2, dtype=np.float64)
    ang2 = 2.0 * np.pi * np.outer(idx, idx) / HY_N2
    c, s = np.cos(ang2), np.sin(ang2)
    f2 = np.block([[c, s], [-s, c]])
    f2i = np.block([[c, -s], [s, c]]) / big_n
    lane = np.ones((1, 1, HY_CB))
    ang0 = 2.0 * np.pi * np.arange(group)[:, None, None] * idx[None, :, None] / big_n
    angg = 2.0 * np.pi * group * idx[:, None] / big_n
    return {
        "wr": jnp.asarray(np.cos(ang1), F32), "wi": jnp.asarray(-np.sin(ang1), F32),
        "f2": jnp.asarray(f2, F32), "f2i": jnp.asarray(f2i, F32),
        "tw0r": jnp.asarray(np.cos(ang0) * lane, F32), "tw0i": jnp.asarray(-np.sin(ang0) * lane, F32),
        "wgr": jnp.asarray(np.cos(angg) * lane[0], F32), "wgi": jnp.asarray(-np.sin(angg) * lane[0], F32),
    }


def _hy_tables(n):
    t = np.linspace(0.0, 1.0, n, dtype=np.float32)[:, None].astype(np.float64)
    bands = (HY_EMB - 1) // 2
    w = 2.0 * np.pi * np.arange(n, dtype=np.float64) / n
    f = np.linspace(1e-4, bands - 1, bands, dtype=np.float32).astype(np.float64)
    ang = w[:, None] * f[None, :]
    z = np.concatenate([t, np.cos(ang), -np.sin(ang), np.zeros((n, HY_EMB_PAD - HY_EMB))], axis=-1)
    max_decay = math.log(HY_DECAY_TARGET) / HY_FAST_DECAY
    min_decay = math.log(HY_DECAY_TARGET) / HY_SLOW_DECAY
    deltas = np.abs(np.linspace(min_decay, max_decay, HY_WIDTH, dtype=np.float32).astype(np.float64))
    return jnp.asarray(z, F32), jnp.asarray(np.exp(-t * deltas), F32)


def _cm_short_conv(x, taps, masks):
    first_lane, last_lane, first_row, last_row = masks
    n1h = x.shape[0]
    r = pltpu.roll(x, 1, axis=1)
    prev = jnp.where(first_lane, jnp.where(first_row, 0.0, pltpu.roll(r, 1, axis=0)), r)
    l = pltpu.roll(x, HY_N2 - 1, axis=1)
    nxt = jnp.where(last_lane, jnp.where(last_row, 0.0, pltpu.roll(l, n1h - 1, axis=0)), l)
    return taps[0] * prev + taps[1] * x + taps[2] * nxt + taps[3]


def _cm_masks(n1h):
    lane = lax.broadcasted_iota(jnp.int32, (n1h, HY_N2), 1)
    row = lax.broadcasted_iota(jnp.int32, (n1h, HY_N2), 0)
    return lane == 0, lane == HY_N2 - 1, row == 0, row == n1h - 1


def _cm_taps(cw_ref, cb_ref, ch):
    width = 3 * HY_WIDTH
    return cw_ref[ch], cw_ref[width + ch], cw_ref[2 * width + ch], cb_ref[ch]


def _cm_twiddle(a2, n1, twr, twi, conj):
    out = []
    for h in range(2):
        ar = a2[:n1, h * HY_N2:(h + 1) * HY_N2]
        ai = a2[n1:, h * HY_N2:(h + 1) * HY_N2]
        if conj:
            out.append((ar * twr + ai * twi, ai * twr - ar * twi))
        else:
            out.append((ar * twr - ai * twi, ar * twi + ai * twr))
    return out


def _hycm_spec_kernel(skip_ref, hf_ref, hb_ref, f1r_ref, f2t_ref, twr_ref, twi_ref, k_ref, f1_s, f2t_s):
    order, cblk = pl.program_id(0), pl.program_id(1)
    cb, n1h, _ = hf_ref.shape
    n1 = 2 * n1h
    f1_s[...] = f1r_ref[...].astype(BF16)
    f2t_s[...] = f2t_ref[...].astype(BF16)
    twr, twi = twr_ref[...], twi_ref[...]

    def group(gi, carry):
        base = gi * HY_CH_GROUP
        blocks = []
        for j in range(HY_CH_GROUP):
            x2 = jnp.concatenate([hf_ref[base + j], hb_ref[base + j]], axis=1).astype(BF16)
            for re, im in _cm_twiddle(_dot(f1_s[...], x2), n1, twr, twi, False):
                blocks.append(jnp.concatenate([re, im], axis=1).astype(BF16))
        b2 = _dot(jnp.concatenate(blocks, axis=0), f2t_s[...])
        for j in range(HY_CH_GROUP):
            bf = b2[(2 * j) * n1:(2 * j + 1) * n1]
            bb = b2[(2 * j + 1) * n1:(2 * j + 2) * n1]
            skip = skip_ref[order * HY_WIDTH + cblk * cb + base + j]
            k_ref[0, base + j] = jnp.concatenate([bf[:, :HY_N2] + bb[:, :HY_N2] + skip,
                                                  bf[:, HY_N2:] - bb[:, HY_N2:]], axis=1)
        return carry

    lax.fori_loop(0, cb // HY_CH_GROUP, group, 0)


def _hycm_spectra(filt_cm, skip, consts):
    _, n1h, _ = filt_cm.shape
    n1 = 2 * n1h
    nblk = HY_WIDTH // HY_CB
    smem = pl.BlockSpec(memory_space=pltpu.SMEM)
    blk = lambda direction: pl.BlockSpec((HY_CB, n1h, HY_N2),
                                         lambda o, c: ((direction * HY_ORDER + o) * nblk + c, 0, 0))
    names = ("f1r", "f2t", "twr", "twi")
    return pl.pallas_call(
        _hycm_spec_kernel,
        grid=(HY_ORDER, nblk),
        in_specs=[smem, blk(0), blk(1)] + [_full_spec(consts[k]) for k in names],
        out_specs=pl.BlockSpec((1, HY_CB, n1, 2 * HY_N2), lambda o, c: (o, c, 0, 0)),
        out_shape=jax.ShapeDtypeStruct((HY_ORDER, HY_WIDTH, n1, 2 * HY_N2), F32),
        scratch_shapes=[pltpu.VMEM((2 * n1, n1h), BF16), pltpu.VMEM((2 * HY_N2, 2 * HY_N2), BF16)],
        compiler_params=_params("parallel", "parallel"),
        name="hyena_spectra_cm",
    )(skip.reshape(-1), filt_cm, filt_cm, *[consts[k] for k in names])


def _hycm_conv_kernel(cw_ref, cb_ref, u_ref, g_ref, k_ref, f1_ref, f1i_ref, f2t_ref, f2ti_ref, twr_ref, twi_ref,
                      o_ref, f1_s, f1i_s, f2t_s, f2ti_s, *, conv_u, u_ch0, g_ch0):
    cblk = pl.program_id(0)
    _, cb, n1h, _ = u_ref.shape
    n1 = 2 * n1h
    for dst, src in ((f1_s, f1_ref), (f1i_s, f1i_ref), (f2t_s, f2t_ref), (f2ti_s, f2ti_ref)):
        dst[...] = src[...].astype(BF16)
    twr, twi = twr_ref[...], twi_ref[...]
    masks = _cm_masks(n1h)
    group_n = HY_CH_GROUP

    def group(gi, carry):
        base = gi * group_n
        x2s = []
        for j in range(group_n):
            xs = []
            for b in range(2):
                x = u_ref[b, base + j].astype(F32)
                if conv_u:
                    x = _cm_short_conv(x, _cm_taps(cw_ref, cb_ref, u_ch0 + cblk * cb + base + j), masks)
                xs.append(x)
            x2s.append(jnp.concatenate(xs, axis=0).astype(BF16))
        blocks = []
        for j in range(0, group_n, 2):
            a2 = _dot(f1_s[...], jnp.concatenate([x2s[j], x2s[j + 1]], axis=1))
            for re, im in _cm_twiddle(a2, n1, twr, twi, False):
                blocks.append(jnp.concatenate([re, im], axis=1).astype(BF16))
        b2 = _dot(jnp.concatenate(blocks, axis=0), f2t_s[...])
        prods = []
        for j in range(group_n):
            br = b2[j * n1:(j + 1) * n1, :HY_N2]
            bi = b2[j * n1:(j + 1) * n1, HY_N2:]
            kr = k_ref[0, base + j, :, 0:HY_N2]
            ki = k_ref[0, base + j, :, HY_N2:]
            prods.append(jnp.concatenate([br * kr - bi * ki, br * ki + bi * kr], axis=1).astype(BF16))
        c2 = _dot(jnp.concatenate(prods, axis=0), f2ti_s[...])
        cols = []
        for j in range(group_n):
            cr = c2[j * n1:(j + 1) * n1, :HY_N2]
            ci = c2[j * n1:(j + 1) * n1, HY_N2:]
            cols.append(jnp.concatenate([cr * twr + ci * twi, ci * twr - cr * twi], axis=0).astype(BF16))
        for j in range(0, group_n, 2):
            y2 = _dot(f1i_s[...], jnp.concatenate([cols[j], cols[j + 1]], axis=1))
            for h in range(2):
                ch = base + j + h
                taps = _cm_taps(cw_ref, cb_ref, g_ch0 + cblk * cb + ch)
                for b in range(2):
                    y = y2[b * n1h:(b + 1) * n1h, h * HY_N2:(h + 1) * HY_N2]
                    gate = _cm_short_conv(g_ref[b, ch], taps, masks)
                    o_ref[b, ch] = (gate * y).astype(o_ref.dtype)
        return carry

    lax.fori_loop(0, cb // group_n, group, 0)


def _hycm_conv(u, u_ch0, g, g_ch0, conv_w, conv_b, spectra, order, consts, conv_u):
    bx, _, n1h, _ = u.shape
    n1 = 2 * n1h
    nblk = HY_WIDTH // HY_CB
    smem = pl.BlockSpec(memory_space=pltpu.SMEM)
    data = lambda ch0: pl.BlockSpec((2, HY_CB, n1h, HY_N2), lambda c, p, ch0=ch0: (p, ch0 // HY_CB + c, 0, 0))
    names = ("f1", "f1i", "f2t", "f2ti", "twr", "twi")
    dft = pltpu.VMEM((2 * HY_N2, 2 * HY_N2), BF16)
    return pl.pallas_call(
        functools.partial(_hycm_conv_kernel, conv_u=conv_u, u_ch0=u_ch0, g_ch0=g_ch0),
        grid=(nblk, bx // 2),
        in_specs=[smem, smem, data(u_ch0), data(g_ch0),
                  pl.BlockSpec((1, HY_CB, n1, 2 * HY_N2), lambda c, p: (order, c, 0, 0))]
                 + [_full_spec(consts[k]) for k in names],
        out_specs=pl.BlockSpec((2, HY_CB, n1h, HY_N2), lambda c, p: (p, c, 0, 0)),
        out_shape=jax.ShapeDtypeStruct((bx, HY_WIDTH, n1h, HY_N2), BF16),
        scratch_shapes=[pltpu.VMEM((2 * n1, n1), BF16), pltpu.VMEM((n1, 2 * n1), BF16), dft, dft],
        compiler_params=_params("parallel", "arbitrary"),
        name="hyena_conv_cm",
    )(conv_w.reshape(-1), conv_b.reshape(-1), u, g, spectra, *[consts[k] for k in names])


def _cm_consts(n):
    big_n = 2 * n
    n1h = n // HY_N2
    n1 = 2 * n1h
    a1 = 2.0 * np.pi * np.outer(np.arange(n1), np.arange(n1h)) / n1
    c1, s1 = np.cos(a1), np.sin(a1)
    idx = np.arange(HY_N2, dtype=np.float64)
    a2 = 2.0 * np.pi * np.outer(idx, idx) / HY_N2
    c2, s2 = np.cos(a2), np.sin(a2)
    at = 2.0 * np.pi * np.outer(np.arange(n1), idx) / big_n
    f1 = np.block([[c1, s1], [-s1, c1]])
    return {
        "f1": jnp.asarray(f1, F32), "f1r": jnp.asarray(f1[:, :n1h], F32),
        "f1i": jnp.asarray(np.block([[c1.T, -s1.T], [s1.T, c1.T]]), F32),
        "f2t": jnp.asarray(np.block([[c2, -s2], [s2, c2]]), F32),
        "f2ti": jnp.asarray(np.block([[c2, s2], [-s2, c2]]) / big_n, F32),
        "twr": jnp.asarray(np.cos(at), F32), "twi": jnp.asarray(-np.sin(at), F32),
    }


def _gelu_tanh(x):
    return x * (0.5 * (1.0 + jnp.tanh(math.sqrt(2.0 / math.pi) * (x + 0.044715 * (x * x * x)))))


def _merge_kernel(x_ref, sh_ref, sc_ref, gt_ref, g_ref, ya_ref, hf_ref, hb_ref, lg_ref, yc_ref,
                  wgate_ref, wa_ref, wb_ref, wc_ref, wo_ref, o_ref):
    x = x_ref[0]
    h = _norm_mod(x, g_ref[...], sh_ref[0], sc_ref[0]).astype(BF16)
    yb = ((hf_ref[0] + hb_ref[0]) * _gelu_tanh(lg_ref[0])).astype(BF16)
    y = None
    for j, (br, w_ref) in enumerate(((ya_ref[0], wa_ref), (yb, wb_ref), (yc_ref[0], wc_ref))):
        gate = jax.nn.sigmoid(_dot(h, wgate_ref[:, j * D_MODEL:(j + 1) * D_MODEL]))
        term = gate * _dot(br, w_ref[...])
        y = term if y is None else y + term
    o_ref[0] = x + gt_ref[0] * _dot(y.astype(BF16), wo_ref[...])


def _merge(x, mods, norm_g, ya, hf, hb, lg, yc, wgate, wa, wb, wc, wo):
    bx, t, _ = x.shape
    tm = TOKEN_TILE
    tile = lambda w: pl.BlockSpec((1, tm, w), lambda b, i: (b, i, 0))
    return pl.pallas_call(
        _merge_kernel,
        grid=(bx, t // tm),
        in_specs=[tile(D_MODEL), _mod_spec(mods, 0), _mod_spec(mods, 1), _mod_spec(mods, 2), _full_spec(norm_g),
                  tile(MLA_HEADS * V_HEAD), tile(LRU_WIDTH), tile(LRU_WIDTH), tile(LRU_WIDTH), tile(HY_WIDTH),
                  _full_spec(wgate), _full_spec(wa), _full_spec(wb), _full_spec(wc), _full_spec(wo)],
        out_specs=tile(D_MODEL),
        out_shape=jax.ShapeDtypeStruct(x.shape, F32),
        compiler_params=_params("parallel", "parallel"),
        name="merge",
    )(x, mods, mods, mods, norm_g, ya, hf, hb, lg, yc, wgate, wa, wb, wc, wo)


def _ffn_kernel(x_ref, sh_ref, sc_ref, gt_ref, g_ref, wg_ref, wu_ref, wd_ref, fg_ref, o_ref, *, final):
    x = x_ref[0]
    h = _norm_mod(x, g_ref[...], sh_ref[0], sc_ref[0]).astype(BF16)
    gate = _dot(h, wg_ref[...])
    act = (gate * jax.nn.sigmoid(gate) * _dot(h, wu_ref[...])).astype(BF16)
    y = x + gt_ref[0] * _dot(act, wd_ref[...])
    o_ref[0] = _rms(y, fg_ref[...]) if final else y


def _ffn(x, mods, norm_g, wg, wu, wd, final_g, final):
    bx, t, _ = x.shape
    tm = TOKEN_TILE
    tile = pl.BlockSpec((1, tm, D_MODEL), lambda b, i: (b, i, 0))
    return pl.pallas_call(
        functools.partial(_ffn_kernel, final=final),
        grid=(bx, t // tm),
        in_specs=[tile, _mod_spec(mods, 3), _mod_spec(mods, 4), _mod_spec(mods, 5), _full_spec(norm_g),
                  _full_spec(wg), _full_spec(wu), _full_spec(wd), _full_spec(final_g)],
        out_specs=tile,
        out_shape=jax.ShapeDtypeStruct(x.shape, F32),
        compiler_params=_params("parallel", "parallel"),
        name="ffn",
    )(x, mods, mods, mods, norm_g, wg, wu, wd, final_g)


_ROPE_SWAP = np.array([8, 9, 10, 11, 12, 13, 14, 15, 0, 1, 2, 3, 4, 5, 6, 7,
                       24, 25, 26, 27, 28, 29, 30, 31, 16, 17, 18, 19, 20, 21, 22, 23])


def _rope_tables(n):
    cos = np.zeros((n, HEAD_PAD))
    sin = np.zeros((n, HEAD_PAD))
    cos[:, :QK_NOPE + QK_ROPE] = 1.0
    if n % GRID_W == 0 and n > 0:
        pos = np.arange(n)
        seg = QK_ROPE // 2
        inv = 1.0 / (ROPE_BASE ** (np.arange(seg // 2, dtype=np.float64) * 2.0 / seg))
        for s, p in enumerate((pos // GRID_W, pos % GRID_W)):
            ang = p[:, None] * inv[None, :]
            base = QK_NOPE + s * seg
            cos[:, base:base + seg] = np.concatenate([np.cos(ang), np.cos(ang)], axis=-1)
            sin[:, base:base + seg] = np.concatenate([-np.sin(ang), np.sin(ang)], axis=-1)
    return jnp.asarray(cos, F32), jnp.asarray(sin, F32)


def _identity_rope_tables():
    cos = np.zeros((TOKEN_TILE, HEAD_PAD))
    cos[:, :QK_NOPE + QK_ROPE] = 1.0
    return jnp.asarray(cos, F32), jnp.zeros((TOKEN_TILE, HEAD_PAD), F32)


def _block_diag(w):
    g, i, j = w.shape
    return jnp.einsum("gij,gh->gihj", w, jnp.eye(g, dtype=w.dtype)).reshape(g * i, g * j)


def _prep_layer(p):
    w_in = p["w_in"]
    kpe = w_in[:, Q_LORA + KV_LORA:MLA_IN]
    z64 = jnp.zeros((D_MODEL, QK_NOPE), F32)
    z32 = jnp.zeros((D_MODEL, HEAD_PAD - QK_NOPE - QK_ROPE), F32)
    out = {
        "w_mla": jnp.concatenate([w_in[:, :Q_LORA + KV_LORA], z64, kpe, z32, z64, kpe[:, _ROPE_SWAP], z32],
                                 axis=1).astype(BF16),
        "w_lx": w_in[:, IN_SPLITS[0]:IN_SPLITS[1]].astype(BF16),
        "w_lg": w_in[:, IN_SPLITS[1]:IN_SPLITS[2]].astype(BF16),
        "w_hy": w_in[:, IN_SPLITS[2]:IN_SPLITS[3]].astype(BF16),
        "w_gate": w_in[:, IN_SPLITS[3]:].astype(BF16),
    }
    wq = p["w_uq"].reshape(Q_LORA, MLA_HEADS, QK_NOPE + QK_ROPE)
    pad = jnp.zeros((Q_LORA, MLA_HEADS, HEAD_PAD - QK_NOPE - QK_ROPE), F32)
    out["w_q"] = jnp.concatenate([wq, pad], axis=-1).reshape(Q_LORA, -1).astype(BF16)
    out["w_qs"] = jnp.concatenate([jnp.zeros((Q_LORA, MLA_HEADS, QK_NOPE), F32),
                                   wq[:, :, QK_NOPE:][:, :, _ROPE_SWAP], pad], axis=-1
                                  ).reshape(Q_LORA, -1).astype(BF16)
    wkv = p["w_ukv"].reshape(KV_LORA, MLA_HEADS, QK_NOPE + V_HEAD)
    out["w_k"] = jnp.concatenate([wkv[:, :, :QK_NOPE], jnp.zeros((KV_LORA, MLA_HEADS, HEAD_PAD - QK_NOPE), F32)],
                                 axis=-1).reshape(KV_LORA, -1).astype(BF16)
    out["w_v"] = wkv[:, :, QK_NOPE:].reshape(KV_LORA, -1).astype(BF16)
    for d, name in enumerate(("f", "b")):
        out["lru_w" + name] = jnp.concatenate([_block_diag(p["lru_wa"][d]), _block_diag(p["lru_wx"][d])],
                                              axis=1).astype(BF16)
        out["lru_bias" + name] = jnp.concatenate([p["lru_ba"][d], p["lru_bx"][d]])[None, :]
    out["hy_w1"] = jnp.concatenate([p["hy_w1"], jnp.zeros((HY_EMB_PAD - HY_EMB, HY_HID), F32)], axis=0)
    for name in ("w_br_a", "w_br_b", "w_br_c", "w_out", "ffn_w_gate", "ffn_w_up", "ffn_w_down"):
        out[name] = p[name].astype(BF16)
    return out


def _mixers(z_mla, lx, hy, p, w, rope, kv_ctx=None, h0=None, branch_out=True):
    bx, n, _ = lx.shape
    q, k, v = _mla_prep(z_mla, p["q_norm_g"][None], p["kv_norm_g"][None], w["w_q"], w["w_qs"], w["w_k"], w["w_v"],
                        *rope)
    if h0 is None:
        h0 = jnp.zeros((bx, 2, LRU_WIDTH), F32)
    hf, hb, hlast = _lru(lx, p["lru_conv_w"], p["lru_conv_b"][None], w["lru_wf"], w["lru_biasf"],
                         w["lru_wb"], w["lru_biasb"], p["lru_lam"], h0)
    if not branch_out:
        return None, (k, v), hlast
    if kv_ctx is None:
        ya = _flash(q, k, v)
    else:
        ya = _flash(q, kv_ctx[0], kv_ctx[1], k, v)
    z, decay = _hy_tables(n)
    filt = _hy_filters(z, w["hy_w1"], p["hy_b1"][None], p["hy_w2"], p["hy_b2"], p["hy_freq"][None],
                       p["hy_w_out"], decay)
    n1h = n // HY_N2
    if n1h % 8 == 0:
        consts = _cm_consts(n)
        spectra = _hycm_spectra(filt.T.reshape(-1, n1h, HY_N2), p["hy_skip"], consts)
        hy_cm = hy.reshape(bx, n1h, HY_N2, 3 * HY_WIDTH).transpose(0, 3, 1, 2)
        cw, cb = p["hy_conv_w"], p["hy_conv_b"]
        y1 = _hycm_conv(hy_cm, 0, hy_cm, HY_WIDTH, cw, cb, spectra, 0, consts, True)
        yc = _hycm_conv(y1, 0, hy_cm, 2 * HY_WIDTH, cw, cb, spectra, 1, consts, False)
        yc = yc.transpose(0, 2, 3, 1).reshape(bx, n, HY_WIDTH)
    else:
        consts = _hy_consts(n)
        spectra = _hy_spectra(filt, p["hy_skip"], consts)
        cbn = HY_WIDTH // HY_CB
        cw, cb = p["hy_conv_w"], p["hy_conv_b"][None]
        y1 = _hy_conv(hy, 0, hy, cbn, cw, cb, spectra, 0, consts, True)
        yc = _hy_conv(y1, 0, hy, 2 * cbn, cw, cb, spectra, 1, consts, False)
    return (ya, hf, hb, yc), (k, v), hlast


def _layer(x, xc, mods_l, mods_c, p, final_g, last):
    bx, s, _ = x.shape
    sc = xc.shape[1]
    w = _prep_layer(p)
    n1g = p["norm1_g"][None]
    n2g = p["norm2_g"][None]
    flat = lambda a: a.reshape(1, bx * sc, a.shape[-1])
    unflat = lambda a: a.reshape(bx, sc, a.shape[-1])

    zc = [unflat(a) for a in _in_proj(flat(xc), mods_c, n1g, w["w_mla"], w["w_lx"], w["w_lg"], w["w_hy"])]
    zl = _in_proj(x, mods_l, n1g, w["w_mla"], w["w_lx"], w["w_lg"], w["w_hy"])

    br_c, kv_c, h_c = _mixers(zc[0], zc[1], zc[3], p, w, _identity_rope_tables(), branch_out=not last)
    br_l, _, _ = _mixers(zl[0], zl[1], zl[3], p, w, _rope_tables(s), kv_ctx=kv_c, h0=h_c)

    merge_w = (w["w_gate"], w["w_br_a"], w["w_br_b"], w["w_br_c"], w["w_out"])
    ffn_w = (w["ffn_w_gate"], w["ffn_w_up"], w["ffn_w_down"])
    x = _merge(x, mods_l, n1g, br_l[0], br_l[1], br_l[2], zl[2], br_l[3], *merge_w)
    x = _ffn(x, mods_l, n2g, *ffn_w, final_g, last)
    if not last:
        xcf = _merge(flat(xc), mods_c, n1g, flat(br_c[0]), flat(br_c[1]), flat(br_c[2]), flat(zc[2]),
                     flat(br_c[3]), *merge_w)
        xc = unflat(_ffn(xcf, mods_c, n2g, *ffn_w, final_g, False))
    return x, xc


def kernel(x, c, ctx, c_ctx, ada_w, ada_b, norm1_g, norm2_g, w_in, q_norm_g, w_uq, kv_norm_g, w_ukv,
           lru_conv_w, lru_conv_b, lru_wa, lru_ba, lru_wx, lru_bx, lru_lam,
           hy_conv_w, hy_conv_b, hy_w1, hy_b1, hy_w2, hy_b2, hy_freq, hy_w_out, hy_skip,
           w_br_a, w_br_b, w_br_c, w_out, ffn_w_gate, ffn_w_up, ffn_w_down, final_norm_g):
    depth = ada_w.shape[0]
    bx = x.shape[0]
    assert bx % 2 == 0 and bx + 1 <= 8
    cond = jnp.concatenate([c, c_ctx[None], jnp.zeros((8 - bx - 1, D_MODEL), F32)], axis=0)
    mods = _ada_mods(cond, ada_w, ada_b)
    stacked = dict(norm1_g=norm1_g, norm2_g=norm2_g, w_in=w_in, q_norm_g=q_norm_g, w_uq=w_uq,
                   kv_norm_g=kv_norm_g, w_ukv=w_ukv, lru_conv_w=lru_conv_w, lru_conv_b=lru_conv_b,
                   lru_wa=lru_wa, lru_ba=lru_ba, lru_wx=lru_wx, lru_bx=lru_bx, lru_lam=lru_lam,
                   hy_conv_w=hy_conv_w, hy_conv_b=hy_conv_b, hy_w1=hy_w1, hy_b1=hy_b1, hy_w2=hy_w2, hy_b2=hy_b2,
                   hy_freq=hy_freq, hy_w_out=hy_w_out, hy_skip=hy_skip, w_br_a=w_br_a, w_br_b=w_br_b,
                   w_br_c=w_br_c, w_out=w_out, ffn_w_gate=ffn_w_gate, ffn_w_up=ffn_w_up, ffn_w_down=ffn_w_down)
    xc = ctx
    fg = final_norm_g[None]
    for l in range(depth):
        p = {k: v[l] for k, v in stacked.items()}
        mods_l = mods[l, :bx][:, None, :]
        mods_c = mods[l, bx:bx + 1][:, None, :]
        x, xc = _layer(x, xc, mods_l, mods_c, p, fg, l == depth - 1)
    return x
```

```python
import functools
import math

import numpy as np
import jax
import jax.numpy as jnp
from jax import lax
from jax.experimental import pallas as pl
from jax.experimental.pallas import tpu as pltpu

F32 = jnp.float32
BF16 = jnp.bfloat16

D_MODEL = 1024
GRID_W = 64
EPS = 1e-6

MLA_HEADS = 8
Q_LORA = 384
KV_LORA = 256
QK_NOPE = 64
QK_ROPE = 32
V_HEAD = 64
ROPE_BASE = 10000.0
SM_SCALE = (QK_NOPE + QK_ROPE) ** -0.5
HEAD_PAD = 128
MLA_Z = Q_LORA + KV_LORA + 2 * HEAD_PAD

LRU_WIDTH = 512
LRU_BLOCKS = 8
LRU_CONV = 4
LRU_C = 8.0

HY_WIDTH = 512
HY_ORDER = 2
HY_SHORT = 3
HY_EMB = 33
HY_EMB_PAD = 40
HY_HID = 64
HY_INNER = 2
HY_FAST_DECAY = 0.3
HY_SLOW_DECAY = 1.5
HY_DECAY_TARGET = 1e-2
HY_N2 = 128
HY_CB = 128
HY_GROUP = 4
HY_CH_GROUP = 16

FFN_HID = 2816
N_BRANCH = 3
MLA_IN = Q_LORA + KV_LORA + QK_ROPE
IN_SPLITS = (MLA_IN, MLA_IN + LRU_WIDTH, MLA_IN + 2 * LRU_WIDTH, MLA_IN + 2 * LRU_WIDTH + 3 * HY_WIDTH)

Q_PRESCALE = SM_SCALE * math.log2(math.e)
FLASH_Q_TILE = 1024
FLASH_KEY_CHUNK = 1024
TOKEN_TILE = 256
HALO = 8
VMEM_LIMIT = 56 * 1024 * 1024


def _params(*sem):
    return pltpu.CompilerParams(dimension_semantics=sem, vmem_limit_bytes=VMEM_LIMIT)


def _dot(a, b):
    return jnp.dot(a, b, preferred_element_type=F32)


def _rms(x, g):
    return x * lax.rsqrt(jnp.mean(x * x, axis=-1, keepdims=True) + EPS) * g


def _norm_mod(x, g, shift, scale):
    return _rms(x, g) * (1.0 + scale) + shift


def _arr(op):
    return op[0] if isinstance(op, tuple) else op


def _full_spec(op):
    if isinstance(op, tuple):
        arr, layer = op
        nd = arr.ndim
        return pl.BlockSpec((None,) + arr.shape[1:], lambda *_: (layer,) + (0,) * (nd - 1))
    nd = op.ndim
    return pl.BlockSpec(op.shape, lambda *_: (0,) * nd)


def _mod_spec(mods, j):
    arr, layer, row = mods
    if row is None:
        return pl.BlockSpec((None, 1, 1, D_MODEL), lambda b, i: (layer, b, 0, j))
    return pl.BlockSpec((None, 1, 1, D_MODEL), lambda b, i: (layer, row, 0, j))


def _ada_kernel(c_ref, w_ref, b_ref, o_ref):
    c = c_ref[...]
    s = (c * jax.nn.sigmoid(c)).astype(BF16)
    o_ref[0] = _dot(s, w_ref[0].astype(BF16)) + b_ref[0]


def _ada_mods(cond, ada_w, ada_b):
    depth, _, width = ada_w.shape
    tn = 1536
    return pl.pallas_call(
        _ada_kernel,
        grid=(depth, width // tn),
        in_specs=[pl.BlockSpec((8, D_MODEL), lambda l, j: (0, 0)),
                  pl.BlockSpec((1, D_MODEL, tn), lambda l, j: (l, 0, j)),
                  pl.BlockSpec((1, 1, tn), lambda l, j: (l, 0, j))],
        out_specs=pl.BlockSpec((1, 8, tn), lambda l, j: (l, 0, j)),
        out_shape=jax.ShapeDtypeStruct((depth, 8, width), F32),
        compiler_params=_params("arbitrary", "arbitrary"),
        name="ada_mods",
    )(cond, ada_w, ada_b.reshape(depth, 1, width))


def _inproj_kernel(x_ref, sh_ref, sc_ref, g_ref, wm_ref, wx_ref, wg_ref, wh_ref,
                   qg_ref, kvg_ref, wq_ref, wqs_ref, wk_ref, wv_ref, cos_ref, sin_ref,
                   q_ref, k_ref, v_ref, lx_ref, lg_ref, hy_ref):
    h = _norm_mod(x_ref[0], g_ref[...], sh_ref[0], sc_ref[0]).astype(BF16)
    lx_ref[0] = _dot(h, wx_ref[...])
    lg_ref[0] = _dot(h, wg_ref[...])
    hy_ref[0] = _dot(h, wh_ref[...]).astype(hy_ref.dtype)
    z = _dot(h, wm_ref[...])
    nq = _rms(z[:, :Q_LORA], qg_ref[...]).astype(BF16)
    nkv = _rms(z[:, Q_LORA:Q_LORA + KV_LORA], kvg_ref[...]).astype(BF16)
    pe = z[:, Q_LORA + KV_LORA:Q_LORA + KV_LORA + HEAD_PAD]
    pes = z[:, Q_LORA + KV_LORA + HEAD_PAD:]
    cos = cos_ref[...]
    sin = sin_ref[...]
    q = _dot(nq, wq_ref[...])
    qs = _dot(nq, wqs_ref[...])
    kn = _dot(nkv, wk_ref[...])
    k_pe = pe * cos + pes * sin
    for hd in range(MLA_HEADS):
        sl = slice(hd * HEAD_PAD, (hd + 1) * HEAD_PAD)
        q_ref[0, :, sl] = ((q[:, sl] * cos + qs[:, sl] * sin) * Q_PRESCALE).astype(BF16)
        k_ref[0, :, sl] = (kn[:, sl] + k_pe).astype(BF16)
    v_ref[0] = _dot(nkv, wv_ref[...]).astype(BF16)


def _in_proj(x, mods, weights, cos, sin):
    bx, t, _ = x.shape
    tm = TOKEN_TILE
    tile = lambda w: pl.BlockSpec((1, tm, w), lambda b, i: (b, i, 0))
    if cos.shape[0] == tm:
        tab = pl.BlockSpec((tm, HEAD_PAD), lambda b, i: (0, 0))
    else:
        tab = pl.BlockSpec((tm, HEAD_PAD), lambda b, i: (i, 0))
    outs = ((MLA_HEADS * HEAD_PAD, BF16), (MLA_HEADS * HEAD_PAD, BF16), (MLA_HEADS * V_HEAD, BF16),
            (LRU_WIDTH, F32), (LRU_WIDTH, F32), (3 * HY_WIDTH, BF16))
    return pl.pallas_call(
        _inproj_kernel,
        grid=(bx, t // tm),
        in_specs=[tile(D_MODEL), _mod_spec(mods, 0), _mod_spec(mods, 1)]
                 + [_full_spec(w) for w in weights] + [tab, tab],
        out_specs=[tile(w) for w, _ in outs],
        out_shape=[jax.ShapeDtypeStruct((bx, t, w), dt) for w, dt in outs],
        compiler_params=_params("parallel", "parallel"),
        name="in_proj",
    )(x, mods[0], mods[0], *[_arr(w) for w in weights], cos, sin)


def _qk(q, k):
    return lax.dot_general(q, k, (((1,), (1,)), ((), ())), preferred_element_type=F32)


def _flash_kernel(*refs, chunks):
    q_ref, o_ref = refs[0], refs[-1]
    kv = refs[1:-1]
    state = [None, None]
    for src, off, size in chunks:
        k_ref, v_ref = kv[2 * src], kv[2 * src + 1]
        v = v_ref[0, off:off + size, :]
        for h in range(2):
            sl = slice(h * HEAD_PAD, (h + 1) * HEAD_PAD)
            s = _qk(q_ref[0, :, sl], k_ref[0, off:off + size, sl])
            m_blk = jnp.max(s, axis=-1, keepdims=True)
            if state[h] is None:
                m = m_blk
                p = jnp.exp2(s - m)
                l = jnp.sum(p, axis=-1, keepdims=True)
                acc = _dot(p.astype(BF16), v)
            else:
                m_old, l, acc = state[h]
                m = jnp.maximum(m_old, m_blk)
                alpha = jnp.exp2(m_old - m)
                p = jnp.exp2(s - m)
                l = alpha * l + jnp.sum(p, axis=-1, keepdims=True)
                acc = alpha * acc + _dot(p.astype(BF16), v)
            state[h] = (m, l, acc)
    outs = [acc / l for _, l, acc in state]
    lane = lax.broadcasted_iota(jnp.int32, outs[0].shape, 1)
    o_ref[0] = jnp.where(lane < V_HEAD, outs[0], outs[1]).astype(o_ref.dtype)


def _flash(q, kc, vc, kl=None, vl=None):
    bx, t, _ = q.shape
    tq = min(FLASH_Q_TILE, t)
    sc = kc.shape[1]
    pairs = MLA_HEADS // 2
    qspec = pl.BlockSpec((1, tq, 2 * HEAD_PAD), lambda b, hp, i: (b, i, hp))
    kspec = lambda n: pl.BlockSpec((1, n, 2 * HEAD_PAD), lambda b, hp, i: (b, 0, hp))
    vspec = lambda n: pl.BlockSpec((1, n, 2 * V_HEAD), lambda b, hp, i: (b, 0, hp))
    in_specs = [qspec, kspec(sc), vspec(sc)]
    args = [q, kc, vc]
    chunks = [(0, 0, sc)]
    if kl is not None:
        sl = kl.shape[1]
        size = min(FLASH_KEY_CHUNK, sl)
        chunks += [(1, off, size) for off in range(0, sl, size)]
        in_specs += [kspec(sl), vspec(sl)]
        args += [kl, vl]
    return pl.pallas_call(
        functools.partial(_flash_kernel, chunks=tuple(chunks)),
        grid=(bx, pairs, t // tq),
        in_specs=in_specs,
        out_specs=pl.BlockSpec((1, tq, 2 * V_HEAD), lambda b, hp, i: (b, i, hp)),
        out_shape=jax.ShapeDtypeStruct((bx, t, MLA_HEADS * V_HEAD), BF16),
        compiler_params=_params("parallel", "parallel", "arbitrary"),
        name="flash",
    )(*args)


def _lru_kernel(xf_ref, xfp_ref, xfn_ref, xb_ref, xbp_ref, xbn_ref, cw_ref, cb_ref,
                wf_ref, bf_ref, wb_ref, bb_ref, lam_ref, h0_ref,
                hf_ref, hb_ref, hl_ref, af_s, bfw_s, ab_s, bbw_s, pad_s, carry_s):
    t = pl.program_id(0)
    nt = pl.num_programs(0)
    bx, tm, _ = xf_ref.shape

    @pl.when(t == 0)
    def _():
        carry_s[...] = h0_ref[...]

    def coeffs(b, x_ref, prev_ref, next_ref, tile, w_ref, bias_ref, lam, a_s, b_s):
        x = x_ref[b]
        pad_s[0:HALO, :] = jnp.where(tile > 0, prev_ref[b], 0.0)
        pad_s[HALO:HALO + tm, :] = x
        pad_s[HALO + tm:, :] = jnp.where(tile < nt - 1, next_ref[b], 0.0)
        u = (cw_ref[0:1, :] * pad_s[HALO - 2:HALO - 2 + tm, :] + cw_ref[1:2, :] * pad_s[HALO - 1:HALO - 1 + tm, :]
             + cw_ref[2:3, :] * x + cw_ref[3:4, :] * pad_s[HALO + 1:HALO + 1 + tm, :] + cb_ref[...])
        y = _dot(u.astype(BF16), w_ref[...]) + bias_ref[...]
        r = jax.nn.sigmoid(y[:, :LRU_WIDTH])
        i = jax.nn.sigmoid(y[:, LRU_WIDTH:])
        neg = -lam
        softplus = jnp.maximum(neg, 0.0) + jnp.log1p(jnp.exp(-jnp.abs(neg)))
        a = jnp.exp((-LRU_C) * r * softplus)
        a_s[b] = a
        b_s[b] = jnp.sqrt(1.0 - a * a) * (i * u)

    for b in range(bx):
        coeffs(b, xf_ref, xfp_ref, xfn_ref, t, wf_ref, bf_ref, lam_ref[0:1, :], af_s, bfw_s)
        coeffs(b, xb_ref, xbp_ref, xbn_ref, nt - 1 - t, wb_ref, bb_ref, lam_ref[1:2, :], ab_s, bbw_s)

    def body(j, hs):
        jb = tm - 1 - j
        out = []
        for b in range(bx):
            hf = af_s[b, pl.ds(j, 1), :] * hs[2 * b] + bfw_s[b, pl.ds(j, 1), :]
            hf_ref[b, pl.ds(j, 1), :] = hf
            hb = ab_s[b, pl.ds(jb, 1), :] * hs[2 * b + 1] + bbw_s[b, pl.ds(jb, 1), :]
            hb_ref[b, pl.ds(jb, 1), :] = hb
            out += [hf, hb]
        return tuple(out)

    init = tuple(carry_s[b, d:d + 1, :] for b in range(bx) for d in range(2))
    hs = lax.fori_loop(0, tm, body, init, unroll=8)
    for b in range(bx):
        for d in range(2):
            carry_s[b, d:d + 1, :] = hs[2 * b + d]
    hl_ref[...] = carry_s[...]


def _lru(lx, conv_w, conv_b, wf, biasf, wb, biasb, lam, h0):
    bx, t, w = lx.shape
    tm = TOKEN_TILE
    nt = t // tm
    hb_per = tm // HALO
    last_halo = t // HALO - 1
    fwd = lambda i: (0, i, 0)
    bwd = lambda i: (0, nt - 1 - i, 0)
    prev_of = lambda f: (lambda i: (0, jnp.maximum(f(i)[1] * hb_per - 1, 0), 0))
    next_of = lambda f: (lambda i: (0, jnp.minimum((f(i)[1] + 1) * hb_per, last_halo), 0))
    tile = lambda f: pl.BlockSpec((bx, tm, w), f)
    halo = lambda f: pl.BlockSpec((bx, HALO, w), f)
    state = pl.BlockSpec((bx, 2, w), lambda i: (0, 0, 0))
    return pl.pallas_call(
        _lru_kernel,
        grid=(nt,),
        in_specs=[tile(fwd), halo(prev_of(fwd)), halo(next_of(fwd)),
                  tile(bwd), halo(prev_of(bwd)), halo(next_of(bwd)),
                  _full_spec(conv_w), _full_spec(conv_b), _full_spec(wf), _full_spec(biasf),
                  _full_spec(wb), _full_spec(biasb), _full_spec(lam), state],
        out_specs=[tile(fwd), tile(bwd), state],
        out_shape=[jax.ShapeDtypeStruct((bx, t, w), F32), jax.ShapeDtypeStruct((bx, t, w), F32),
                   jax.ShapeDtypeStruct((bx, 2, w), F32)],
        scratch_shapes=[pltpu.VMEM((bx, tm, w), F32)] * 4
                       + [pltpu.VMEM((tm + 2 * HALO, w), F32), pltpu.VMEM((bx, 2, w), F32)],
        compiler_params=_params("arbitrary"),
        name="rglru",
    )(lx, lx, lx, lx, lx, lx, *[_arr(w) for w in (conv_w, conv_b, wf, biasf, wb, biasb, lam)], h0)


def _hyfilt_kernel(z_ref, w1_ref, b1_ref, w2_ref, b2_ref, fr_ref, wo_ref, dec_ref, o_ref):
    hi = lax.Precision.HIGHEST
    fr = fr_ref[...]
    h = jnp.sin(fr * (jnp.dot(z_ref[...], w1_ref[...], precision=hi, preferred_element_type=F32) + b1_ref[...]))
    for j in range(HY_INNER):
        h = jnp.sin(fr * (jnp.dot(h, w2_ref[j], precision=hi, preferred_element_type=F32) + b2_ref[j:j + 1, :]))
    dec = dec_ref[...]
    for g in range(2 * HY_ORDER):
        sl = slice(g * HY_WIDTH, (g + 1) * HY_WIDTH)
        o_ref[:, sl] = jnp.dot(h, wo_ref[:, sl], precision=hi, preferred_element_type=F32) * dec


def _hy_filters(z, w1, b1, w2, b2, freq, w_out, decay):
    n = z.shape[0]
    tn = min(n, 512)
    width = 2 * HY_ORDER * HY_WIDTH
    return pl.pallas_call(
        _hyfilt_kernel,
        grid=(n // tn,),
        in_specs=[pl.BlockSpec((tn, HY_EMB_PAD), lambda i: (i, 0)), _full_spec(w1), _full_spec(b1),
                  _full_spec(w2), _full_spec(b2), _full_spec(freq), _full_spec(w_out),
                  pl.BlockSpec((tn, HY_WIDTH), lambda i: (i, 0))],
        out_specs=pl.BlockSpec((tn, width), lambda i: (i, 0)),
        out_shape=jax.ShapeDtypeStruct((n, width), F32),
        compiler_params=_params("parallel"),
        name="hyena_filters",
    )(z, *[_arr(w) for w in (w1, b1, w2, b2, freq, w_out)], decay)


def _bitrev(k, bits):
    r = jnp.zeros_like(k)
    for b in range(bits):
        r = r | (((k >> b) & 1) << (bits - 1 - b))
    return r


def _fft_lead_fwd(sre, sim, wr_ref, wi_ref, n1):
    half = n1 // 2
    m = half
    first = True
    while m >= 1:
        shift = int(math.log2(m))
        stride = half // m

        def body(q, c, m=m, shift=shift, stride=stride, first=first):
            grp = q >> shift
            j = q - (grp << shift)
            i0 = (grp << (shift + 1)) + j
            i1 = i0 + m
            wr = wr_ref[j * stride]
            wi = wi_ref[j * stride]
            ar, ai = sre[i0], sim[i0]
            if first:
                dr, di = ar, ai
            else:
                br, bi = sre[i1], sim[i1]
                sre[i0] = ar + br
                sim[i0] = ai + bi
                dr, di = ar - br, ai - bi
            sre[i1] = dr * wr - di * wi
            sim[i1] = dr * wi + di * wr
            return c

        lax.fori_loop(0, half, body, 0)
        first = False
        m //= 2


def _fft_lead_inv(sre, sim, wr_ref, wi_ref, n1):
    half = n1 // 2
    m = 1
    while m <= half:
        shift = int(math.log2(m))
        stride = half // m
        last = m == half

        def body(q, c, m=m, shift=shift, stride=stride, last=last):
            grp = q >> shift
            j = q - (grp << shift)
            i0 = (grp << (shift + 1)) + j
            i1 = i0 + m
            wr = wr_ref[j * stride]
            wi = wi_ref[j * stride]
            ar, ai = sre[i0], sim[i0]
            br, bi = sre[i1], sim[i1]
            tr = br * wr + bi * wi
            ti = bi * wr - br * wi
            sre[i0] = ar + tr
            sim[i0] = ai + ti
            if not last:
                sre[i1] = ar - tr
                sim[i1] = ai - ti
            return c

        lax.fori_loop(0, half, body, 0)
        m *= 2


def _spectrum_loop(sre, sim, f2_s, tw, n1, emit):
    tw0r_ref, tw0i_ref, wgr_ref, wgi_ref = tw
    bits = int(math.log2(n1))
    group = min(HY_GROUP, n1)

    def body(kb, tws):
        ks, xs = [], []
        for g in range(group):
            k1 = kb * group + g
            blk = _bitrev(k1, bits)
            tr, ti = tws[2 * g], tws[2 * g + 1]
            ar, ai = sre[blk], sim[blk]
            xs.append(jnp.concatenate([ar * tr - ai * ti, ar * ti + ai * tr], axis=0).astype(BF16))
            ks.append((k1, blk))
        b2 = _dot(f2_s[...], jnp.concatenate(xs, axis=1))
        emit(ks, b2, tws)
        wgr, wgi = wgr_ref[...], wgi_ref[...]
        nxt = []
        for g in range(group):
            tr, ti = tws[2 * g], tws[2 * g + 1]
            nxt += [tr * wgr - ti * wgi, tr * wgi + ti * wgr]
        return tuple(nxt)

    init = tuple(r[g] for g in range(group) for r in (tw0r_ref, tw0i_ref))
    lax.fori_loop(0, n1 // group, body, init)


def _group_cols(b2, g, cb):
    return b2[:HY_N2, g * cb:(g + 1) * cb], b2[HY_N2:, g * cb:(g + 1) * cb]


def _hyspec_kernel(wr_ref, wi_ref, hf_ref, hb_ref, skip_ref, f2_ref, tw0r_ref, tw0i_ref, wgr_ref, wgi_ref,
                   k_ref, sre, sim, f2_s):
    n = hf_ref.shape[0]
    n1 = 2 * n // HY_N2
    half = n1 // 2
    cb = hf_ref.shape[1]
    f2_s[...] = f2_ref[...].astype(BF16)
    for direction, h_ref in enumerate((hf_ref, hb_ref)):
        sre[0:half] = h_ref[...].reshape(half, HY_N2, cb)
        sim[0:half] = jnp.zeros((half, HY_N2, cb), F32)
        _fft_lead_fwd(sre, sim, wr_ref, wi_ref, n1)

        def emit(ks, b2, tws, direction=direction):
            for g, (k1, _) in enumerate(ks):
                br, bi = _group_cols(b2, g, cb)
                if direction == 0:
                    k_ref[0, k1, 0:HY_N2, :] = br + skip_ref[0]
                    k_ref[0, k1, HY_N2:, :] = bi
                else:
                    k_ref[0, k1, 0:HY_N2, :] = k_ref[0, k1, 0:HY_N2, :] + br
                    k_ref[0, k1, HY_N2:, :] = k_ref[0, k1, HY_N2:, :] - bi

        _spectrum_loop(sre, sim, f2_s, (tw0r_ref, tw0i_ref, wgr_ref, wgi_ref), n1, emit)


def _hy_spectra(filt, skip, consts):
    n = filt.shape[0]
    n1 = 2 * n // HY_N2
    cbn = HY_WIDTH // HY_CB
    smem = pl.BlockSpec(memory_space=pltpu.SMEM)
    col = lambda direction: (lambda o, c: (0, (direction * HY_ORDER + o) * cbn + c))
    tw_names = ("tw0r", "tw0i", "wgr", "wgi")
    return pl.pallas_call(
        _hyspec_kernel,
        grid=(HY_ORDER, cbn),
        in_specs=[smem, smem,
                  pl.BlockSpec((n, HY_CB), col(0)), pl.BlockSpec((n, HY_CB), col(1)),
                  pl.BlockSpec((None, 1, 1, HY_CB), lambda o, c: (skip[1], o, 0, c)),
                  _full_spec(consts["f2"])] + [_full_spec(consts[k]) for k in tw_names],
        out_specs=pl.BlockSpec((1, n1, 2 * HY_N2, HY_CB), lambda o, c: (o, 0, 0, c)),
        out_shape=jax.ShapeDtypeStruct((HY_ORDER, n1, 2 * HY_N2, HY_WIDTH), F32),
        scratch_shapes=[pltpu.VMEM((n1, HY_N2, HY_CB), F32)] * 2 + [pltpu.VMEM((2 * HY_N2, 2 * HY_N2), BF16)],
        compiler_params=_params("parallel", "parallel"),
        name="hyena_spectra",
    )(consts["wr"], consts["wi"], filt, filt, skip[0], consts["f2"], *[consts[k] for k in tw_names])


def _short_conv3(x, pad_s, w_ref, b_ref):
    n = x.shape[0]
    pad_s[HALO:HALO + n, :] = x
    return (w_ref[0:1, :] * pad_s[HALO - 1:HALO - 1 + n, :] + w_ref[1:2, :] * x
            + w_ref[2:3, :] * pad_s[HALO + 1:HALO + 1 + n, :] + b_ref[...])


def _hyconv_kernel(wr_ref, wi_ref, u_ref, g_ref, ucw_ref, ucb_ref, gcw_ref, gcb_ref, k_ref,
                   f2_ref, f2i_ref, tw0r_ref, tw0i_ref, wgr_ref, wgi_ref, o_ref,
                   sre, sim, pad_s, f2_s, f2i_s, *, conv_u):
    n = u_ref.shape[1]
    cb = u_ref.shape[2]
    n1 = 2 * n // HY_N2
    half = n1 // 2
    f2_s[...] = f2_ref[...].astype(BF16)
    f2i_s[...] = f2i_ref[...].astype(BF16)
    margin = jnp.zeros((HALO, cb), F32)
    pad_s[0:HALO, :] = margin
    pad_s[HALO + n:, :] = margin

    for b, s in enumerate((sre, sim)):
        u = u_ref[b].astype(F32)
        if conv_u:
            u = _short_conv3(u, pad_s, ucw_ref, ucb_ref)
        s[0:half] = u.reshape(half, HY_N2, cb)
    _fft_lead_fwd(sre, sim, wr_ref, wi_ref, n1)

    def emit(ks, b2, tws):
        ps = []
        for g, (k1, _) in enumerate(ks):
            br, bi = _group_cols(b2, g, cb)
            kr = k_ref[0, k1, 0:HY_N2, :]
            ki = k_ref[0, k1, HY_N2:, :]
            ps.append(jnp.concatenate([br * kr - bi * ki, br * ki + bi * kr], axis=0).astype(BF16))
        c2 = _dot(f2i_s[...], jnp.concatenate(ps, axis=1))
        for g, (_, blk) in enumerate(ks):
            cr, ci = _group_cols(c2, g, cb)
            tr, ti = tws[2 * g], tws[2 * g + 1]
            sre[blk] = cr * tr + ci * ti
            sim[blk] = ci * tr - cr * ti

    _spectrum_loop(sre, sim, f2_s, (tw0r_ref, tw0i_ref, wgr_ref, wgi_ref), n1, emit)
    _fft_lead_inv(sre, sim, wr_ref, wi_ref, n1)

    for b, s in enumerate((sre, sim)):
        gate = _short_conv3(g_ref[b].astype(F32), pad_s, gcw_ref, gcb_ref)
        o_ref[b] = (gate * s[0:half].reshape(n, cb)).astype(o_ref.dtype)


def _hy_conv(u, u_col, g, g_col, conv_w, conv_b, spectra, order, consts, conv_u):
    bx, n, _ = u.shape
    n1 = 2 * n // HY_N2
    cbn = HY_WIDTH // HY_CB
    smem = pl.BlockSpec(memory_space=pltpu.SMEM)
    data = lambda col: pl.BlockSpec((2, n, HY_CB), lambda c, p, col=col: (p, 0, col + c))
    layer = conv_w[1]
    wrow = lambda rows, col: pl.BlockSpec((None, rows, HY_CB), lambda c, p, col=col: (layer, 0, col + c))
    ucol = u_col if conv_u else g_col
    const_names = ("f2", "f2i", "tw0r", "tw0i", "wgr", "wgi")
    dft = pltpu.VMEM((2 * HY_N2, 2 * HY_N2), BF16)
    return pl.pallas_call(
        functools.partial(_hyconv_kernel, conv_u=conv_u),
        grid=(cbn, bx // 2),
        in_specs=[smem, smem, data(u_col), data(g_col),
                  wrow(HY_SHORT, ucol), wrow(1, ucol), wrow(HY_SHORT, g_col), wrow(1, g_col),
                  pl.BlockSpec((1, n1, 2 * HY_N2, HY_CB), lambda c, p: (order, 0, 0, c))]
                 + [_full_spec(consts[k]) for k in const_names],
        out_specs=pl.BlockSpec((2, n, HY_CB), lambda c, p: (p, 0, c)),
        out_shape=jax.ShapeDtypeStruct((bx, n, HY_WIDTH), BF16),
        scratch_shapes=[pltpu.VMEM((n1, HY_N2, HY_CB), F32)] * 2
                       + [pltpu.VMEM((n + 2 * HALO, HY_CB), F32), dft, dft],
        compiler_params=_params("parallel", "arbitrary"),
        name="hyena_conv",
    )(consts["wr"], consts["wi"], u, g, conv_w[0], conv_b[0], conv_w[0], conv_b[0], spectra,
      *[consts[k] for k in const_names])


def _hy_consts(n):
    big_n = 2 * n
    n1 = big_n // HY_N2
    group = min(HY_GROUP, n1)
    q = np.arange(max(n1 // 2, 1), dtype=np.float64)
    ang1 = 2.0 * np.pi * q / n1
    idx = np.arange(HY_N2, dtype=np.float64)
    ang2 = 2.0 * np.pi * np.outer(idx, idx) / HY_N2
    c, s = np.cos(ang2), np.sin(ang2)
    f2 = np.block([[c, s], [-s, c]])
    f2i = np.block([[c, -s], [s, c]]) / big_n
    lane = np.ones((1, 1, HY_CB))
    ang0 = 2.0 * np.pi * np.arange(group)[:, None, None] * idx[None, :, None] / big_n
    angg = 2.0 * np.pi * group * idx[:, None] / big_n
    return {
        "wr": jnp.asarray(np.cos(ang1), F32), "wi": jnp.asarray(-np.sin(ang1), F32),
        "f2": jnp.asarray(f2, F32), "f2i": jnp.asarray(f2i, F32),
        "tw0r": jnp.asarray(np.cos(ang0) * lane, F32), "tw0i": jnp.asarray(-np.sin(ang0) * lane, F32),
        "wgr": jnp.asarray(np.cos(angg) * lane[0], F32), "wgi": jnp.asarray(-np.sin(angg) * lane[0], F32),
    }


def _hy_tables(n):
    t = np.linspace(0.0, 1.0, n, dtype=np.float32)[:, None].astype(np.float64)
    bands = (HY_EMB - 1) // 2
    w = 2.0 * np.pi * np.arange(n, dtype=np.float64) / n
    f = np.linspace(1e-4, bands - 1, bands, dtype=np.float32).astype(np.float64)
    ang = w[:, None] * f[None, :]
    z = np.concatenate([t, np.cos(ang), -np.sin(ang), np.zeros((n, HY_EMB_PAD - HY_EMB))], axis=-1)
    max_decay = math.log(HY_DECAY_TARGET) / HY_FAST_DECAY
    min_decay = math.log(HY_DECAY_TARGET) / HY_SLOW_DECAY
    deltas = np.abs(np.linspace(min_decay, max_decay, HY_WIDTH, dtype=np.float32).astype(np.float64))
    return jnp.asarray(z, F32), jnp.asarray(np.exp(-t * deltas), F32)


def _cm_short_conv(x, taps, masks):
    first_lane, last_lane, first_row, last_row = masks
    n1h = x.shape[0]
    r = pltpu.roll(x, 1, axis=1)
    prev = jnp.where(first_lane, jnp.where(first_row, 0.0, pltpu.roll(r, 1, axis=0)), r)
    l = pltpu.roll(x, HY_N2 - 1, axis=1)
    nxt = jnp.where(last_lane, jnp.where(last_row, 0.0, pltpu.roll(l, n1h - 1, axis=0)), l)
    return taps[0] * prev + taps[1] * x + taps[2] * nxt + taps[3]


def _cm_masks(n1h):
    lane = lax.broadcasted_iota(jnp.int32, (n1h, HY_N2), 1)
    row = lax.broadcasted_iota(jnp.int32, (n1h, HY_N2), 0)
    return lane == 0, lane == HY_N2 - 1, row == 0, row == n1h - 1


def _cm_taps(cw_ref, cb_ref, layer, ch):
    width = 3 * HY_WIDTH
    base = layer * HY_SHORT * width + ch
    return cw_ref[base], cw_ref[base + width], cw_ref[base + 2 * width], cb_ref[layer * width + ch]


def _cm_twiddle(a2, n1, twr, twi, conj):
    out = []
    for h in range(2):
        ar = a2[:n1, h * HY_N2:(h + 1) * HY_N2]
        ai = a2[n1:, h * HY_N2:(h + 1) * HY_N2]
        if conj:
            out.append((ar * twr + ai * twi, ai * twr - ar * twi))
        else:
            out.append((ar * twr - ai * twi, ar * twi + ai * twr))
    return out


def _hycm_spec_kernel(skip_ref, hf_ref, hb_ref, f1r_ref, f2t_ref, twr_ref, twi_ref, k_ref, f1_s, f2t_s, *, layer):
    order, cblk = pl.program_id(0), pl.program_id(1)
    cb, n1h, _ = hf_ref.shape
    n1 = 2 * n1h
    f1_s[...] = f1r_ref[...].astype(BF16)
    f2t_s[...] = f2t_ref[...].astype(BF16)
    twr, twi = twr_ref[...], twi_ref[...]

    def group(gi, carry):
        base = gi * HY_CH_GROUP
        blocks = []
        for j in range(HY_CH_GROUP):
            x2 = jnp.concatenate([hf_ref[base + j], hb_ref[base + j]], axis=1).astype(BF16)
            for re, im in _cm_twiddle(_dot(f1_s[...], x2), n1, twr, twi, False):
                blocks.append(jnp.concatenate([re, im], axis=1).astype(BF16))
        b2 = _dot(jnp.concatenate(blocks, axis=0), f2t_s[...])
        for j in range(HY_CH_GROUP):
            bf = b2[(2 * j) * n1:(2 * j + 1) * n1]
            bb = b2[(2 * j + 1) * n1:(2 * j + 2) * n1]
            skip = skip_ref[(layer * HY_ORDER + order) * HY_WIDTH + cblk * cb + base + j]
            k_ref[0, base + j] = jnp.concatenate([bf[:, :HY_N2] + bb[:, :HY_N2] + skip,
                                                  bf[:, HY_N2:] - bb[:, HY_N2:]], axis=1)
        return carry

    lax.fori_loop(0, cb // HY_CH_GROUP, group, 0)


def _hycm_spectra(filt_cm, skip, consts):
    _, n1h, _ = filt_cm.shape
    n1 = 2 * n1h
    nblk = HY_WIDTH // HY_CB
    smem = pl.BlockSpec(memory_space=pltpu.SMEM)
    blk = lambda direction: pl.BlockSpec((HY_CB, n1h, HY_N2),
                                         lambda o, c: ((direction * HY_ORDER + o) * nblk + c, 0, 0))
    names = ("f1r", "f2t", "twr", "twi")
    return pl.pallas_call(
        functools.partial(_hycm_spec_kernel, layer=skip[1]),
        grid=(HY_ORDER, nblk),
        in_specs=[smem, blk(0), blk(1)] + [_full_spec(consts[k]) for k in names],
        out_specs=pl.BlockSpec((1, HY_CB, n1, 2 * HY_N2), lambda o, c: (o, c, 0, 0)),
        out_shape=jax.ShapeDtypeStruct((HY_ORDER, HY_WIDTH, n1, 2 * HY_N2), F32),
        scratch_shapes=[pltpu.VMEM((2 * n1, n1h), BF16), pltpu.VMEM((2 * HY_N2, 2 * HY_N2), BF16)],
        compiler_params=_params("parallel", "parallel"),
        name="hyena_spectra_cm",
    )(skip[0], filt_cm, filt_cm, *[consts[k] for k in names])


def _hycm_conv_kernel(cw_ref, cb_ref, u_ref, g_ref, k_ref, f1_ref, f1i_ref, f2t_ref, f2ti_ref, twr_ref, twi_ref,
                      o_ref, f1_s, f1i_s, f2t_s, f2ti_s, *, conv_u, u_ch0, g_ch0, layer):
    cblk = pl.program_id(0)
    _, cb, n1h, _ = u_ref.shape
    n1 = 2 * n1h
    for dst, src in ((f1_s, f1_ref), (f1i_s, f1i_ref), (f2t_s, f2t_ref), (f2ti_s, f2ti_ref)):
        dst[...] = src[...].astype(BF16)
    twr, twi = twr_ref[...], twi_ref[...]
    masks = _cm_masks(n1h)
    group_n = HY_CH_GROUP

    def group(gi, carry):
        base = gi * group_n
        x2s = []
        for j in range(group_n):
            xs = []
            for b in range(2):
                x = u_ref[b, base + j].astype(F32)
                if conv_u:
                    x = _cm_short_conv(x, _cm_taps(cw_ref, cb_ref, layer, u_ch0 + cblk * cb + base + j), masks)
                xs.append(x)
            x2s.append(jnp.concatenate(xs, axis=0).astype(BF16))
        blocks = []
        for j in range(0, group_n, 2):
            a2 = _dot(f1_s[...], jnp.concatenate([x2s[j], x2s[j + 1]], axis=1))
            for re, im in _cm_twiddle(a2, n1, twr, twi, False):
                blocks.append(jnp.concatenate([re, im], axis=1).astype(BF16))
        b2 = _dot(jnp.concatenate(blocks, axis=0), f2t_s[...])
        prods = []
        for j in range(group_n):
            br = b2[j * n1:(j + 1) * n1, :HY_N2]
            bi = b2[j * n1:(j + 1) * n1, HY_N2:]
            kr = k_ref[0, base + j, :, 0:HY_N2]
            ki = k_ref[0, base + j, :, HY_N2:]
            prods.append(jnp.concatenate([br * kr - bi * ki, br * ki + bi * kr], axis=1).astype(BF16))
        c2 = _dot(jnp.concatenate(prods, axis=0), f2ti_s[...])
        cols = []
        for j in range(group_n):
            cr = c2[j * n1:(j + 1) * n1, :HY_N2]
            ci = c2[j * n1:(j + 1) * n1, HY_N2:]
            cols.append(jnp.concatenate([cr * twr + ci * twi, ci * twr - cr * twi], axis=0).astype(BF16))
        for j in range(0, group_n, 2):
            y2 = _dot(f1i_s[...], jnp.concatenate([cols[j], cols[j + 1]], axis=1))
            for h in range(2):
                ch = base + j + h
                taps = _cm_taps(cw_ref, cb_ref, layer, g_ch0 + cblk * cb + ch)
                for b in range(2):
                    y = y2[b * n1h:(b + 1) * n1h, h * HY_N2:(h + 1) * HY_N2]
                    gate = _cm_short_conv(g_ref[b, ch].astype(F32), taps, masks)
                    o_ref[b, ch] = (gate * y).astype(o_ref.dtype)
        return carry

    lax.fori_loop(0, cb // group_n, group, 0)


def _hycm_conv(u, u_ch0, g, g_ch0, conv_w, conv_b, spectra, order, consts, conv_u):
    bx, _, n1h, _ = u.shape
    n1 = 2 * n1h
    nblk = HY_WIDTH // HY_CB
    smem = pl.BlockSpec(memory_space=pltpu.SMEM)
    data = lambda ch0: pl.BlockSpec((2, HY_CB, n1h, HY_N2), lambda c, p, ch0=ch0: (p, ch0 // HY_CB + c, 0, 0))
    names = ("f1", "f1i", "f2t", "f2ti", "twr", "twi")
    dft = pltpu.VMEM((2 * HY_N2, 2 * HY_N2), BF16)
    return pl.pallas_call(
        functools.partial(_hycm_conv_kernel, conv_u=conv_u, u_ch0=u_ch0, g_ch0=g_ch0, layer=conv_w[1]),
        grid=(nblk, bx // 2),
        in_specs=[smem, smem, data(u_ch0), data(g_ch0),
                  pl.BlockSpec((1, HY_CB, n1, 2 * HY_N2), lambda c, p: (order, c, 0, 0))]
                 + [_full_spec(consts[k]) for k in names],
        out_specs=pl.BlockSpec((2, HY_CB, n1h, HY_N2), lambda c, p: (p, c, 0, 0)),
        out_shape=jax.ShapeDtypeStruct((bx, HY_WIDTH, n1h, HY_N2), BF16),
        scratch_shapes=[pltpu.VMEM((2 * n1, n1), BF16), pltpu.VMEM((n1, 2 * n1), BF16), dft, dft],
        compiler_params=_params("parallel", "arbitrary"),
        name="hyena_conv_cm",
    )(conv_w[0], conv_b[0], u, g, spectra, *[consts[k] for k in names])


def _cm_consts(n):
    big_n = 2 * n
    n1h = n // HY_N2
    n1 = 2 * n1h
    a1 = 2.0 * np.pi * np.outer(np.arange(n1), np.arange(n1h)) / n1
    c1, s1 = np.cos(a1), np.sin(a1)
    idx = np.arange(HY_N2, dtype=np.float64)
    a2 = 2.0 * np.pi * np.outer(idx, idx) / HY_N2
    c2, s2 = np.cos(a2), np.sin(a2)
    at = 2.0 * np.pi * np.outer(np.arange(n1), idx) / big_n
    f1 = np.block([[c1, s1], [-s1, c1]])
    return {
        "f1": jnp.asarray(f1, F32), "f1r": jnp.asarray(f1[:, :n1h], F32),
        "f1i": jnp.asarray(np.block([[c1.T, -s1.T], [s1.T, c1.T]]), F32),
        "f2t": jnp.asarray(np.block([[c2, -s2], [s2, c2]]), F32),
        "f2ti": jnp.asarray(np.block([[c2, s2], [-s2, c2]]) / big_n, F32),
        "twr": jnp.asarray(np.cos(at), F32), "twi": jnp.asarray(-np.sin(at), F32),
    }


def _gelu_tanh(x):
    return x * (0.5 * (1.0 + jnp.tanh(math.sqrt(2.0 / math.pi) * (x + 0.044715 * (x * x * x)))))


def _merge_kernel(x_ref, sh_ref, sc_ref, gt_ref, g_ref, ya_ref, hf_ref, hb_ref, lg_ref, yc_ref,
                  wgate_ref, wa_ref, wb_ref, wc_ref, wo_ref, o_ref):
    x = x_ref[0]
    h = _norm_mod(x, g_ref[...], sh_ref[0], sc_ref[0]).astype(BF16)
    yb = ((hf_ref[0] + hb_ref[0]) * _gelu_tanh(lg_ref[0])).astype(BF16)
    y = None
    for j, (br, w_ref) in enumerate(((ya_ref[0], wa_ref), (yb, wb_ref), (yc_ref[0], wc_ref))):
        gate = jax.nn.sigmoid(_dot(h, wgate_ref[:, j * D_MODEL:(j + 1) * D_MODEL]))
        term = gate * _dot(br, w_ref[...])
        y = term if y is None else y + term
    o_ref[0] = x + gt_ref[0] * _dot(y.astype(BF16), wo_ref[...])


def _merge(x, mods, norm_g, ya, hf, hb, lg, yc, wgate, wa, wb, wc, wo):
    bx, t, _ = x.shape
    tm = TOKEN_TILE
    tile = lambda w: pl.BlockSpec((1, tm, w), lambda b, i: (b, i, 0))
    return pl.pallas_call(
        _merge_kernel,
        grid=(bx, t // tm),
        in_specs=[tile(D_MODEL), _mod_spec(mods, 0), _mod_spec(mods, 1), _mod_spec(mods, 2), _full_spec(norm_g),
                  tile(MLA_HEADS * V_HEAD), tile(LRU_WIDTH), tile(LRU_WIDTH), tile(LRU_WIDTH), tile(HY_WIDTH),
                  _full_spec(wgate), _full_spec(wa), _full_spec(wb), _full_spec(wc), _full_spec(wo)],
        out_specs=tile(D_MODEL),
        out_shape=jax.ShapeDtypeStruct(x.shape, F32),
        compiler_params=_params("parallel", "parallel"),
        name="merge",
    )(x, mods[0], mods[0], mods[0], _arr(norm_g), ya, hf, hb, lg, yc, *[_arr(w) for w in (wgate, wa, wb, wc, wo)])


def _ffn_kernel(x_ref, sh_ref, sc_ref, gt_ref, g_ref, wg_ref, wu_ref, wd_ref, fg_ref, o_ref, *, final):
    x = x_ref[0]
    h = _norm_mod(x, g_ref[...], sh_ref[0], sc_ref[0]).astype(BF16)
    gate = _dot(h, wg_ref[...])
    act = (gate * jax.nn.sigmoid(gate) * _dot(h, wu_ref[...])).astype(BF16)
    y = x + gt_ref[0] * _dot(act, wd_ref[...])
    o_ref[0] = _rms(y, fg_ref[...]) if final else y


def _ffn(x, mods, norm_g, wg, wu, wd, final_g, final):
    bx, t, _ = x.shape
    tm = TOKEN_TILE
    tile = pl.BlockSpec((1, tm, D_MODEL), lambda b, i: (b, i, 0))
    return pl.pallas_call(
        functools.partial(_ffn_kernel, final=final),
        grid=(bx, t // tm),
        in_specs=[tile, _mod_spec(mods, 3), _mod_spec(mods, 4), _mod_spec(mods, 5), _full_spec(norm_g),
                  _full_spec(wg), _full_spec(wu), _full_spec(wd), _full_spec(final_g)],
        out_specs=tile,
        out_shape=jax.ShapeDtypeStruct(x.shape, F32),
        compiler_params=_params("parallel", "parallel"),
        name="ffn",
    )(x, mods[0], mods[0], mods[0], *[_arr(w) for w in (norm_g, wg, wu, wd)], final_g)


_ROPE_SWAP = np.array([8, 9, 10, 11, 12, 13, 14, 15, 0, 1, 2, 3, 4, 5, 6, 7,
                       24, 25, 26, 27, 28, 29, 30, 31, 16, 17, 18, 19, 20, 21, 22, 23])


def _rope_tables(n):
    cos = np.zeros((n, HEAD_PAD))
    sin = np.zeros((n, HEAD_PAD))
    cos[:, :QK_NOPE + QK_ROPE] = 1.0
    if n % GRID_W == 0 and n > 0:
        pos = np.arange(n)
        seg = QK_ROPE // 2
        inv = 1.0 / (ROPE_BASE ** (np.arange(seg // 2, dtype=np.float64) * 2.0 / seg))
        for s, p in enumerate((pos // GRID_W, pos % GRID_W)):
            ang = p[:, None] * inv[None, :]
            base = QK_NOPE + s * seg
            cos[:, base:base + seg] = np.concatenate([np.cos(ang), np.cos(ang)], axis=-1)
            sin[:, base:base + seg] = np.concatenate([-np.sin(ang), np.sin(ang)], axis=-1)
    return jnp.asarray(cos, F32), jnp.asarray(sin, F32)


def _identity_rope_tables():
    cos = np.zeros((TOKEN_TILE, HEAD_PAD))
    cos[:, :QK_NOPE + QK_ROPE] = 1.0
    return jnp.asarray(cos, F32), jnp.zeros((TOKEN_TILE, HEAD_PAD), F32)


def _block_diag(w):
    l, g, i, j = w.shape
    return jnp.einsum("lgij,gh->lgihj", w, jnp.eye(g, dtype=w.dtype)).reshape(l, g * i, g * j)


def _prep_weights(p):
    depth = p["w_in"].shape[0]
    w_in = p["w_in"]
    kpe = w_in[..., Q_LORA + KV_LORA:MLA_IN]
    z64 = jnp.zeros((depth, D_MODEL, QK_NOPE), F32)
    z32 = jnp.zeros((depth, D_MODEL, HEAD_PAD - QK_NOPE - QK_ROPE), F32)
    row = lambda a: a[:, None, :]
    out = {
        "w_mla": jnp.concatenate([w_in[..., :Q_LORA + KV_LORA], z64, kpe, z32, z64, kpe[..., _ROPE_SWAP], z32],
                                 axis=-1).astype(BF16),
        "w_lx": w_in[..., IN_SPLITS[0]:IN_SPLITS[1]].astype(BF16),
        "w_lg": w_in[..., IN_SPLITS[1]:IN_SPLITS[2]].astype(BF16),
        "w_hy": w_in[..., IN_SPLITS[2]:IN_SPLITS[3]].astype(BF16),
        "w_gate": w_in[..., IN_SPLITS[3]:].astype(BF16),
    }
    wq = p["w_uq"].reshape(depth, Q_LORA, MLA_HEADS, QK_NOPE + QK_ROPE)
    pad = jnp.zeros((depth, Q_LORA, MLA_HEADS, HEAD_PAD - QK_NOPE - QK_ROPE), F32)
    out["w_q"] = jnp.concatenate([wq, pad], axis=-1).reshape(depth, Q_LORA, -1).astype(BF16)
    out["w_qs"] = jnp.concatenate([jnp.zeros((depth, Q_LORA, MLA_HEADS, QK_NOPE), F32),
                                   wq[..., QK_NOPE:][..., _ROPE_SWAP], pad], axis=-1
                                  ).reshape(depth, Q_LORA, -1).astype(BF16)
    wkv = p["w_ukv"].reshape(depth, KV_LORA, MLA_HEADS, QK_NOPE + V_HEAD)
    out["w_k"] = jnp.concatenate([wkv[..., :QK_NOPE],
                                  jnp.zeros((depth, KV_LORA, MLA_HEADS, HEAD_PAD - QK_NOPE), F32)],
                                 axis=-1).reshape(depth, KV_LORA, -1).astype(BF16)
    out["w_v"] = wkv[..., QK_NOPE:].reshape(depth, KV_LORA, -1).astype(BF16)
    for d, name in enumerate(("f", "b")):
        out["lru_w" + name] = jnp.concatenate([_block_diag(p["lru_wa"][:, d]), _block_diag(p["lru_wx"][:, d])],
                                              axis=-1).astype(BF16)
        out["lru_bias" + name] = row(jnp.concatenate([p["lru_ba"][:, d], p["lru_bx"][:, d]], axis=-1))
    out["hy_w1"] = jnp.concatenate([p["hy_w1"], jnp.zeros((depth, HY_EMB_PAD - HY_EMB, HY_HID), F32)], axis=1)
    for name in ("w_br_a", "w_br_b", "w_br_c", "w_out", "ffn_w_gate", "ffn_w_up", "ffn_w_down"):
        out[name] = p[name].astype(BF16)
    for name in ("norm1_g", "norm2_g", "q_norm_g", "kv_norm_g", "lru_conv_b", "hy_b1", "hy_freq", "hy_conv_b"):
        out[name] = row(p[name])
    for name in ("lru_conv_w", "lru_lam", "hy_w2", "hy_b2", "hy_w_out", "hy_conv_w"):
        out[name] = p[name]
    out["hy_skip"] = p["hy_skip"][:, :, None, :]
    out["hy_conv_w_flat"] = p["hy_conv_w"].reshape(-1)
    out["hy_conv_b_flat"] = p["hy_conv_b"].reshape(-1)
    out["hy_skip_flat"] = p["hy_skip"].reshape(-1)
    return out


def _mixers(q, k, v, lx, hy, w, layer, kv_ctx=None, h0=None, branch_out=True):
    bx, n, _ = lx.shape
    at = lambda name: (w[name], layer)
    if h0 is None:
        h0 = jnp.zeros((bx, 2, LRU_WIDTH), F32)
    hf, hb, hlast = _lru(lx, at("lru_conv_w"), at("lru_conv_b"), at("lru_wf"), at("lru_biasf"),
                         at("lru_wb"), at("lru_biasb"), at("lru_lam"), h0)
    if not branch_out:
        return None, hlast
    if kv_ctx is None:
        ya = _flash(q, k, v)
    else:
        ya = _flash(q, kv_ctx[0], kv_ctx[1], k, v)
    z, decay = _hy_tables(n)
    filt = _hy_filters(z, at("hy_w1"), at("hy_b1"), at("hy_w2"), at("hy_b2"), at("hy_freq"), at("hy_w_out"), decay)
    n1h = n // HY_N2
    if n1h % 8 == 0:
        consts = _cm_consts(n)
        spectra = _hycm_spectra(filt.T.reshape(-1, n1h, HY_N2), at("hy_skip_flat"), consts)
        hy_cm = hy.reshape(bx, n1h, HY_N2, 3 * HY_WIDTH).transpose(0, 3, 1, 2)
        cw, cb = at("hy_conv_w_flat"), at("hy_conv_b_flat")
        y1 = _hycm_conv(hy_cm, 0, hy_cm, HY_WIDTH, cw, cb, spectra, 0, consts, True)
        yc = _hycm_conv(y1, 0, hy_cm, 2 * HY_WIDTH, cw, cb, spectra, 1, consts, False)
        yc = yc.transpose(0, 2, 3, 1).reshape(bx, n, HY_WIDTH)
    else:
        consts = _hy_consts(n)
        spectra = _hy_spectra(filt, at("hy_skip"), consts)
        cbn = HY_WIDTH // HY_CB
        cw, cb = at("hy_conv_w"), at("hy_conv_b")
        y1 = _hy_conv(hy, 0, hy, cbn, cw, cb, spectra, 0, consts, True)
        yc = _hy_conv(y1, 0, hy, 2 * cbn, cw, cb, spectra, 1, consts, False)
    return (ya, hf, hb, yc), hlast


def _layer(x, xc, mods, w, layer, final_g, last):
    bx, s, _ = x.shape
    sc = xc.shape[1]
    at = lambda name: (w[name], layer)
    mods_l = (mods, layer, None)
    mods_c = (mods, layer, bx)
    flat = lambda a: a.reshape(1, bx * sc, a.shape[-1])
    unflat = lambda a: a.reshape(bx, sc, a.shape[-1])

    proj_w = tuple(at(name) for name in ("norm1_g", "w_mla", "w_lx", "w_lg", "w_hy", "q_norm_g", "kv_norm_g",
                                         "w_q", "w_qs", "w_k", "w_v"))
    qc, kc, vc, lxc, lgc, hyc = [unflat(a) for a in _in_proj(flat(xc), mods_c, proj_w, *_identity_rope_tables())]
    ql, kl, vl, lxl, lgl, hyl = _in_proj(x, mods_l, proj_w, *_rope_tables(s))

    br_c, h_c = _mixers(qc, kc, vc, lxc, hyc, w, layer, branch_out=not last)
    br_l, _ = _mixers(ql, kl, vl, lxl, hyl, w, layer, kv_ctx=(kc, vc), h0=h_c)

    merge_w = tuple(at(name) for name in ("w_gate", "w_br_a", "w_br_b", "w_br_c", "w_out"))
    ffn_w = tuple(at(name) for name in ("ffn_w_gate", "ffn_w_up", "ffn_w_down"))
    x = _merge(x, mods_l, at("norm1_g"), br_l[0], br_l[1], br_l[2], lgl, br_l[3], *merge_w)
    x = _ffn(x, mods_l, at("norm2_g"), *ffn_w, final_g, last)
    if not last:
        xcf = _merge(flat(xc), mods_c, at("norm1_g"), flat(br_c[0]), flat(br_c[1]), flat(br_c[2]), flat(lgc),
                     flat(br_c[3]), *merge_w)
        xc = unflat(_ffn(xcf, mods_c, at("norm2_g"), *ffn_w, final_g, False))
    return x, xc


def kernel(x, c, ctx, c_ctx, ada_w, ada_b, norm1_g, norm2_g, w_in, q_norm_g, w_uq, kv_norm_g, w_ukv,
           lru_conv_w, lru_conv_b, lru_wa, lru_ba, lru_wx, lru_bx, lru_lam,
           hy_conv_w, hy_conv_b, hy_w1, hy_b1, hy_w2, hy_b2, hy_freq, hy_w_out, hy_skip,
           w_br_a, w_br_b, w_br_c, w_out, ffn_w_gate, ffn_w_up, ffn_w_down, final_norm_g):
    depth = ada_w.shape[0]
    bx = x.shape[0]
    assert bx % 2 == 0 and bx + 1 <= 8
    cond = jnp.concatenate([c, c_ctx[None], jnp.zeros((8 - bx - 1, D_MODEL), F32)], axis=0)
    mods = _ada_mods(cond, ada_w, ada_b)[:, :, None, :]
    w = _prep_weights(dict(
        norm1_g=norm1_g, norm2_g=norm2_g, w_in=w_in, q_norm_g=q_norm_g, w_uq=w_uq,
        kv_norm_g=kv_norm_g, w_ukv=w_ukv, lru_conv_w=lru_conv_w, lru_conv_b=lru_conv_b,
        lru_wa=lru_wa, lru_ba=lru_ba, lru_wx=lru_wx, lru_bx=lru_bx, lru_lam=lru_lam,
        hy_conv_w=hy_conv_w, hy_conv_b=hy_conv_b, hy_w1=hy_w1, hy_b1=hy_b1, hy_w2=hy_w2, hy_b2=hy_b2,
        hy_freq=hy_freq, hy_w_out=hy_w_out, hy_skip=hy_skip, w_br_a=w_br_a, w_br_b=w_br_b,
        w_br_c=w_br_c, w_out=w_out, ffn_w_gate=ffn_w_gate, ffn_w_up=ffn_w_up, ffn_w_down=ffn_w_down))
    xc = ctx
    fg = final_norm_g[None]
    for layer in range(depth):
        x, xc = _layer(x, xc, mods, w, layer, fg, layer == depth - 1)
    return x
```

```python
import functools
import math

import numpy as np
import jax
import jax.numpy as jnp
from jax import lax
from jax.experimental import pallas as pl
from jax.experimental.pallas import tpu as pltpu

F32 = jnp.float32
BF16 = jnp.bfloat16

D_MODEL = 1024
GRID_W = 64
EPS = 1e-6

MLA_HEADS = 8
Q_LORA = 384
KV_LORA = 256
QK_NOPE = 64
QK_ROPE = 32
V_HEAD = 64
ROPE_BASE = 10000.0
SM_SCALE = (QK_NOPE + QK_ROPE) ** -0.5
HEAD_PAD = 128
MLA_Z = Q_LORA + KV_LORA + 2 * HEAD_PAD

LRU_WIDTH = 512
LRU_BLOCKS = 8
LRU_CONV = 4
LRU_C = 8.0

HY_WIDTH = 512
HY_ORDER = 2
HY_SHORT = 3
HY_EMB = 33
HY_EMB_PAD = 48
HY_HID = 64
HY_INNER = 2
HY_FAST_DECAY = 0.3
HY_SLOW_DECAY = 1.5
HY_DECAY_TARGET = 1e-2
HY_N2 = 128
HY_CB = 128
HY_GROUP = 4
HY_CH_GROUP = 16

FFN_HID = 2816
N_BRANCH = 3
MLA_IN = Q_LORA + KV_LORA + QK_ROPE
IN_SPLITS = (MLA_IN, MLA_IN + LRU_WIDTH, MLA_IN + 2 * LRU_WIDTH, MLA_IN + 2 * LRU_WIDTH + 3 * HY_WIDTH)

Q_PRESCALE = SM_SCALE * math.log2(math.e)
FLASH_Q_TILE = 1024
FLASH_KEY_CHUNK = 1024
TOKEN_TILE = 256
DENSE_TILE = 512
HALO = 8
VMEM_LIMIT = 56 * 1024 * 1024


def _params(*sem):
    return pltpu.CompilerParams(dimension_semantics=sem, vmem_limit_bytes=VMEM_LIMIT)


def _dot(a, b):
    return jnp.dot(a, b, preferred_element_type=F32)


def _rms(x, g):
    return x * lax.rsqrt(jnp.mean(x * x, axis=-1, keepdims=True) + EPS) * g


def _norm_mod(x, g, shift, scale):
    return _rms(x, g) * (1.0 + scale) + shift


def _arr(op):
    return op[0] if isinstance(op, tuple) else op


def _full_spec(op):
    once = pl.Buffered(1)
    if isinstance(op, tuple):
        arr, layer = op
        nd = arr.ndim
        return pl.BlockSpec((None,) + arr.shape[1:], lambda *_: (layer,) + (0,) * (nd - 1), pipeline_mode=once)
    nd = op.ndim
    return pl.BlockSpec(op.shape, lambda *_: (0,) * nd, pipeline_mode=once)


def _mod_spec(mods, j):
    arr, layer, row = mods
    if row is None:
        return pl.BlockSpec((None, 1, 1, D_MODEL), lambda b, i: (layer, b, 0, j))
    return pl.BlockSpec((None, 1, 1, D_MODEL), lambda b, i: (layer, row, 0, j))


def _ada_kernel(c_ref, w_ref, b_ref, o_ref):
    c = c_ref[...]
    s = (c * jax.nn.sigmoid(c)).astype(BF16)
    o_ref[0] = _dot(s, w_ref[0].astype(BF16)) + b_ref[0]


def _ada_mods(cond, ada_w, ada_b):
    depth, _, width = ada_w.shape
    tn = 1536
    return pl.pallas_call(
        _ada_kernel,
        grid=(depth, width // tn),
        in_specs=[pl.BlockSpec((8, D_MODEL), lambda l, j: (0, 0)),
                  pl.BlockSpec((1, D_MODEL, tn), lambda l, j: (l, 0, j)),
                  pl.BlockSpec((1, 1, tn), lambda l, j: (l, 0, j))],
        out_specs=pl.BlockSpec((1, 8, tn), lambda l, j: (l, 0, j)),
        out_shape=jax.ShapeDtypeStruct((depth, 8, width), F32),
        compiler_params=_params("arbitrary", "arbitrary"),
        name="ada_mods",
    )(cond, ada_w, ada_b.reshape(depth, 1, width))


def _inproj_kernel(x_ref, sh_ref, sc_ref, g_ref, wm_ref, wx_ref, wg_ref, wh_ref,
                   qg_ref, kvg_ref, wq_ref, wqs_ref, wk_ref, wv_ref, cos_ref, sin_ref,
                   q_ref, k_ref, v_ref, lx_ref, lg_ref, hy_ref):
    h = _norm_mod(x_ref[0], g_ref[...], sh_ref[0], sc_ref[0]).astype(BF16)
    lx_ref[0] = _dot(h, wx_ref[...])
    lg_ref[0] = _dot(h, wg_ref[...])
    hy_ref[0] = _dot(h, wh_ref[...]).astype(hy_ref.dtype)
    z = _dot(h, wm_ref[...])
    nq = _rms(z[:, :Q_LORA], qg_ref[...]).astype(BF16)
    nkv = _rms(z[:, Q_LORA:Q_LORA + KV_LORA], kvg_ref[...]).astype(BF16)
    pe = z[:, Q_LORA + KV_LORA:Q_LORA + KV_LORA + HEAD_PAD]
    pes = z[:, Q_LORA + KV_LORA + HEAD_PAD:]
    cos = cos_ref[...]
    sin = sin_ref[...]
    q = _dot(nq, wq_ref[...])
    qs = _dot(nq, wqs_ref[...])
    kn = _dot(nkv, wk_ref[...])
    k_pe = pe * cos + pes * sin
    for hd in range(MLA_HEADS):
        sl = slice(hd * HEAD_PAD, (hd + 1) * HEAD_PAD)
        q_ref[0, :, sl] = ((q[:, sl] * cos + qs[:, sl] * sin) * Q_PRESCALE).astype(BF16)
        k_ref[0, :, sl] = (kn[:, sl] + k_pe).astype(BF16)
    v_ref[0] = _dot(nkv, wv_ref[...]).astype(BF16)


def _in_proj(x, mods, weights, cos, sin):
    bx, t, _ = x.shape
    tm = DENSE_TILE
    tile = lambda w: pl.BlockSpec((1, tm, w), lambda b, i: (b, i, 0))
    if cos.shape[0] == tm:
        tab = pl.BlockSpec((tm, HEAD_PAD), lambda b, i: (0, 0))
    else:
        tab = pl.BlockSpec((tm, HEAD_PAD), lambda b, i: (i, 0))
    outs = ((MLA_HEADS * HEAD_PAD, BF16), (MLA_HEADS * HEAD_PAD, BF16), (MLA_HEADS * V_HEAD, BF16),
            (LRU_WIDTH, F32), (LRU_WIDTH, F32), (3 * HY_WIDTH, BF16))
    return pl.pallas_call(
        _inproj_kernel,
        grid=(bx, t // tm),
        in_specs=[tile(D_MODEL), _mod_spec(mods, 0), _mod_spec(mods, 1)]
                 + [_full_spec(w) for w in weights] + [tab, tab],
        out_specs=[tile(w) for w, _ in outs],
        out_shape=[jax.ShapeDtypeStruct((bx, t, w), dt) for w, dt in outs],
        compiler_params=_params("parallel", "parallel"),
        name="in_proj",
    )(x, mods[0], mods[0], *[_arr(w) for w in weights], cos, sin)


def _qk(q, k):
    return lax.dot_general(q, k, (((1,), (1,)), ((), ())), preferred_element_type=F32)


def _flash_kernel(*refs, chunks):
    q_ref, o_ref = refs[0], refs[-1]
    kv = refs[1:-1]
    state = [None, None]
    for src, off, size in chunks:
        k_ref, v_ref = kv[2 * src], kv[2 * src + 1]
        v = v_ref[0, off:off + size, :]
        for h in range(2):
            sl = slice(h * HEAD_PAD, (h + 1) * HEAD_PAD)
            s = _qk(q_ref[0, :, sl], k_ref[0, off:off + size, sl])
            m_blk = jnp.max(s, axis=-1, keepdims=True)
            if state[h] is None:
                m = m_blk
                p = jnp.exp2(s - m)
                l = jnp.sum(p, axis=-1, keepdims=True)
                acc = _dot(p.astype(BF16), v)
            else:
                m_old, l, acc = state[h]
                m = jnp.maximum(m_old, m_blk)
                alpha = jnp.exp2(m_old - m)
                p = jnp.exp2(s - m)
                l = alpha * l + jnp.sum(p, axis=-1, keepdims=True)
                acc = alpha * acc + _dot(p.astype(BF16), v)
            state[h] = (m, l, acc)
    outs = [acc / l for _, l, acc in state]
    lane = lax.broadcasted_iota(jnp.int32, outs[0].shape, 1)
    o_ref[0] = jnp.where(lane < V_HEAD, outs[0], outs[1]).astype(o_ref.dtype)


def _flash(q, kc, vc, kl=None, vl=None):
    bx, t, _ = q.shape
    tq = min(FLASH_Q_TILE, t)
    sc = kc.shape[1]
    pairs = MLA_HEADS // 2
    qspec = pl.BlockSpec((1, tq, 2 * HEAD_PAD), lambda b, hp, i: (b, i, hp))
    kspec = lambda n: pl.BlockSpec((1, n, 2 * HEAD_PAD), lambda b, hp, i: (b, 0, hp))
    vspec = lambda n: pl.BlockSpec((1, n, 2 * V_HEAD), lambda b, hp, i: (b, 0, hp))
    in_specs = [qspec, kspec(sc), vspec(sc)]
    args = [q, kc, vc]
    chunks = [(0, 0, sc)]
    if kl is not None:
        sl = kl.shape[1]
        size = min(FLASH_KEY_CHUNK, sl)
        chunks += [(1, off, size) for off in range(0, sl, size)]
        in_specs += [kspec(sl), vspec(sl)]
        args += [kl, vl]
    return pl.pallas_call(
        functools.partial(_flash_kernel, chunks=tuple(chunks)),
        grid=(bx, pairs, t // tq),
        in_specs=in_specs,
        out_specs=pl.BlockSpec((1, tq, 2 * V_HEAD), lambda b, hp, i: (b, i, hp)),
        out_shape=jax.ShapeDtypeStruct((bx, t, MLA_HEADS * V_HEAD), BF16),
        compiler_params=_params("parallel", "parallel", "arbitrary"),
        name="flash",
    )(*args)


def _lru_kernel(xf_ref, xfp_ref, xfn_ref, xb_ref, xbp_ref, xbn_ref, cw_ref, cb_ref,
                wf_ref, bf_ref, wb_ref, bb_ref, lam_ref, h0_ref,
                hf_ref, hb_ref, hl_ref, af_s, bfw_s, ab_s, bbw_s, pad_s, carry_s):
    t = pl.program_id(0)
    nt = pl.num_programs(0)
    bx, tm, _ = xf_ref.shape

    @pl.when(t == 0)
    def _():
        carry_s[...] = h0_ref[...]

    def coeffs(b, x_ref, prev_ref, next_ref, tile, w_ref, bias_ref, lam, a_s, b_s):
        x = x_ref[b]
        pad_s[0:HALO, :] = jnp.where(tile > 0, prev_ref[b], 0.0)
        pad_s[HALO:HALO + tm, :] = x
        pad_s[HALO + tm:, :] = jnp.where(tile < nt - 1, next_ref[b], 0.0)
        u = (cw_ref[0:1, :] * pad_s[HALO - 2:HALO - 2 + tm, :] + cw_ref[1:2, :] * pad_s[HALO - 1:HALO - 1 + tm, :]
             + cw_ref[2:3, :] * x + cw_ref[3:4, :] * pad_s[HALO + 1:HALO + 1 + tm, :] + cb_ref[...])
        y = _dot(u.astype(BF16), w_ref[...]) + bias_ref[...]
        r = jax.nn.sigmoid(y[:, :LRU_WIDTH])
        i = jax.nn.sigmoid(y[:, LRU_WIDTH:])
        neg = -lam
        softplus = jnp.maximum(neg, 0.0) + jnp.log1p(jnp.exp(-jnp.abs(neg)))
        a = jnp.exp((-LRU_C) * r * softplus)
        a_s[b] = a
        b_s[b] = jnp.sqrt(1.0 - a * a) * (i * u)

    for b in range(bx):
        coeffs(b, xf_ref, xfp_ref, xfn_ref, t, wf_ref, bf_ref, lam_ref[0:1, :], af_s, bfw_s)
        coeffs(b, xb_ref, xbp_ref, xbn_ref, nt - 1 - t, wb_ref, bb_ref, lam_ref[1:2, :], ab_s, bbw_s)

    def body(j, hs):
        jb = tm - 1 - j
        out = []
        for b in range(bx):
            hf = af_s[b, pl.ds(j, 1), :] * hs[2 * b] + bfw_s[b, pl.ds(j, 1), :]
            hf_ref[b, pl.ds(j, 1), :] = hf
            hb = ab_s[b, pl.ds(jb, 1), :] * hs[2 * b + 1] + bbw_s[b, pl.ds(jb, 1), :]
            hb_ref[b, pl.ds(jb, 1), :] = hb
            out += [hf, hb]
        return tuple(out)

    init = tuple(carry_s[b, d:d + 1, :] for b in range(bx) for d in range(2))
    hs = lax.fori_loop(0, tm, body, init, unroll=8)
    for b in range(bx):
        for d in range(2):
            carry_s[b, d:d + 1, :] = hs[2 * b + d]
    hl_ref[...] = carry_s[...]


def _lru(lx, conv_w, conv_b, wf, biasf, wb, biasb, lam, h0):
    bx, t, w = lx.shape
    tm = TOKEN_TILE
    nt = t // tm
    hb_per = tm // HALO
    last_halo = t // HALO - 1
    fwd = lambda i: (0, i, 0)
    bwd = lambda i: (0, nt - 1 - i, 0)
    prev_of = lambda f: (lambda i: (0, jnp.maximum(f(i)[1] * hb_per - 1, 0), 0))
    next_of = lambda f: (lambda i: (0, jnp.minimum((f(i)[1] + 1) * hb_per, last_halo), 0))
    tile = lambda f: pl.BlockSpec((bx, tm, w), f)
    halo = lambda f: pl.BlockSpec((bx, HALO, w), f)
    state = pl.BlockSpec((bx, 2, w), lambda i: (0, 0, 0))
    return pl.pallas_call(
        _lru_kernel,
        grid=(nt,),
        in_specs=[tile(fwd), halo(prev_of(fwd)), halo(next_of(fwd)),
                  tile(bwd), halo(prev_of(bwd)), halo(next_of(bwd)),
                  _full_spec(conv_w), _full_spec(conv_b), _full_spec(wf), _full_spec(biasf),
                  _full_spec(wb), _full_spec(biasb), _full_spec(lam), state],
        out_specs=[tile(fwd), tile(bwd), state],
        out_shape=[jax.ShapeDtypeStruct((bx, t, w), F32), jax.ShapeDtypeStruct((bx, t, w), F32),
                   jax.ShapeDtypeStruct((bx, 2, w), F32)],
        scratch_shapes=[pltpu.VMEM((bx, tm, w), F32)] * 4
                       + [pltpu.VMEM((tm + 2 * HALO, w), F32), pltpu.VMEM((bx, 2, w), F32)],
        compiler_params=_params("arbitrary"),
        name="rglru",
    )(lx, lx, lx, lx, lx, lx, *[_arr(w) for w in (conv_w, conv_b, wf, biasf, wb, biasb, lam)], h0)


def _split_bf16(x):
    hi = x.astype(BF16)
    return hi, (x - hi.astype(F32)).astype(BF16)


def _dot3(a, b):
    ah, al = _split_bf16(a)
    bh, bl = _split_bf16(b)
    return _dot(ah, bh) + (_dot(ah, bl) + _dot(al, bh))


def _hyfilt_kernel(z_ref, w1_ref, b1_ref, w2_ref, b2_ref, fr_ref, wo_ref, dec_ref, o_ref):
    fr = fr_ref[...]
    h = jnp.sin(fr * (_dot3(z_ref[...], w1_ref[...]) + b1_ref[...]))
    for j in range(HY_INNER):
        h = jnp.sin(fr * (_dot3(h, w2_ref[j]) + b2_ref[j:j + 1, :]))
    dec = dec_ref[...]
    for g in range(2 * HY_ORDER):
        sl = slice(g * HY_WIDTH, (g + 1) * HY_WIDTH)
        o_ref[:, sl] = _dot3(h, wo_ref[:, sl]) * dec


def _hy_filters(z, w1, b1, w2, b2, freq, w_out, decay):
    n = z.shape[0]
    tn = min(n, 512)
    width = 2 * HY_ORDER * HY_WIDTH
    return pl.pallas_call(
        _hyfilt_kernel,
        grid=(n // tn,),
        in_specs=[pl.BlockSpec((tn, HY_EMB_PAD), lambda i: (i, 0)), _full_spec(w1), _full_spec(b1),
                  _full_spec(w2), _full_spec(b2), _full_spec(freq), _full_spec(w_out),
                  pl.BlockSpec((tn, HY_WIDTH), lambda i: (i, 0))],
        out_specs=pl.BlockSpec((tn, width), lambda i: (i, 0)),
        out_shape=jax.ShapeDtypeStruct((n, width), F32),
        compiler_params=_params("parallel"),
        name="hyena_filters",
    )(z, *[_arr(w) for w in (w1, b1, w2, b2, freq, w_out)], decay)


def _bitrev(k, bits):
    r = jnp.zeros_like(k)
    for b in range(bits):
        r = r | (((k >> b) & 1) << (bits - 1 - b))
    return r


def _fft_lead_fwd(sre, sim, wr_ref, wi_ref, n1):
    half = n1 // 2
    m = half
    first = True
    while m >= 1:
        shift = int(math.log2(m))
        stride = half // m

        def body(q, c, m=m, shift=shift, stride=stride, first=first):
            grp = q >> shift
            j = q - (grp << shift)
            i0 = (grp << (shift + 1)) + j
            i1 = i0 + m
            wr = wr_ref[j * stride]
            wi = wi_ref[j * stride]
            ar, ai = sre[i0], sim[i0]
            if first:
                dr, di = ar, ai
            else:
                br, bi = sre[i1], sim[i1]
                sre[i0] = ar + br
                sim[i0] = ai + bi
                dr, di = ar - br, ai - bi
            sre[i1] = dr * wr - di * wi
            sim[i1] = dr * wi + di * wr
            return c

        lax.fori_loop(0, half, body, 0)
        first = False
        m //= 2


def _fft_lead_inv(sre, sim, wr_ref, wi_ref, n1):
    half = n1 // 2
    m = 1
    while m <= half:
        shift = int(math.log2(m))
        stride = half // m
        last = m == half

        def body(q, c, m=m, shift=shift, stride=stride, last=last):
            grp = q >> shift
            j = q - (grp << shift)
            i0 = (grp << (shift + 1)) + j
            i1 = i0 + m
            wr = wr_ref[j * stride]
            wi = wi_ref[j * stride]
            ar, ai = sre[i0], sim[i0]
            br, bi = sre[i1], sim[i1]
            tr = br * wr + bi * wi
            ti = bi * wr - br * wi
            sre[i0] = ar + tr
            sim[i0] = ai + ti
            if not last:
                sre[i1] = ar - tr
                sim[i1] = ai - ti
            return c

        lax.fori_loop(0, half, body, 0)
        m *= 2


def _spectrum_loop(sre, sim, f2_s, tw, n1, emit):
    tw0r_ref, tw0i_ref, wgr_ref, wgi_ref = tw
    bits = int(math.log2(n1))
    group = min(HY_GROUP, n1)

    def body(kb, tws):
        ks, xs = [], []
        for g in range(group):
            k1 = kb * group + g
            blk = _bitrev(k1, bits)
            tr, ti = tws[2 * g], tws[2 * g + 1]
            ar, ai = sre[blk], sim[blk]
            xs.append(jnp.concatenate([ar * tr - ai * ti, ar * ti + ai * tr], axis=0).astype(BF16))
            ks.append((k1, blk))
        b2 = _dot(f2_s[...], jnp.concatenate(xs, axis=1))
        emit(ks, b2, tws)
        wgr, wgi = wgr_ref[...], wgi_ref[...]
        nxt = []
        for g in range(group):
            tr, ti = tws[2 * g], tws[2 * g + 1]
            nxt += [tr * wgr - ti * wgi, tr * wgi + ti * wgr]
        return tuple(nxt)

    init = tuple(r[g] for g in range(group) for r in (tw0r_ref, tw0i_ref))
    lax.fori_loop(0, n1 // group, body, init)


def _group_cols(b2, g, cb):
    return b2[:HY_N2, g * cb:(g + 1) * cb], b2[HY_N2:, g * cb:(g + 1) * cb]


def _hyspec_kernel(wr_ref, wi_ref, hf_ref, hb_ref, skip_ref, f2_ref, tw0r_ref, tw0i_ref, wgr_ref, wgi_ref,
                   k_ref, sre, sim, f2_s):
    n = hf_ref.shape[0]
    n1 = 2 * n // HY_N2
    half = n1 // 2
    cb = hf_ref.shape[1]
    f2_s[...] = f2_ref[...].astype(BF16)
    for direction, h_ref in enumerate((hf_ref, hb_ref)):
        sre[0:half] = h_ref[...].reshape(half, HY_N2, cb)
        sim[0:half] = jnp.zeros((half, HY_N2, cb), F32)
        _fft_lead_fwd(sre, sim, wr_ref, wi_ref, n1)

        def emit(ks, b2, tws, direction=direction):
            for g, (k1, _) in enumerate(ks):
                br, bi = _group_cols(b2, g, cb)
                if direction == 0:
                    k_ref[0, k1, 0:HY_N2, :] = br + skip_ref[0]
                    k_ref[0, k1, HY_N2:, :] = bi
                else:
                    k_ref[0, k1, 0:HY_N2, :] = k_ref[0, k1, 0:HY_N2, :] + br
                    k_ref[0, k1, HY_N2:, :] = k_ref[0, k1, HY_N2:, :] - bi

        _spectrum_loop(sre, sim, f2_s, (tw0r_ref, tw0i_ref, wgr_ref, wgi_ref), n1, emit)


def _hy_spectra(filt, skip, consts):
    n = filt.shape[0]
    n1 = 2 * n // HY_N2
    cbn = HY_WIDTH // HY_CB
    smem = pl.BlockSpec(memory_space=pltpu.SMEM)
    col = lambda direction: (lambda o, c: (0, (direction * HY_ORDER + o) * cbn + c))
    tw_names = ("tw0r", "tw0i", "wgr", "wgi")
    return pl.pallas_call(
        _hyspec_kernel,
        grid=(HY_ORDER, cbn),
        in_specs=[smem, smem,
                  pl.BlockSpec((n, HY_CB), col(0)), pl.BlockSpec((n, HY_CB), col(1)),
                  pl.BlockSpec((None, 1, 1, HY_CB), lambda o, c: (skip[1], o, 0, c)),
                  _full_spec(consts["f2"])] + [_full_spec(consts[k]) for k in tw_names],
        out_specs=pl.BlockSpec((1, n1, 2 * HY_N2, HY_CB), lambda o, c: (o, 0, 0, c)),
        out_shape=jax.ShapeDtypeStruct((HY_ORDER, n1, 2 * HY_N2, HY_WIDTH), F32),
        scratch_shapes=[pltpu.VMEM((n1, HY_N2, HY_CB), F32)] * 2 + [pltpu.VMEM((2 * HY_N2, 2 * HY_N2), BF16)],
        compiler_params=_params("parallel", "parallel"),
        name="hyena_spectra",
    )(consts["wr"], consts["wi"], filt, filt, skip[0], consts["f2"], *[consts[k] for k in tw_names])


def _short_conv3(x, pad_s, w_ref, b_ref):
    n = x.shape[0]
    pad_s[HALO:HALO + n, :] = x
    return (w_ref[0:1, :] * pad_s[HALO - 1:HALO - 1 + n, :] + w_ref[1:2, :] * x
            + w_ref[2:3, :] * pad_s[HALO + 1:HALO + 1 + n, :] + b_ref[...])


def _hyconv_kernel(wr_ref, wi_ref, u_ref, g_ref, ucw_ref, ucb_ref, gcw_ref, gcb_ref, k_ref,
                   f2_ref, f2i_ref, tw0r_ref, tw0i_ref, wgr_ref, wgi_ref, o_ref,
                   sre, sim, pad_s, f2_s, f2i_s, *, conv_u):
    n = u_ref.shape[1]
    cb = u_ref.shape[2]
    n1 = 2 * n // HY_N2
    half = n1 // 2
    f2_s[...] = f2_ref[...].astype(BF16)
    f2i_s[...] = f2i_ref[...].astype(BF16)
    margin = jnp.zeros((HALO, cb), F32)
    pad_s[0:HALO, :] = margin
    pad_s[HALO + n:, :] = margin

    for b, s in enumerate((sre, sim)):
        u = u_ref[b].astype(F32)
        if conv_u:
            u = _short_conv3(u, pad_s, ucw_ref, ucb_ref)
        s[0:half] = u.reshape(half, HY_N2, cb)
    _fft_lead_fwd(sre, sim, wr_ref, wi_ref, n1)

    def emit(ks, b2, tws):
        ps = []
        for g, (k1, _) in enumerate(ks):
            br, bi = _group_cols(b2, g, cb)
            kr = k_ref[0, k1, 0:HY_N2, :]
            ki = k_ref[0, k1, HY_N2:, :]
            ps.append(jnp.concatenate([br * kr - bi * ki, br * ki + bi * kr], axis=0).astype(BF16))
        c2 = _dot(f2i_s[...], jnp.concatenate(ps, axis=1))
        for g, (_, blk) in enumerate(ks):
            cr, ci = _group_cols(c2, g, cb)
            tr, ti = tws[2 * g], tws[2 * g + 1]
            sre[blk] = cr * tr + ci * ti
            sim[blk] = ci * tr - cr * ti

    _spectrum_loop(sre, sim, f2_s, (tw0r_ref, tw0i_ref, wgr_ref, wgi_ref), n1, emit)
    _fft_lead_inv(sre, sim, wr_ref, wi_ref, n1)

    for b, s in enumerate((sre, sim)):
        gate = _short_conv3(g_ref[b].astype(F32), pad_s, gcw_ref, gcb_ref)
        o_ref[b] = (gate * s[0:half].reshape(n, cb)).astype(o_ref.dtype)


def _hy_conv(u, u_col, g, g_col, conv_w, conv_b, spectra, order, consts, conv_u):
    bx, n, _ = u.shape
    n1 = 2 * n // HY_N2
    cbn = HY_WIDTH // HY_CB
    smem = pl.BlockSpec(memory_space=pltpu.SMEM)
    data = lambda col: pl.BlockSpec((2, n, HY_CB), lambda c, p, col=col: (p, 0, col + c))
    layer = conv_w[1]
    wrow = lambda rows, col: pl.BlockSpec((None, rows, HY_CB), lambda c, p, col=col: (layer, 0, col + c))
    ucol = u_col if conv_u else g_col
    const_names = ("f2", "f2i", "tw0r", "tw0i", "wgr", "wgi")
    dft = pltpu.VMEM((2 * HY_N2, 2 * HY_N2), BF16)
    return pl.pallas_call(
        functools.partial(_hyconv_kernel, conv_u=conv_u),
        grid=(cbn, bx // 2),
        in_specs=[smem, smem, data(u_col), data(g_col),
                  wrow(HY_SHORT, ucol), wrow(1, ucol), wrow(HY_SHORT, g_col), wrow(1, g_col),
                  pl.BlockSpec((1, n1, 2 * HY_N2, HY_CB), lambda c, p: (order, 0, 0, c))]
                 + [_full_spec(consts[k]) for k in const_names],
        out_specs=pl.BlockSpec((2, n, HY_CB), lambda c, p: (p, 0, c)),
        out_shape=jax.ShapeDtypeStruct((bx, n, HY_WIDTH), BF16),
        scratch_shapes=[pltpu.VMEM((n1, HY_N2, HY_CB), F32)] * 2
                       + [pltpu.VMEM((n + 2 * HALO, HY_CB), F32), dft, dft],
        compiler_params=_params("parallel", "arbitrary"),
        name="hyena_conv",
    )(consts["wr"], consts["wi"], u, g, conv_w[0], conv_b[0], conv_w[0], conv_b[0], spectra,
      *[consts[k] for k in const_names])


def _hy_consts(n):
    big_n = 2 * n
    n1 = big_n // HY_N2
    group = min(HY_GROUP, n1)
    q = np.arange(max(n1 // 2, 1), dtype=np.float64)
    ang1 = 2.0 * np.pi * q / n1
    idx = np.arange(HY_N2, dtype=np.float64)
    ang2 = 2.0 * np.pi * np.outer(idx, idx) / HY_N2
    c, s = np.cos(ang2), np.sin(ang2)
    f2 = np.block([[c, s], [-s, c]])
    f2i = np.block([[c, -s], [s, c]]) / big_n
    lane = np.ones((1, 1, HY_CB))
    ang0 = 2.0 * np.pi * np.arange(group)[:, None, None] * idx[None, :, None] / big_n
    angg = 2.0 * np.pi * group * idx[:, None] / big_n
    return {
        "wr": jnp.asarray(np.cos(ang1), F32), "wi": jnp.asarray(-np.sin(ang1), F32),
        "f2": jnp.asarray(f2, F32), "f2i": jnp.asarray(f2i, F32),
        "tw0r": jnp.asarray(np.cos(ang0) * lane, F32), "tw0i": jnp.asarray(-np.sin(ang0) * lane, F32),
        "wgr": jnp.asarray(np.cos(angg) * lane[0], F32), "wgi": jnp.asarray(-np.sin(angg) * lane[0], F32),
    }


def _hy_tables(n):
    t = np.linspace(0.0, 1.0, n, dtype=np.float32)[:, None].astype(np.float64)
    bands = (HY_EMB - 1) // 2
    w = 2.0 * np.pi * np.arange(n, dtype=np.float64) / n
    f = np.linspace(1e-4, bands - 1, bands, dtype=np.float32).astype(np.float64)
    ang = w[:, None] * f[None, :]
    z = np.concatenate([t, np.cos(ang), -np.sin(ang), np.zeros((n, HY_EMB_PAD - HY_EMB))], axis=-1)
    max_decay = math.log(HY_DECAY_TARGET) / HY_FAST_DECAY
    min_decay = math.log(HY_DECAY_TARGET) / HY_SLOW_DECAY
    deltas = np.abs(np.linspace(min_decay, max_decay, HY_WIDTH, dtype=np.float32).astype(np.float64))
    return jnp.asarray(z, F32), jnp.asarray(np.exp(-t * deltas), F32)


def _cm_short_conv(x, taps, masks):
    first_lane, last_lane, first_row, last_row = masks
    n1h = x.shape[0]
    r = pltpu.roll(x, 1, axis=1)
    prev = jnp.where(first_lane, jnp.where(first_row, 0.0, pltpu.roll(r, 1, axis=0)), r)
    l = pltpu.roll(x, HY_N2 - 1, axis=1)
    nxt = jnp.where(last_lane, jnp.where(last_row, 0.0, pltpu.roll(l, n1h - 1, axis=0)), l)
    return taps[0] * prev + taps[1] * x + taps[2] * nxt + taps[3]


def _cm_masks(n1h):
    lane = lax.broadcasted_iota(jnp.int32, (n1h, HY_N2), 1)
    row = lax.broadcasted_iota(jnp.int32, (n1h, HY_N2), 0)
    return lane == 0, lane == HY_N2 - 1, row == 0, row == n1h - 1


def _cm_taps(cw_ref, cb_ref, layer, ch):
    width = 3 * HY_WIDTH
    base = layer * HY_SHORT * width + ch
    return cw_ref[base], cw_ref[base + width], cw_ref[base + 2 * width], cb_ref[layer * width + ch]


def _cm_twiddle(a2, n1, twr, twi, conj):
    out = []
    for h in range(2):
        ar = a2[:n1, h * HY_N2:(h + 1) * HY_N2]
        ai = a2[n1:, h * HY_N2:(h + 1) * HY_N2]
        if conj:
            out.append((ar * twr + ai * twi, ai * twr - ar * twi))
        else:
            out.append((ar * twr - ai * twi, ar * twi + ai * twr))
    return out


def _hycm_spec_kernel(skip_ref, hf_ref, hb_ref, f1r_ref, f2t_ref, twr_ref, twi_ref, k_ref, f1_s, f2t_s, *, layer):
    order, cblk = pl.program_id(0), pl.program_id(1)
    cb, n1h, _ = hf_ref.shape
    n1 = 2 * n1h
    f1_s[...] = f1r_ref[...].astype(BF16)
    f2t_s[...] = f2t_ref[...].astype(BF16)
    twr, twi = twr_ref[...], twi_ref[...]

    def group(gi, carry):
        base = gi * HY_CH_GROUP
        blocks = []
        for j in range(HY_CH_GROUP):
            x2 = jnp.concatenate([hf_ref[base + j], hb_ref[base + j]], axis=1).astype(BF16)
            for re, im in _cm_twiddle(_dot(f1_s[...], x2), n1, twr, twi, False):
                blocks.append(jnp.concatenate([re, im], axis=1).astype(BF16))
        b2 = _dot(jnp.concatenate(blocks, axis=0), f2t_s[...])
        for j in range(HY_CH_GROUP):
            bf = b2[(2 * j) * n1:(2 * j + 1) * n1]
            bb = b2[(2 * j + 1) * n1:(2 * j + 2) * n1]
            skip = skip_ref[(layer * HY_ORDER + order) * HY_WIDTH + cblk * cb + base + j]
            k_ref[0, base + j] = jnp.concatenate([bf[:, :HY_N2] + bb[:, :HY_N2] + skip,
                                                  bf[:, HY_N2:] - bb[:, HY_N2:]], axis=1)
        return carry

    lax.fori_loop(0, cb // HY_CH_GROUP, group, 0)


def _hycm_spectra(filt_cm, skip, consts):
    _, n1h, _ = filt_cm.shape
    n1 = 2 * n1h
    nblk = HY_WIDTH // HY_CB
    smem = pl.BlockSpec(memory_space=pltpu.SMEM)
    blk = lambda direction: pl.BlockSpec((HY_CB, n1h, HY_N2),
                                         lambda o, c: ((direction * HY_ORDER + o) * nblk + c, 0, 0))
    names = ("f1r", "f2t", "twr", "twi")
    return pl.pallas_call(
        functools.partial(_hycm_spec_kernel, layer=skip[1]),
        grid=(HY_ORDER, nblk),
        in_specs=[smem, blk(0), blk(1)] + [_full_spec(consts[k]) for k in names],
        out_specs=pl.BlockSpec((1, HY_CB, n1, 2 * HY_N2), lambda o, c: (o, c, 0, 0)),
        out_shape=jax.ShapeDtypeStruct((HY_ORDER, HY_WIDTH, n1, 2 * HY_N2), F32),
        scratch_shapes=[pltpu.VMEM((2 * n1, n1h), BF16), pltpu.VMEM((2 * HY_N2, 2 * HY_N2), BF16)],
        compiler_params=_params("parallel", "parallel"),
        name="hyena_spectra_cm",
    )(skip[0], filt_cm, filt_cm, *[consts[k] for k in names])


def _hycm_conv_kernel(cw_ref, cb_ref, u_ref, g_ref, k_ref, f1_ref, f1i_ref, f2t_ref, f2ti_ref, twr_ref, twi_ref,
                      o_ref, f1_s, f1i_s, f2t_s, f2ti_s, *, conv_u, u_ch0, g_ch0, layer):
    cblk = pl.program_id(0)
    _, cb, n1h, _ = u_ref.shape
    n1 = 2 * n1h
    for dst, src in ((f1_s, f1_ref), (f1i_s, f1i_ref), (f2t_s, f2t_ref), (f2ti_s, f2ti_ref)):
        dst[...] = src[...].astype(BF16)
    twr, twi = twr_ref[...], twi_ref[...]
    masks = _cm_masks(n1h)
    group_n = HY_CH_GROUP

    def group(gi, carry):
        base = gi * group_n
        x2s = []
        for j in range(group_n):
            xs = []
            for b in range(2):
                x = u_ref[b, base + j].astype(F32)
                if conv_u:
                    x = _cm_short_conv(x, _cm_taps(cw_ref, cb_ref, layer, u_ch0 + cblk * cb + base + j), masks)
                xs.append(x)
            x2s.append(jnp.concatenate(xs, axis=0).astype(BF16))
        blocks = []
        for j in range(0, group_n, 2):
            a2 = _dot(f1_s[...], jnp.concatenate([x2s[j], x2s[j + 1]], axis=1))
            for re, im in _cm_twiddle(a2, n1, twr, twi, False):
                blocks.append(jnp.concatenate([re, im], axis=1).astype(BF16))
        b2 = _dot(jnp.concatenate(blocks, axis=0), f2t_s[...])
        prods = []
        for j in range(group_n):
            br = b2[j * n1:(j + 1) * n1, :HY_N2]
            bi = b2[j * n1:(j + 1) * n1, HY_N2:]
            kr = k_ref[0, base + j, :, 0:HY_N2]
            ki = k_ref[0, base + j, :, HY_N2:]
            prods.append(jnp.concatenate([br * kr - bi * ki, br * ki + bi * kr], axis=1).astype(BF16))
        c2 = _dot(jnp.concatenate(prods, axis=0), f2ti_s[...])
        cols = []
        for j in range(group_n):
            cr = c2[j * n1:(j + 1) * n1, :HY_N2]
            ci = c2[j * n1:(j + 1) * n1, HY_N2:]
            cols.append(jnp.concatenate([cr * twr + ci * twi, ci * twr - cr * twi], axis=0).astype(BF16))
        for j in range(0, group_n, 2):
            y2 = _dot(f1i_s[...], jnp.concatenate([cols[j], cols[j + 1]], axis=1))
            for h in range(2):
                ch = base + j + h
                taps = _cm_taps(cw_ref, cb_ref, layer, g_ch0 + cblk * cb + ch)
                for b in range(2):
                    y = y2[b * n1h:(b + 1) * n1h, h * HY_N2:(h + 1) * HY_N2]
                    gate = _cm_short_conv(g_ref[b, ch].astype(F32), taps, masks)
                    o_ref[b, ch] = (gate * y).astype(o_ref.dtype)
        return carry

    lax.fori_loop(0, cb // group_n, group, 0)


def _hycm_conv(u, u_ch0, g, g_ch0, conv_w, conv_b, spectra, order, consts, conv_u):
    bx, _, n1h, _ = u.shape
    n1 = 2 * n1h
    nblk = HY_WIDTH // HY_CB
    smem = pl.BlockSpec(memory_space=pltpu.SMEM)
    data = lambda ch0: pl.BlockSpec((2, HY_CB, n1h, HY_N2), lambda c, p, ch0=ch0: (p, ch0 // HY_CB + c, 0, 0))
    names = ("f1", "f1i", "f2t", "f2ti", "twr", "twi")
    dft = pltpu.VMEM((2 * HY_N2, 2 * HY_N2), BF16)
    return pl.pallas_call(
        functools.partial(_hycm_conv_kernel, conv_u=conv_u, u_ch0=u_ch0, g_ch0=g_ch0, layer=conv_w[1]),
        grid=(nblk, bx // 2),
        in_specs=[smem, smem, data(u_ch0), data(g_ch0),
                  pl.BlockSpec((1, HY_CB, n1, 2 * HY_N2), lambda c, p: (order, c, 0, 0))]
                 + [_full_spec(consts[k]) for k in names],
        out_specs=pl.BlockSpec((2, HY_CB, n1h, HY_N2), lambda c, p: (p, c, 0, 0)),
        out_shape=jax.ShapeDtypeStruct((bx, HY_WIDTH, n1h, HY_N2), BF16),
        scratch_shapes=[pltpu.VMEM((2 * n1, n1), BF16), pltpu.VMEM((n1, 2 * n1), BF16), dft, dft],
        compiler_params=_params("parallel", "arbitrary"),
        name="hyena_conv_cm",
    )(conv_w[0], conv_b[0], u, g, spectra, *[consts[k] for k in names])


def _cm_consts(n):
    big_n = 2 * n
    n1h = n // HY_N2
    n1 = 2 * n1h
    a1 = 2.0 * np.pi * np.outer(np.arange(n1), np.arange(n1h)) / n1
    c1, s1 = np.cos(a1), np.sin(a1)
    idx = np.arange(HY_N2, dtype=np.float64)
    a2 = 2.0 * np.pi * np.outer(idx, idx) / HY_N2
    c2, s2 = np.cos(a2), np.sin(a2)
    at = 2.0 * np.pi * np.outer(np.arange(n1), idx) / big_n
    f1 = np.block([[c1, s1], [-s1, c1]])
    return {
        "f1": jnp.asarray(f1, F32), "f1r": jnp.asarray(f1[:, :n1h], F32),
        "f1i": jnp.asarray(np.block([[c1.T, -s1.T], [s1.T, c1.T]]), F32),
        "f2t": jnp.asarray(np.block([[c2, -s2], [s2, c2]]), F32),
        "f2ti": jnp.asarray(np.block([[c2, s2], [-s2, c2]]) / big_n, F32),
        "twr": jnp.asarray(np.cos(at), F32), "twi": jnp.asarray(-np.sin(at), F32),
    }


def _gelu_tanh(x):
    return x * (0.5 * (1.0 + jnp.tanh(math.sqrt(2.0 / math.pi) * (x + 0.044715 * (x * x * x)))))


def _merge_kernel(x_ref, sh_ref, sc_ref, gt_ref, g_ref, ya_ref, hf_ref, hb_ref, lg_ref, yc_ref,
                  wgate_ref, wa_ref, wb_ref, wc_ref, wo_ref, o_ref):
    x = x_ref[0]
    h = _norm_mod(x, g_ref[...], sh_ref[0], sc_ref[0]).astype(BF16)
    yb = ((hf_ref[0] + hb_ref[0]) * _gelu_tanh(lg_ref[0])).astype(BF16)
    y = None
    for j, (br, w_ref) in enumerate(((ya_ref[0], wa_ref), (yb, wb_ref), (yc_ref[0], wc_ref))):
        gate = jax.nn.sigmoid(_dot(h, wgate_ref[:, j * D_MODEL:(j + 1) * D_MODEL]))
        term = gate * _dot(br, w_ref[...])
        y = term if y is None else y + term
    o_ref[0] = x + gt_ref[0] * _dot(y.astype(BF16), wo_ref[...])


def _merge(x, mods, norm_g, ya, hf, hb, lg, yc, wgate, wa, wb, wc, wo):
    bx, t, _ = x.shape
    tm = DENSE_TILE
    tile = lambda w: pl.BlockSpec((1, tm, w), lambda b, i: (b, i, 0))
    return pl.pallas_call(
        _merge_kernel,
        grid=(bx, t // tm),
        in_specs=[tile(D_MODEL), _mod_spec(mods, 0), _mod_spec(mods, 1), _mod_spec(mods, 2), _full_spec(norm_g),
                  tile(MLA_HEADS * V_HEAD), tile(LRU_WIDTH), tile(LRU_WIDTH), tile(LRU_WIDTH), tile(HY_WIDTH),
                  _full_spec(wgate), _full_spec(wa), _full_spec(wb), _full_spec(wc), _full_spec(wo)],
        out_specs=tile(D_MODEL),
        out_shape=jax.ShapeDtypeStruct(x.shape, F32),
        compiler_params=_params("parallel", "parallel"),
        name="merge",
    )(x, mods[0], mods[0], mods[0], _arr(norm_g), ya, hf, hb, lg, yc, *[_arr(w) for w in (wgate, wa, wb, wc, wo)])


def _ffn_kernel(x_ref, sh_ref, sc_ref, gt_ref, g_ref, wg_ref, wu_ref, wd_ref, fg_ref, o_ref, *, final):
    x = x_ref[0]
    h = _norm_mod(x, g_ref[...], sh_ref[0], sc_ref[0]).astype(BF16)
    gate = _dot(h, wg_ref[...])
    act = (gate * jax.nn.sigmoid(gate) * _dot(h, wu_ref[...])).astype(BF16)
    y = x + gt_ref[0] * _dot(act, wd_ref[...])
    o_ref[0] = _rms(y, fg_ref[...]) if final else y


def _ffn(x, mods, norm_g, wg, wu, wd, final_g, final):
    bx, t, _ = x.shape
    tm = DENSE_TILE
    tile = pl.BlockSpec((1, tm, D_MODEL), lambda b, i: (b, i, 0))
    return pl.pallas_call(
        functools.partial(_ffn_kernel, final=final),
        grid=(bx, t // tm),
        in_specs=[tile, _mod_spec(mods, 3), _mod_spec(mods, 4), _mod_spec(mods, 5), _full_spec(norm_g),
                  _full_spec(wg), _full_spec(wu), _full_spec(wd), _full_spec(final_g)],
        out_specs=tile,
        out_shape=jax.ShapeDtypeStruct(x.shape, F32),
        compiler_params=_params("parallel", "parallel"),
        name="ffn",
    )(x, mods[0], mods[0], mods[0], *[_arr(w) for w in (norm_g, wg, wu, wd)], final_g)


_ROPE_SWAP = np.array([8, 9, 10, 11, 12, 13, 14, 15, 0, 1, 2, 3, 4, 5, 6, 7,
                       24, 25, 26, 27, 28, 29, 30, 31, 16, 17, 18, 19, 20, 21, 22, 23])


def _rope_tables(n):
    cos = np.zeros((n, HEAD_PAD))
    sin = np.zeros((n, HEAD_PAD))
    cos[:, :QK_NOPE + QK_ROPE] = 1.0
    if n % GRID_W == 0 and n > 0:
        pos = np.arange(n)
        seg = QK_ROPE // 2
        inv = 1.0 / (ROPE_BASE ** (np.arange(seg // 2, dtype=np.float64) * 2.0 / seg))
        for s, p in enumerate((pos // GRID_W, pos % GRID_W)):
            ang = p[:, None] * inv[None, :]
            base = QK_NOPE + s * seg
            cos[:, base:base + seg] = np.concatenate([np.cos(ang), np.cos(ang)], axis=-1)
            sin[:, base:base + seg] = np.concatenate([-np.sin(ang), np.sin(ang)], axis=-1)
    return jnp.asarray(cos, F32), jnp.asarray(sin, F32)


def _identity_rope_tables():
    cos = np.zeros((DENSE_TILE, HEAD_PAD))
    cos[:, :QK_NOPE + QK_ROPE] = 1.0
    return jnp.asarray(cos, F32), jnp.zeros((DENSE_TILE, HEAD_PAD), F32)


def _block_diag(w):
    l, g, i, j = w.shape
    return jnp.einsum("lgij,gh->lgihj", w, jnp.eye(g, dtype=w.dtype)).reshape(l, g * i, g * j)


def _prep_weights(p):
    depth = p["w_in"].shape[0]
    w_in = p["w_in"]
    kpe = w_in[..., Q_LORA + KV_LORA:MLA_IN]
    z64 = jnp.zeros((depth, D_MODEL, QK_NOPE), F32)
    z32 = jnp.zeros((depth, D_MODEL, HEAD_PAD - QK_NOPE - QK_ROPE), F32)
    row = lambda a: a[:, None, :]
    out = {
        "w_mla": jnp.concatenate([w_in[..., :Q_LORA + KV_LORA], z64, kpe, z32, z64, kpe[..., _ROPE_SWAP], z32],
                                 axis=-1).astype(BF16),
        "w_lx": w_in[..., IN_SPLITS[0]:IN_SPLITS[1]].astype(BF16),
        "w_lg": w_in[..., IN_SPLITS[1]:IN_SPLITS[2]].astype(BF16),
        "w_hy": w_in[..., IN_SPLITS[2]:IN_SPLITS[3]].astype(BF16),
        "w_gate": w_in[..., IN_SPLITS[3]:].astype(BF16),
    }
    wq = p["w_uq"].reshape(depth, Q_LORA, MLA_HEADS, QK_NOPE + QK_ROPE)
    pad = jnp.zeros((depth, Q_LORA, MLA_HEADS, HEAD_PAD - QK_NOPE - QK_ROPE), F32)
    out["w_q"] = jnp.concatenate([wq, pad], axis=-1).reshape(depth, Q_LORA, -1).astype(BF16)
    out["w_qs"] = jnp.concatenate([jnp.zeros((depth, Q_LORA, MLA_HEADS, QK_NOPE), F32),
                                   wq[..., QK_NOPE:][..., _ROPE_SWAP], pad], axis=-1
                                  ).reshape(depth, Q_LORA, -1).astype(BF16)
    wkv = p["w_ukv"].reshape(depth, KV_LORA, MLA_HEADS, QK_NOPE + V_HEAD)
    out["w_k"] = jnp.concatenate([wkv[..., :QK_NOPE],
                                  jnp.zeros((depth, KV_LORA, MLA_HEADS, HEAD_PAD - QK_NOPE), F32)],
                                 axis=-1).reshape(depth, KV_LORA, -1).astype(BF16)
    out["w_v"] = wkv[..., QK_NOPE:].reshape(depth, KV_LORA, -1).astype(BF16)
    for d, name in enumerate(("f", "b")):
        out["lru_w" + name] = jnp.concatenate([_block_diag(p["lru_wa"][:, d]), _block_diag(p["lru_wx"][:, d])],
                                              axis=-1).astype(BF16)
        out["lru_bias" + name] = row(jnp.concatenate([p["lru_ba"][:, d], p["lru_bx"][:, d]], axis=-1))
    out["hy_w1"] = jnp.concatenate([p["hy_w1"], jnp.zeros((depth, HY_EMB_PAD - HY_EMB, HY_HID), F32)], axis=1)
    for name in ("w_br_a", "w_br_b", "w_br_c", "w_out", "ffn_w_gate", "ffn_w_up", "ffn_w_down"):
        out[name] = p[name].astype(BF16)
    for name in ("norm1_g", "norm2_g", "q_norm_g", "kv_norm_g", "lru_conv_b", "hy_b1", "hy_freq", "hy_conv_b"):
        out[name] = row(p[name])
    for name in ("lru_conv_w", "lru_lam", "hy_w2", "hy_b2", "hy_w_out", "hy_conv_w"):
        out[name] = p[name]
    out["hy_skip"] = p["hy_skip"][:, :, None, :]
    out["hy_conv_w_flat"] = p["hy_conv_w"].reshape(-1)
    out["hy_conv_b_flat"] = p["hy_conv_b"].reshape(-1)
    out["hy_skip_flat"] = p["hy_skip"].reshape(-1)
    return out


def _mixers(q, k, v, lx, hy, w, layer, kv_ctx=None, h0=None, branch_out=True):
    bx, n, _ = lx.shape
    at = lambda name: (w[name], layer)
    if h0 is None:
        h0 = jnp.zeros((bx, 2, LRU_WIDTH), F32)
    hf, hb, hlast = _lru(lx, at("lru_conv_w"), at("lru_conv_b"), at("lru_wf"), at("lru_biasf"),
                         at("lru_wb"), at("lru_biasb"), at("lru_lam"), h0)
    if not branch_out:
        return None, hlast
    if kv_ctx is None:
        ya = _flash(q, k, v)
    else:
        ya = _flash(q, kv_ctx[0], kv_ctx[1], k, v)
    z, decay = _hy_tables(n)
    filt = _hy_filters(z, at("hy_w1"), at("hy_b1"), at("hy_w2"), at("hy_b2"), at("hy_freq"), at("hy_w_out"), decay)
    n1h = n // HY_N2
    if n1h % 8 == 0:
        consts = _cm_consts(n)
        spectra = _hycm_spectra(filt.T.reshape(-1, n1h, HY_N2), at("hy_skip_flat"), consts)
        hy_cm = hy.reshape(bx, n1h, HY_N2, 3 * HY_WIDTH).transpose(0, 3, 1, 2)
        cw, cb = at("hy_conv_w_flat"), at("hy_conv_b_flat")
        y1 = _hycm_conv(hy_cm, 0, hy_cm, HY_WIDTH, cw, cb, spectra, 0, consts, True)
        yc = _hycm_conv(y1, 0, hy_cm, 2 * HY_WIDTH, cw, cb, spectra, 1, consts, False)
        yc = yc.transpose(0, 2, 3, 1).reshape(bx, n, HY_WIDTH)
    else:
        consts = _hy_consts(n)
        spectra = _hy_spectra(filt, at("hy_skip"), consts)
        cbn = HY_WIDTH // HY_CB
        cw, cb = at("hy_conv_w"), at("hy_conv_b")
        y1 = _hy_conv(hy, 0, hy, cbn, cw, cb, spectra, 0, consts, True)
        yc = _hy_conv(y1, 0, hy, 2 * cbn, cw, cb, spectra, 1, consts, False)
    return (ya, hf, hb, yc), hlast


def _layer(x, xc, mods, w, layer, final_g, last):
    bx, s, _ = x.shape
    sc = xc.shape[1]
    at = lambda name: (w[name], layer)
    mods_l = (mods, layer, None)
    mods_c = (mods, layer, bx)
    flat = lambda a: a.reshape(1, bx * sc, a.shape[-1])
    unflat = lambda a: a.reshape(bx, sc, a.shape[-1])

    proj_w = tuple(at(name) for name in ("norm1_g", "w_mla", "w_lx", "w_lg", "w_hy", "q_norm_g", "kv_norm_g",
                                         "w_q", "w_qs", "w_k", "w_v"))
    qc, kc, vc, lxc, lgc, hyc = [unflat(a) for a in _in_proj(flat(xc), mods_c, proj_w, *_identity_rope_tables())]
    ql, kl, vl, lxl, lgl, hyl = _in_proj(x, mods_l, proj_w, *_rope_tables(s))

    br_c, h_c = _mixers(qc, kc, vc, lxc, hyc, w, layer, branch_out=not last)
    br_l, _ = _mixers(ql, kl, vl, lxl, hyl, w, layer, kv_ctx=(kc, vc), h0=h_c)

    merge_w = tuple(at(name) for name in ("w_gate", "w_br_a", "w_br_b", "w_br_c", "w_out"))
    ffn_w = tuple(at(name) for name in ("ffn_w_gate", "ffn_w_up", "ffn_w_down"))
    x = _merge(x, mods_l, at("norm1_g"), br_l[0], br_l[1], br_l[2], lgl, br_l[3], *merge_w)
    x = _ffn(x, mods_l, at("norm2_g"), *ffn_w, final_g, last)
    if not last:
        xcf = _merge(flat(xc), mods_c, at("norm1_g"), flat(br_c[0]), flat(br_c[1]), flat(br_c[2]), flat(lgc),
                     flat(br_c[3]), *merge_w)
        xc = unflat(_ffn(xcf, mods_c, at("norm2_g"), *ffn_w, final_g, False))
    return x, xc


def kernel(x, c, ctx, c_ctx, ada_w, ada_b, norm1_g, norm2_g, w_in, q_norm_g, w_uq, kv_norm_g, w_ukv,
           lru_conv_w, lru_conv_b, lru_wa, lru_ba, lru_wx, lru_bx, lru_lam,
           hy_conv_w, hy_conv_b, hy_w1, hy_b1, hy_w2, hy_b2, hy_freq, hy_w_out, hy_skip,
           w_br_a, w_br_b, w_br_c, w_out, ffn_w_gate, ffn_w_up, ffn_w_down, final_norm_g):
    depth = ada_w.shape[0]
    bx = x.shape[0]
    assert bx % 2 == 0 and bx + 1 <= 8
    cond = jnp.concatenate([c, c_ctx[None], jnp.zeros((8 - bx - 1, D_MODEL), F32)], axis=0)
    mods = _ada_mods(cond, ada_w, ada_b)[:, :, None, :]
    w = _prep_weights(dict(
        norm1_g=norm1_g, norm2_g=norm2_g, w_in=w_in, q_norm_g=q_norm_g, w_uq=w_uq,
        kv_norm_g=kv_norm_g, w_ukv=w_ukv, lru_conv_w=lru_conv_w, lru_conv_b=lru_conv_b,
        lru_wa=lru_wa, lru_ba=lru_ba, lru_wx=lru_wx, lru_bx=lru_bx, lru_lam=lru_lam,
        hy_conv_w=hy_conv_w, hy_conv_b=hy_conv_b, hy_w1=hy_w1, hy_b1=hy_b1, hy_w2=hy_w2, hy_b2=hy_b2,
        hy_freq=hy_freq, hy_w_out=hy_w_out, hy_skip=hy_skip, w_br_a=w_br_a, w_br_b=w_br_b,
        w_br_c=w_br_c, w_out=w_out, ffn_w_gate=ffn_w_gate, ffn_w_up=ffn_w_up, ffn_w_down=ffn_w_down))
    xc = ctx
    fg = final_norm_g[None]
    for layer in range(depth):
        x, xc = _layer(x, xc, mods, w, layer, fg, layer == depth - 1)
    return x
```

```python
import functools
import math

import numpy as np
import jax
import jax.numpy as jnp
from jax import lax
from jax.experimental import pallas as pl
from jax.experimental.pallas import tpu as pltpu

F32 = jnp.float32
BF16 = jnp.bfloat16

D_MODEL = 1024
GRID_W = 64
EPS = 1e-6

MLA_HEADS = 8
Q_LORA = 384
KV_LORA = 256
QK_NOPE = 64
QK_ROPE = 32
V_HEAD = 64
ROPE_BASE = 10000.0
SM_SCALE = (QK_NOPE + QK_ROPE) ** -0.5
HEAD_PAD = 128
MLA_Z = Q_LORA + KV_LORA + 2 * HEAD_PAD

LRU_WIDTH = 512
LRU_BLOCKS = 8
LRU_CONV = 4
LRU_C = 8.0

HY_WIDTH = 512
HY_ORDER = 2
HY_SHORT = 3
HY_EMB = 33
HY_EMB_PAD = 48
HY_HID = 64
HY_INNER = 2
HY_FAST_DECAY = 0.3
HY_SLOW_DECAY = 1.5
HY_DECAY_TARGET = 1e-2
HY_N2 = 128
HY_CB = 128
HY_GROUP = 4
HY_CH_GROUP = 16

FFN_HID = 2816
N_BRANCH = 3
MLA_IN = Q_LORA + KV_LORA + QK_ROPE
IN_SPLITS = (MLA_IN, MLA_IN + LRU_WIDTH, MLA_IN + 2 * LRU_WIDTH, MLA_IN + 2 * LRU_WIDTH + 3 * HY_WIDTH)

Q_PRESCALE = SM_SCALE * math.log2(math.e)
FLASH_Q_TILE = 1024
FLASH_KEY_CHUNK = 1024
TOKEN_TILE = 256
DENSE_TILE = 512
HALO = 8
VMEM_LIMIT = 56 * 1024 * 1024


def _params(*sem):
    return pltpu.CompilerParams(dimension_semantics=sem, vmem_limit_bytes=VMEM_LIMIT)


def _dot(a, b):
    return jnp.dot(a, b, preferred_element_type=F32)


def _rms(x, g):
    return x * lax.rsqrt(jnp.mean(x * x, axis=-1, keepdims=True) + EPS) * g


def _sigmoid(x):
    return 0.5 * jnp.tanh(0.5 * x) + 0.5


def _norm_mod(x, g, shift, scale):
    return _rms(x, g) * (1.0 + scale) + shift


def _arr(op):
    return op[0] if isinstance(op, tuple) else op


def _full_spec(op):
    once = pl.Buffered(1)
    if isinstance(op, tuple):
        arr, layer = op
        nd = arr.ndim
        return pl.BlockSpec((None,) + arr.shape[1:], lambda *_: (layer,) + (0,) * (nd - 1), pipeline_mode=once)
    nd = op.ndim
    return pl.BlockSpec(op.shape, lambda *_: (0,) * nd, pipeline_mode=once)


def _mod_spec(mods, j):
    arr, layer, row = mods
    if row is None:
        return pl.BlockSpec((None, 1, 1, D_MODEL), lambda b, i: (layer, b, 0, j))
    return pl.BlockSpec((None, 1, 1, D_MODEL), lambda b, i: (layer, row, 0, j))


def _ada_kernel(c_ref, w_ref, b_ref, o_ref):
    c = c_ref[...]
    s = (c * jax.nn.sigmoid(c)).astype(BF16)
    o_ref[0] = _dot(s, w_ref[0].astype(BF16)) + b_ref[0]


def _ada_mods(cond, ada_w, ada_b):
    depth, _, width = ada_w.shape
    tn = 1536
    return pl.pallas_call(
        _ada_kernel,
        grid=(depth, width // tn),
        in_specs=[pl.BlockSpec((8, D_MODEL), lambda l, j: (0, 0)),
                  pl.BlockSpec((1, D_MODEL, tn), lambda l, j: (l, 0, j)),
                  pl.BlockSpec((1, 1, tn), lambda l, j: (l, 0, j))],
        out_specs=pl.BlockSpec((1, 8, tn), lambda l, j: (l, 0, j)),
        out_shape=jax.ShapeDtypeStruct((depth, 8, width), F32),
        compiler_params=_params("arbitrary", "arbitrary"),
        name="ada_mods",
    )(cond, ada_w, ada_b.reshape(depth, 1, width))


def _inproj_kernel(x_ref, sh_ref, sc_ref, g_ref, wm_ref, wx_ref, wg_ref, wh_ref,
                   qg_ref, kvg_ref, wq_ref, wqs_ref, wk_ref, wv_ref, cos_ref, sin_ref,
                   q_ref, k_ref, v_ref, lx_ref, lg_ref, hy_ref):
    h = _norm_mod(x_ref[0], g_ref[...], sh_ref[0], sc_ref[0]).astype(BF16)
    lx_ref[0] = _dot(h, wx_ref[...])
    lg_ref[0] = _dot(h, wg_ref[...])
    hy_ref[0] = _dot(h, wh_ref[...]).astype(hy_ref.dtype)
    z = _dot(h, wm_ref[...])
    nq = _rms(z[:, :Q_LORA], qg_ref[...]).astype(BF16)
    nkv = _rms(z[:, Q_LORA:Q_LORA + KV_LORA], kvg_ref[...]).astype(BF16)
    pe = z[:, Q_LORA + KV_LORA:Q_LORA + KV_LORA + HEAD_PAD]
    pes = z[:, Q_LORA + KV_LORA + HEAD_PAD:]
    cos = cos_ref[...]
    sin = sin_ref[...]
    q = _dot(nq, wq_ref[...])
    qs = _dot(nq, wqs_ref[...])
    kn = _dot(nkv, wk_ref[...])
    k_pe = pe * cos + pes * sin
    for hd in range(MLA_HEADS):
        sl = slice(hd * HEAD_PAD, (hd + 1) * HEAD_PAD)
        q_ref[0, :, sl] = ((q[:, sl] * cos + qs[:, sl] * sin) * Q_PRESCALE).astype(BF16)
        k_ref[0, :, sl] = (kn[:, sl] + k_pe).astype(BF16)
    v_ref[0] = _dot(nkv, wv_ref[...]).astype(BF16)


def _in_proj(x, mods, weights, cos, sin):
    bx, t, _ = x.shape
    tm = DENSE_TILE
    tile = lambda w: pl.BlockSpec((1, tm, w), lambda b, i: (b, i, 0))
    if cos.shape[0] == tm:
        tab = pl.BlockSpec((tm, HEAD_PAD), lambda b, i: (0, 0))
    else:
        tab = pl.BlockSpec((tm, HEAD_PAD), lambda b, i: (i, 0))
    outs = ((MLA_HEADS * HEAD_PAD, BF16), (MLA_HEADS * HEAD_PAD, BF16), (MLA_HEADS * V_HEAD, BF16),
            (LRU_WIDTH, F32), (LRU_WIDTH, F32), (3 * HY_WIDTH, BF16))
    return pl.pallas_call(
        _inproj_kernel,
        grid=(bx, t // tm),
        in_specs=[tile(D_MODEL), _mod_spec(mods, 0), _mod_spec(mods, 1)]
                 + [_full_spec(w) for w in weights] + [tab, tab],
        out_specs=[tile(w) for w, _ in outs],
        out_shape=[jax.ShapeDtypeStruct((bx, t, w), dt) for w, dt in outs],
        compiler_params=_params("parallel", "parallel"),
        name="in_proj",
    )(x, mods[0], mods[0], *[_arr(w) for w in weights], cos, sin)


def _qk(q, k):
    return lax.dot_general(q, k, (((1,), (1,)), ((), ())), preferred_element_type=F32)


def _flash_kernel(*refs, chunks):
    q_ref, o_ref = refs[0], refs[-1]
    kv = refs[1:-1]
    state = [None, None]
    for src, off, size in chunks:
        k_ref, v_ref = kv[2 * src], kv[2 * src + 1]
        v = v_ref[0, off:off + size, :]
        for h in range(2):
            sl = slice(h * HEAD_PAD, (h + 1) * HEAD_PAD)
            s = _qk(q_ref[0, :, sl], k_ref[0, off:off + size, sl])
            m_blk = jnp.max(s, axis=-1, keepdims=True)
            if state[h] is None:
                m = m_blk
                p = jnp.exp2(s - m)
                l = jnp.sum(p, axis=-1, keepdims=True)
                acc = _dot(p.astype(BF16), v)
            else:
                m_old, l, acc = state[h]
                m = jnp.maximum(m_old, m_blk)
                alpha = jnp.exp2(m_old - m)
                p = jnp.exp2(s - m)
                l = alpha * l + jnp.sum(p, axis=-1, keepdims=True)
                acc = alpha * acc + _dot(p.astype(BF16), v)
            state[h] = (m, l, acc)
    outs = [acc / l for _, l, acc in state]
    lane = lax.broadcasted_iota(jnp.int32, outs[0].shape, 1)
    o_ref[0] = jnp.where(lane < V_HEAD, outs[0], outs[1]).astype(o_ref.dtype)


def _flash(q, kc, vc, kl=None, vl=None):
    bx, t, _ = q.shape
    tq = min(FLASH_Q_TILE, t)
    sc = kc.shape[1]
    pairs = MLA_HEADS // 2
    qspec = pl.BlockSpec((1, tq, 2 * HEAD_PAD), lambda b, hp, i: (b, i, hp))
    kspec = lambda n: pl.BlockSpec((1, n, 2 * HEAD_PAD), lambda b, hp, i: (b, 0, hp))
    vspec = lambda n: pl.BlockSpec((1, n, 2 * V_HEAD), lambda b, hp, i: (b, 0, hp))
    in_specs = [qspec, kspec(sc), vspec(sc)]
    args = [q, kc, vc]
    chunks = [(0, 0, sc)]
    if kl is not None:
        sl = kl.shape[1]
        size = min(FLASH_KEY_CHUNK, sl)
        chunks += [(1, off, size) for off in range(0, sl, size)]
        in_specs += [kspec(sl), vspec(sl)]
        args += [kl, vl]
    return pl.pallas_call(
        functools.partial(_flash_kernel, chunks=tuple(chunks)),
        grid=(bx, pairs, t // tq),
        in_specs=in_specs,
        out_specs=pl.BlockSpec((1, tq, 2 * V_HEAD), lambda b, hp, i: (b, i, hp)),
        out_shape=jax.ShapeDtypeStruct((bx, t, MLA_HEADS * V_HEAD), BF16),
        compiler_params=_params("parallel", "parallel", "arbitrary"),
        name="flash",
    )(*args)


def _lru_kernel(xf_ref, xfp_ref, xfn_ref, xb_ref, xbp_ref, xbn_ref, cw_ref, cb_ref,
                wf_ref, bf_ref, wb_ref, bb_ref, lam_ref, h0_ref,
                hf_ref, hb_ref, hl_ref, af_s, bfw_s, ab_s, bbw_s, pad_s, carry_s):
    t = pl.program_id(0)
    nt = pl.num_programs(0)
    bx, tm, _ = xf_ref.shape

    @pl.when(t == 0)
    def _():
        carry_s[...] = h0_ref[...]

    def coeffs(b, x_ref, prev_ref, next_ref, tile, w_ref, bias_ref, lam, a_s, b_s):
        x = x_ref[b]
        pad_s[0:HALO, :] = jnp.where(tile > 0, prev_ref[b], 0.0)
        pad_s[HALO:HALO + tm, :] = x
        pad_s[HALO + tm:, :] = jnp.where(tile < nt - 1, next_ref[b], 0.0)
        xp = pad_s[...]
        rows = tm + 2 * HALO
        tap = lambda k: pltpu.roll(xp, k % rows, axis=0)[HALO:HALO + tm]
        u = (cw_ref[0:1, :] * tap(2) + cw_ref[1:2, :] * tap(1) + cw_ref[2:3, :] * x
             + cw_ref[3:4, :] * tap(-1) + cb_ref[...])
        y = _dot(u.astype(BF16), w_ref[...]) + bias_ref[...]
        t_r = jnp.tanh(y[:, :LRU_WIDTH])
        t_i = jnp.tanh(y[:, LRU_WIDTH:])
        neg = -lam
        softplus = jnp.maximum(neg, 0.0) + jnp.log1p(jnp.exp(-jnp.abs(neg)))
        c = (-0.5 * LRU_C * math.log2(math.e)) * softplus
        a = jnp.exp2(c * t_r + c)
        a_s[b] = a
        var = 1.0 - a * a
        half = 0.5 * (jnp.where(var > 0.0, var * lax.rsqrt(var), 0.0) * u)
        b_s[b] = half * t_i + half

    for b in range(bx):
        coeffs(b, xf_ref, xfp_ref, xfn_ref, t, wf_ref, bf_ref, lam_ref[0:1, :], af_s, bfw_s)
        coeffs(b, xb_ref, xbp_ref, xbn_ref, nt - 1 - t, wb_ref, bb_ref, lam_ref[1:2, :], ab_s, bbw_s)

    def body(j, hs):
        jb = tm - 1 - j
        out = []
        for b in range(bx):
            hf = af_s[b, pl.ds(j, 1), :] * hs[2 * b] + bfw_s[b, pl.ds(j, 1), :]
            hf_ref[b, pl.ds(j, 1), :] = hf
            hb = ab_s[b, pl.ds(jb, 1), :] * hs[2 * b + 1] + bbw_s[b, pl.ds(jb, 1), :]
            hb_ref[b, pl.ds(jb, 1), :] = hb
            out += [hf, hb]
        return tuple(out)

    init = tuple(carry_s[b, d:d + 1, :] for b in range(bx) for d in range(2))
    hs = lax.fori_loop(0, tm, body, init, unroll=8)
    for b in range(bx):
        for d in range(2):
            carry_s[b, d:d + 1, :] = hs[2 * b + d]
    hl_ref[...] = carry_s[...]


def _lru(lx, conv_w, conv_b, wf, biasf, wb, biasb, lam, h0):
    bx, t, w = lx.shape
    tm = TOKEN_TILE
    nt = t // tm
    hb_per = tm // HALO
    last_halo = t // HALO - 1
    fwd = lambda i: (0, i, 0)
    bwd = lambda i: (0, nt - 1 - i, 0)
    prev_of = lambda f: (lambda i: (0, jnp.maximum(f(i)[1] * hb_per - 1, 0), 0))
    next_of = lambda f: (lambda i: (0, jnp.minimum((f(i)[1] + 1) * hb_per, last_halo), 0))
    tile = lambda f: pl.BlockSpec((bx, tm, w), f)
    halo = lambda f: pl.BlockSpec((bx, HALO, w), f)
    state = pl.BlockSpec((bx, 2, w), lambda i: (0, 0, 0))
    return pl.pallas_call(
        _lru_kernel,
        grid=(nt,),
        in_specs=[tile(fwd), halo(prev_of(fwd)), halo(next_of(fwd)),
                  tile(bwd), halo(prev_of(bwd)), halo(next_of(bwd)),
                  _full_spec(conv_w), _full_spec(conv_b), _full_spec(wf), _full_spec(biasf),
                  _full_spec(wb), _full_spec(biasb), _full_spec(lam), state],
        out_specs=[tile(fwd), tile(bwd), state],
        out_shape=[jax.ShapeDtypeStruct((bx, t, w), F32), jax.ShapeDtypeStruct((bx, t, w), F32),
                   jax.ShapeDtypeStruct((bx, 2, w), F32)],
        scratch_shapes=[pltpu.VMEM((bx, tm, w), F32)] * 4
                       + [pltpu.VMEM((tm + 2 * HALO, w), F32), pltpu.VMEM((bx, 2, w), F32)],
        compiler_params=_params("arbitrary"),
        name="rglru",
    )(lx, lx, lx, lx, lx, lx, *[_arr(w) for w in (conv_w, conv_b, wf, biasf, wb, biasb, lam)], h0)


def _split_bf16(x):
    hi = x.astype(BF16)
    return hi, (x - hi.astype(F32)).astype(BF16)


def _dot3(a, b):
    ah, al = _split_bf16(a)
    bh, bl = _split_bf16(b)
    return _dot(ah, bh) + (_dot(ah, bl) + _dot(al, bh))


def _hyfilt_kernel(z_ref, w1_ref, b1_ref, w2_ref, b2_ref, fr_ref, wo_ref, dec_ref, o_ref):
    fr = fr_ref[...]
    h = jnp.sin(fr * (_dot3(z_ref[...], w1_ref[...]) + b1_ref[...]))
    for j in range(HY_INNER):
        h = jnp.sin(fr * (_dot3(h, w2_ref[j]) + b2_ref[j:j + 1, :]))
    dec = dec_ref[...]
    for g in range(2 * HY_ORDER):
        sl = slice(g * HY_WIDTH, (g + 1) * HY_WIDTH)
        o_ref[:, sl] = _dot3(h, wo_ref[:, sl]) * dec


def _hyfilt_cm_kernel(z_ref, w1_ref, b1_ref, w2_ref, b2_ref, fr_ref, wot_ref, dect_ref, o_ref):
    tn = z_ref.shape[0]
    fr = fr_ref[...]
    h = jnp.sin(fr * (_dot3(z_ref[...], w1_ref[...]) + b1_ref[...]))
    for j in range(HY_INNER):
        h = jnp.sin(fr * (_dot3(h, w2_ref[j]) + b2_ref[j:j + 1, :]))
    hh, hl = _split_bf16(h)
    nt = lambda a, b: lax.dot_general(a, b, (((1,), (1,)), ((), ())), preferred_element_type=F32)
    dect = dect_ref[...]
    for g in range(2 * HY_ORDER):
        wh, wl = _split_bf16(wot_ref[g * HY_WIDTH:(g + 1) * HY_WIDTH, :])
        res = (nt(wh, hh) + (nt(wh, hl) + nt(wl, hh))) * dect
        o_ref[g * HY_WIDTH:(g + 1) * HY_WIDTH] = res.reshape(HY_WIDTH, tn // HY_N2, HY_N2)


def _hy_filters_cm(z, w1, b1, w2, b2, freq, w_out_t, decay_t):
    n = z.shape[0]
    tn = 8 * HY_N2
    width = 2 * HY_ORDER * HY_WIDTH
    return pl.pallas_call(
        _hyfilt_cm_kernel,
        grid=(n // tn,),
        in_specs=[pl.BlockSpec((tn, HY_EMB_PAD), lambda i: (i, 0)), _full_spec(w1), _full_spec(b1),
                  _full_spec(w2), _full_spec(b2), _full_spec(freq), _full_spec(w_out_t),
                  pl.BlockSpec((HY_WIDTH, tn), lambda i: (0, i))],
        out_specs=pl.BlockSpec((width, tn // HY_N2, HY_N2), lambda i: (0, i, 0)),
        out_shape=jax.ShapeDtypeStruct((width, n // HY_N2, HY_N2), F32),
        compiler_params=_params("parallel"),
        name="hyena_filters_cm",
    )(z, *[_arr(w) for w in (w1, b1, w2, b2, freq, w_out_t)], decay_t)


def _hy_filters(z, w1, b1, w2, b2, freq, w_out, decay):
    n = z.shape[0]
    tn = min(n, 512)
    width = 2 * HY_ORDER * HY_WIDTH
    return pl.pallas_call(
        _hyfilt_kernel,
        grid=(n // tn,),
        in_specs=[pl.BlockSpec((tn, HY_EMB_PAD), lambda i: (i, 0)), _full_spec(w1), _full_spec(b1),
                  _full_spec(w2), _full_spec(b2), _full_spec(freq), _full_spec(w_out),
                  pl.BlockSpec((tn, HY_WIDTH), lambda i: (i, 0))],
        out_specs=pl.BlockSpec((tn, width), lambda i: (i, 0)),
        out_shape=jax.ShapeDtypeStruct((n, width), F32),
        compiler_params=_params("parallel"),
        name="hyena_filters",
    )(z, *[_arr(w) for w in (w1, b1, w2, b2, freq, w_out)], decay)


def _bitrev(k, bits):
    r = jnp.zeros_like(k)
    for b in range(bits):
        r = r | (((k >> b) & 1) << (bits - 1 - b))
    return r


def _fft_lead_fwd(sre, sim, wr_ref, wi_ref, n1):
    half = n1 // 2
    m = half
    first = True
    while m >= 1:
        shift = int(math.log2(m))
        stride = half // m

        def body(q, c, m=m, shift=shift, stride=stride, first=first):
            grp = q >> shift
            j = q - (grp << shift)
            i0 = (grp << (shift + 1)) + j
            i1 = i0 + m
            wr = wr_ref[j * stride]
            wi = wi_ref[j * stride]
            ar, ai = sre[i0], sim[i0]
            if first:
                dr, di = ar, ai
            else:
                br, bi = sre[i1], sim[i1]
                sre[i0] = ar + br
                sim[i0] = ai + bi
                dr, di = ar - br, ai - bi
            sre[i1] = dr * wr - di * wi
            sim[i1] = dr * wi + di * wr
            return c

        lax.fori_loop(0, half, body, 0)
        first = False
        m //= 2


def _fft_lead_inv(sre, sim, wr_ref, wi_ref, n1):
    half = n1 // 2
    m = 1
    while m <= half:
        shift = int(math.log2(m))
        stride = half // m
        last = m == half

        def body(q, c, m=m, shift=shift, stride=stride, last=last):
            grp = q >> shift
            j = q - (grp << shift)
            i0 = (grp << (shift + 1)) + j
            i1 = i0 + m
            wr = wr_ref[j * stride]
            wi = wi_ref[j * stride]
            ar, ai = sre[i0], sim[i0]
            br, bi = sre[i1], sim[i1]
            tr = br * wr + bi * wi
            ti = bi * wr - br * wi
            sre[i0] = ar + tr
            sim[i0] = ai + ti
            if not last:
                sre[i1] = ar - tr
                sim[i1] = ai - ti
            return c

        lax.fori_loop(0, half, body, 0)
        m *= 2


def _spectrum_loop(sre, sim, f2_s, tw, n1, emit):
    tw0r_ref, tw0i_ref, wgr_ref, wgi_ref = tw
    bits = int(math.log2(n1))
    group = min(HY_GROUP, n1)

    def body(kb, tws):
        ks, xs = [], []
        for g in range(group):
            k1 = kb * group + g
            blk = _bitrev(k1, bits)
            tr, ti = tws[2 * g], tws[2 * g + 1]
            ar, ai = sre[blk], sim[blk]
            xs.append(jnp.concatenate([ar * tr - ai * ti, ar * ti + ai * tr], axis=0).astype(BF16))
            ks.append((k1, blk))
        b2 = _dot(f2_s[...], jnp.concatenate(xs, axis=1))
        emit(ks, b2, tws)
        wgr, wgi = wgr_ref[...], wgi_ref[...]
        nxt = []
        for g in range(group):
            tr, ti = tws[2 * g], tws[2 * g + 1]
            nxt += [tr * wgr - ti * wgi, tr * wgi + ti * wgr]
        return tuple(nxt)

    init = tuple(r[g] for g in range(group) for r in (tw0r_ref, tw0i_ref))
    lax.fori_loop(0, n1 // group, body, init)


def _group_cols(b2, g, cb):
    return b2[:HY_N2, g * cb:(g + 1) * cb], b2[HY_N2:, g * cb:(g + 1) * cb]


def _hyspec_kernel(wr_ref, wi_ref, hf_ref, hb_ref, skip_ref, f2_ref, tw0r_ref, tw0i_ref, wgr_ref, wgi_ref,
                   k_ref, sre, sim, f2_s):
    n = hf_ref.shape[0]
    n1 = 2 * n // HY_N2
    half = n1 // 2
    cb = hf_ref.shape[1]
    f2_s[...] = f2_ref[...].astype(BF16)
    for direction, h_ref in enumerate((hf_ref, hb_ref)):
        sre[0:half] = h_ref[...].reshape(half, HY_N2, cb)
        sim[0:half] = jnp.zeros((half, HY_N2, cb), F32)
        _fft_lead_fwd(sre, sim, wr_ref, wi_ref, n1)

        def emit(ks, b2, tws, direction=direction):
            for g, (k1, _) in enumerate(ks):
                br, bi = _group_cols(b2, g, cb)
                if direction == 0:
                    k_ref[0, k1, 0:HY_N2, :] = br + skip_ref[0]
                    k_ref[0, k1, HY_N2:, :] = bi
                else:
                    k_ref[0, k1, 0:HY_N2, :] = k_ref[0, k1, 0:HY_N2, :] + br
                    k_ref[0, k1, HY_N2:, :] = k_ref[0, k1, HY_N2:, :] - bi

        _spectrum_loop(sre, sim, f2_s, (tw0r_ref, tw0i_ref, wgr_ref, wgi_ref), n1, emit)


def _hy_spectra(filt, skip, consts):
    n = filt.shape[0]
    n1 = 2 * n // HY_N2
    cbn = HY_WIDTH // HY_CB
    smem = pl.BlockSpec(memory_space=pltpu.SMEM)
    col = lambda direction: (lambda o, c: (0, (direction * HY_ORDER + o) * cbn + c))
    tw_names = ("tw0r", "tw0i", "wgr", "wgi")
    return pl.pallas_call(
        _hyspec_kernel,
        grid=(HY_ORDER, cbn),
        in_specs=[smem, smem,
                  pl.BlockSpec((n, HY_CB), col(0)), pl.BlockSpec((n, HY_CB), col(1)),
                  pl.BlockSpec((None, 1, 1, HY_CB), lambda o, c: (skip[1], o, 0, c)),
                  _full_spec(consts["f2"])] + [_full_spec(consts[k]) for k in tw_names],
        out_specs=pl.BlockSpec((1, n1, 2 * HY_N2, HY_CB), lambda o, c: (o, 0, 0, c)),
        out_shape=jax.ShapeDtypeStruct((HY_ORDER, n1, 2 * HY_N2, HY_WIDTH), F32),
        scratch_shapes=[pltpu.VMEM((n1, HY_N2, HY_CB), F32)] * 2 + [pltpu.VMEM((2 * HY_N2, 2 * HY_N2), BF16)],
        compiler_params=_params("parallel", "parallel"),
        name="hyena_spectra",
    )(consts["wr"], consts["wi"], filt, filt, skip[0], consts["f2"], *[consts[k] for k in tw_names])


def _short_conv3(x, pad_s, w_ref, b_ref):
    n = x.shape[0]
    pad_s[HALO:HALO + n, :] = x
    return (w_ref[0:1, :] * pad_s[HALO - 1:HALO - 1 + n, :] + w_ref[1:2, :] * x
            + w_ref[2:3, :] * pad_s[HALO + 1:HALO + 1 + n, :] + b_ref[...])


def _hyconv_kernel(wr_ref, wi_ref, u_ref, g_ref, ucw_ref, ucb_ref, gcw_ref, gcb_ref, k_ref,
                   f2_ref, f2i_ref, tw0r_ref, tw0i_ref, wgr_ref, wgi_ref, o_ref,
                   sre, sim, pad_s, f2_s, f2i_s, *, conv_u):
    n = u_ref.shape[1]
    cb = u_ref.shape[2]
    n1 = 2 * n // HY_N2
    half = n1 // 2
    f2_s[...] = f2_ref[...].astype(BF16)
    f2i_s[...] = f2i_ref[...].astype(BF16)
    margin = jnp.zeros((HALO, cb), F32)
    pad_s[0:HALO, :] = margin
    pad_s[HALO + n:, :] = margin

    for b, s in enumerate((sre, sim)):
        u = u_ref[b].astype(F32)
        if conv_u:
            u = _short_conv3(u, pad_s, ucw_ref, ucb_ref)
        s[0:half] = u.reshape(half, HY_N2, cb)
    _fft_lead_fwd(sre, sim, wr_ref, wi_ref, n1)

    def emit(ks, b2, tws):
        ps = []
        for g, (k1, _) in enumerate(ks):
            br, bi = _group_cols(b2, g, cb)
            kr = k_ref[0, k1, 0:HY_N2, :]
            ki = k_ref[0, k1, HY_N2:, :]
            ps.append(jnp.concatenate([br * kr - bi * ki, br * ki + bi * kr], axis=0).astype(BF16))
        c2 = _dot(f2i_s[...], jnp.concatenate(ps, axis=1))
        for g, (_, blk) in enumerate(ks):
            cr, ci = _group_cols(c2, g, cb)
            tr, ti = tws[2 * g], tws[2 * g + 1]
            sre[blk] = cr * tr + ci * ti
            sim[blk] = ci * tr - cr * ti

    _spectrum_loop(sre, sim, f2_s, (tw0r_ref, tw0i_ref, wgr_ref, wgi_ref), n1, emit)
    _fft_lead_inv(sre, sim, wr_ref, wi_ref, n1)

    for b, s in enumerate((sre, sim)):
        gate = _short_conv3(g_ref[b].astype(F32), pad_s, gcw_ref, gcb_ref)
        o_ref[b] = (gate * s[0:half].reshape(n, cb)).astype(o_ref.dtype)


def _hy_conv(u, u_col, g, g_col, conv_w, conv_b, spectra, order, consts, conv_u):
    bx, n, _ = u.shape
    n1 = 2 * n // HY_N2
    cbn = HY_WIDTH // HY_CB
    smem = pl.BlockSpec(memory_space=pltpu.SMEM)
    data = lambda col: pl.BlockSpec((2, n, HY_CB), lambda c, p, col=col: (p, 0, col + c))
    layer = conv_w[1]
    wrow = lambda rows, col: pl.BlockSpec((None, rows, HY_CB), lambda c, p, col=col: (layer, 0, col + c))
    ucol = u_col if conv_u else g_col
    const_names = ("f2", "f2i", "tw0r", "tw0i", "wgr", "wgi")
    dft = pltpu.VMEM((2 * HY_N2, 2 * HY_N2), BF16)
    return pl.pallas_call(
        functools.partial(_hyconv_kernel, conv_u=conv_u),
        grid=(cbn, bx // 2),
        in_specs=[smem, smem, data(u_col), data(g_col),
                  wrow(HY_SHORT, ucol), wrow(1, ucol), wrow(HY_SHORT, g_col), wrow(1, g_col),
                  pl.BlockSpec((1, n1, 2 * HY_N2, HY_CB), lambda c, p: (order, 0, 0, c))]
                 + [_full_spec(consts[k]) for k in const_names],
        out_specs=pl.BlockSpec((2, n, HY_CB), lambda c, p: (p, 0, c)),
        out_shape=jax.ShapeDtypeStruct((bx, n, HY_WIDTH), BF16),
        scratch_shapes=[pltpu.VMEM((n1, HY_N2, HY_CB), F32)] * 2
                       + [pltpu.VMEM((n + 2 * HALO, HY_CB), F32), dft, dft],
        compiler_params=_params("parallel", "arbitrary"),
        name="hyena_conv",
    )(consts["wr"], consts["wi"], u, g, conv_w[0], conv_b[0], conv_w[0], conv_b[0], spectra,
      *[consts[k] for k in const_names])


def _hy_consts(n):
    big_n = 2 * n
    n1 = big_n // HY_N2
    group = min(HY_GROUP, n1)
    q = np.arange(max(n1 // 2, 1), dtype=np.float64)
    ang1 = 2.0 * np.pi * q / n1
    idx = np.arange(HY_N2, dtype=np.float64)
    ang2 = 2.0 * np.pi * np.outer(idx, idx) / HY_N2
    c, s = np.cos(ang2), np.sin(ang2)
    f2 = np.block([[c, s], [-s, c]])
    f2i = np.block([[c, -s], [s, c]]) / big_n
    lane = np.ones((1, 1, HY_CB))
    ang0 = 2.0 * np.pi * np.arange(group)[:, None, None] * idx[None, :, None] / big_n
    angg = 2.0 * np.pi * group * idx[:, None] / big_n
    return {
        "wr": jnp.asarray(np.cos(ang1), F32), "wi": jnp.asarray(-np.sin(ang1), F32),
        "f2": jnp.asarray(f2, F32), "f2i": jnp.asarray(f2i, F32),
        "tw0r": jnp.asarray(np.cos(ang0) * lane, F32), "tw0i": jnp.asarray(-np.sin(ang0) * lane, F32),
        "wgr": jnp.asarray(np.cos(angg) * lane[0], F32), "wgi": jnp.asarray(-np.sin(angg) * lane[0], F32),
    }


def _hy_tables(n):
    t = np.linspace(0.0, 1.0, n, dtype=np.float32)[:, None].astype(np.float64)
    bands = (HY_EMB - 1) // 2
    w = 2.0 * np.pi * np.arange(n, dtype=np.float64) / n
    f = np.linspace(1e-4, bands - 1, bands, dtype=np.float32).astype(np.float64)
    ang = w[:, None] * f[None, :]
    z = np.concatenate([t, np.cos(ang), -np.sin(ang), np.zeros((n, HY_EMB_PAD - HY_EMB))], axis=-1)
    max_decay = math.log(HY_DECAY_TARGET) / HY_FAST_DECAY
    min_decay = math.log(HY_DECAY_TARGET) / HY_SLOW_DECAY
    deltas = np.abs(np.linspace(min_decay, max_decay, HY_WIDTH, dtype=np.float32).astype(np.float64))
    return jnp.asarray(z, F32), jnp.asarray(np.exp(-t * deltas), F32)


def _cm_short_conv(x, taps, masks):
    first_lane, last_lane, first_row, last_row = masks
    n1h = x.shape[0]
    r = pltpu.roll(x, 1, axis=1)
    prev = jnp.where(first_lane, jnp.where(first_row, 0.0, pltpu.roll(r, 1, axis=0)), r)
    l = pltpu.roll(x, HY_N2 - 1, axis=1)
    nxt = jnp.where(last_lane, jnp.where(last_row, 0.0, pltpu.roll(l, n1h - 1, axis=0)), l)
    return taps[0] * prev + taps[1] * x + taps[2] * nxt + taps[3]


def _cm_masks(n1h):
    lane = lax.broadcasted_iota(jnp.int32, (n1h, HY_N2), 1)
    row = lax.broadcasted_iota(jnp.int32, (n1h, HY_N2), 0)
    return lane == 0, lane == HY_N2 - 1, row == 0, row == n1h - 1


def _cm_taps(cw_ref, cb_ref, layer, ch):
    width = 3 * HY_WIDTH
    base = layer * HY_SHORT * width + ch
    return cw_ref[base], cw_ref[base + width], cw_ref[base + 2 * width], cb_ref[layer * width + ch]


def _cm_twiddle(a2, n1, twr, twi, conj):
    out = []
    for h in range(2):
        ar = a2[:n1, h * HY_N2:(h + 1) * HY_N2]
        ai = a2[n1:, h * HY_N2:(h + 1) * HY_N2]
        if conj:
            out.append((ar * twr + ai * twi, ai * twr - ar * twi))
        else:
            out.append((ar * twr - ai * twi, ar * twi + ai * twr))
    return out


def _hycm_spec_kernel(skip_ref, hf_ref, hb_ref, f1r_ref, f2t_ref, twr_ref, twi_ref, k_ref, f1_s, f2t_s, *, layer):
    order, cblk = pl.program_id(0), pl.program_id(1)
    cb, n1h, _ = hf_ref.shape
    n1 = 2 * n1h
    f1_s[...] = f1r_ref[...].astype(BF16)
    f2t_s[...] = f2t_ref[...].astype(BF16)
    twr, twi = twr_ref[...], twi_ref[...]

    def group(gi, carry):
        base = gi * HY_CH_GROUP
        blocks = []
        for j in range(HY_CH_GROUP):
            x2 = jnp.concatenate([hf_ref[base + j], hb_ref[base + j]], axis=1).astype(BF16)
            for re, im in _cm_twiddle(_dot(f1_s[...], x2), n1, twr, twi, False):
                blocks.append(jnp.concatenate([re, im], axis=1).astype(BF16))
        b2 = _dot(jnp.concatenate(blocks, axis=0), f2t_s[...])
        for j in range(HY_CH_GROUP):
            bf = b2[(2 * j) * n1:(2 * j + 1) * n1]
            bb = b2[(2 * j + 1) * n1:(2 * j + 2) * n1]
            skip = skip_ref[(layer * HY_ORDER + order) * HY_WIDTH + cblk * cb + base + j]
            k_ref[0, base + j] = jnp.concatenate([bf[:, :HY_N2] + bb[:, :HY_N2] + skip,
                                                  bf[:, HY_N2:] - bb[:, HY_N2:]], axis=1)
        return carry

    lax.fori_loop(0, cb // HY_CH_GROUP, group, 0)


def _hycm_spectra(filt_cm, skip, consts):
    _, n1h, _ = filt_cm.shape
    n1 = 2 * n1h
    nblk = HY_WIDTH // HY_CB
    smem = pl.BlockSpec(memory_space=pltpu.SMEM)
    blk = lambda direction: pl.BlockSpec((HY_CB, n1h, HY_N2),
                                         lambda o, c: ((direction * HY_ORDER + o) * nblk + c, 0, 0))
    names = ("f1r", "f2t", "twr", "twi")
    return pl.pallas_call(
        functools.partial(_hycm_spec_kernel, layer=skip[1]),
        grid=(HY_ORDER, nblk),
        in_specs=[smem, blk(0), blk(1)] + [_full_spec(consts[k]) for k in names],
        out_specs=pl.BlockSpec((1, HY_CB, n1, 2 * HY_N2), lambda o, c: (o, c, 0, 0)),
        out_shape=jax.ShapeDtypeStruct((HY_ORDER, HY_WIDTH, n1, 2 * HY_N2), F32),
        scratch_shapes=[pltpu.VMEM((2 * n1, n1h), BF16), pltpu.VMEM((2 * HY_N2, 2 * HY_N2), BF16)],
        compiler_params=_params("parallel", "parallel"),
        name="hyena_spectra_cm",
    )(skip[0], filt_cm, filt_cm, *[consts[k] for k in names])


def _hycm_conv_kernel(cw_ref, cb_ref, u_ref, g_ref, k_ref, f1_ref, f1i_ref, f2t_ref, f2ti_ref, twr_ref, twi_ref,
                      o_ref, f1_s, f1i_s, f2t_s, f2ti_s, *, conv_u, u_ch0, g_ch0, layer):
    cblk = pl.program_id(0)
    _, cb, n1h, _ = u_ref.shape
    n1 = 2 * n1h
    for dst, src in ((f1_s, f1_ref), (f1i_s, f1i_ref), (f2t_s, f2t_ref), (f2ti_s, f2ti_ref)):
        dst[...] = src[...].astype(BF16)
    twr, twi = twr_ref[...], twi_ref[...]
    masks = _cm_masks(n1h)
    group_n = HY_CH_GROUP

    def group(gi, carry):
        base = gi * group_n
        x2s = []
        for j in range(group_n):
            xs = []
            for b in range(2):
                x = u_ref[b, base + j].astype(F32)
                if conv_u:
                    x = _cm_short_conv(x, _cm_taps(cw_ref, cb_ref, layer, u_ch0 + cblk * cb + base + j), masks)
                xs.append(x)
            x2s.append(jnp.concatenate(xs, axis=0).astype(BF16))
        blocks = []
        for j in range(0, group_n, 2):
            a2 = _dot(f1_s[...], jnp.concatenate([x2s[j], x2s[j + 1]], axis=1))
            for re, im in _cm_twiddle(a2, n1, twr, twi, False):
                blocks.append(jnp.concatenate([re, im], axis=1).astype(BF16))
        b2 = _dot(jnp.concatenate(blocks, axis=0), f2t_s[...])
        prods = []
        for j in range(group_n):
            br = b2[j * n1:(j + 1) * n1, :HY_N2]
            bi = b2[j * n1:(j + 1) * n1, HY_N2:]
            kr = k_ref[0, base + j, :, 0:HY_N2]
            ki = k_ref[0, base + j, :, HY_N2:]
            prods.append(jnp.concatenate([br * kr - bi * ki, br * ki + bi * kr], axis=1).astype(BF16))
        c2 = _dot(jnp.concatenate(prods, axis=0), f2ti_s[...])
        cols = []
        for j in range(group_n):
            cr = c2[j * n1:(j + 1) * n1, :HY_N2]
            ci = c2[j * n1:(j + 1) * n1, HY_N2:]
            cols.append(jnp.concatenate([cr * twr + ci * twi, ci * twr - cr * twi], axis=0).astype(BF16))
        for j in range(0, group_n, 2):
            y2 = _dot(f1i_s[...], jnp.concatenate([cols[j], cols[j + 1]], axis=1))
            for h in range(2):
                ch = base + j + h
                taps = _cm_taps(cw_ref, cb_ref, layer, g_ch0 + cblk * cb + ch)
                for b in range(2):
                    y = y2[b * n1h:(b + 1) * n1h, h * HY_N2:(h + 1) * HY_N2]
                    gate = _cm_short_conv(g_ref[b, ch].astype(F32), taps, masks)
                    o_ref[b, ch] = (gate * y).astype(o_ref.dtype)
        return carry

    lax.fori_loop(0, cb // group_n, group, 0)


def _hycm_conv(u, u_ch0, g, g_ch0, conv_w, conv_b, spectra, order, consts, conv_u):
    bx, _, n1h, _ = u.shape
    n1 = 2 * n1h
    nblk = HY_WIDTH // HY_CB
    smem = pl.BlockSpec(memory_space=pltpu.SMEM)
    data = lambda ch0: pl.BlockSpec((2, HY_CB, n1h, HY_N2), lambda c, p, ch0=ch0: (p, ch0 // HY_CB + c, 0, 0))
    names = ("f1", "f1i", "f2t", "f2ti", "twr", "twi")
    dft = pltpu.VMEM((2 * HY_N2, 2 * HY_N2), BF16)
    return pl.pallas_call(
        functools.partial(_hycm_conv_kernel, conv_u=conv_u, u_ch0=u_ch0, g_ch0=g_ch0, layer=conv_w[1]),
        grid=(nblk, bx // 2),
        in_specs=[smem, smem, data(u_ch0), data(g_ch0),
                  pl.BlockSpec((1, HY_CB, n1, 2 * HY_N2), lambda c, p: (order, c, 0, 0))]
                 + [_full_spec(consts[k]) for k in names],
        out_specs=pl.BlockSpec((2, HY_CB, n1h, HY_N2), lambda c, p: (p, c, 0, 0)),
        out_shape=jax.ShapeDtypeStruct((bx, HY_WIDTH, n1h, HY_N2), BF16),
        scratch_shapes=[pltpu.VMEM((2 * n1, n1), BF16), pltpu.VMEM((n1, 2 * n1), BF16), dft, dft],
        compiler_params=_params("parallel", "arbitrary"),
        name="hyena_conv_cm",
    )(conv_w[0], conv_b[0], u, g, spectra, *[consts[k] for k in names])


def _cm_consts(n):
    big_n = 2 * n
    n1h = n // HY_N2
    n1 = 2 * n1h
    a1 = 2.0 * np.pi * np.outer(np.arange(n1), np.arange(n1h)) / n1
    c1, s1 = np.cos(a1), np.sin(a1)
    idx = np.arange(HY_N2, dtype=np.float64)
    a2 = 2.0 * np.pi * np.outer(idx, idx) / HY_N2
    c2, s2 = np.cos(a2), np.sin(a2)
    at = 2.0 * np.pi * np.outer(np.arange(n1), idx) / big_n
    f1 = np.block([[c1, s1], [-s1, c1]])
    return {
        "f1": jnp.asarray(f1, F32), "f1r": jnp.asarray(f1[:, :n1h], F32),
        "f1i": jnp.asarray(np.block([[c1.T, -s1.T], [s1.T, c1.T]]), F32),
        "f2t": jnp.asarray(np.block([[c2, -s2], [s2, c2]]), F32),
        "f2ti": jnp.asarray(np.block([[c2, s2], [-s2, c2]]) / big_n, F32),
        "twr": jnp.asarray(np.cos(at), F32), "twi": jnp.asarray(-np.sin(at), F32),
    }


def _gelu_tanh(x):
    return x * (0.5 * (1.0 + jnp.tanh(math.sqrt(2.0 / math.pi) * (x + 0.044715 * (x * x * x)))))


def _merge_kernel(x_ref, sh_ref, sc_ref, gt_ref, g_ref, ya_ref, hf_ref, hb_ref, lg_ref, yc_ref,
                  wgate_ref, wa_ref, wb_ref, wc_ref, wo_ref, o_ref):
    x = x_ref[0]
    h = _norm_mod(x, g_ref[...], sh_ref[0], sc_ref[0]).astype(BF16)
    yb = ((hf_ref[0] + hb_ref[0]) * _gelu_tanh(lg_ref[0])).astype(BF16)
    y = None
    for j, (br, w_ref) in enumerate(((ya_ref[0], wa_ref), (yb, wb_ref), (yc_ref[0], wc_ref))):
        gate = _sigmoid(_dot(h, wgate_ref[:, j * D_MODEL:(j + 1) * D_MODEL]))
        term = gate * _dot(br, w_ref[...])
        y = term if y is None else y + term
    o_ref[0] = x + gt_ref[0] * _dot(y.astype(BF16), wo_ref[...])


def _merge(x, mods, norm_g, ya, hf, hb, lg, yc, wgate, wa, wb, wc, wo):
    bx, t, _ = x.shape
    tm = DENSE_TILE
    tile = lambda w: pl.BlockSpec((1, tm, w), lambda b, i: (b, i, 0))
    return pl.pallas_call(
        _merge_kernel,
        grid=(bx, t // tm),
        in_specs=[tile(D_MODEL), _mod_spec(mods, 0), _mod_spec(mods, 1), _mod_spec(mods, 2), _full_spec(norm_g),
                  tile(MLA_HEADS * V_HEAD), tile(LRU_WIDTH), tile(LRU_WIDTH), tile(LRU_WIDTH), tile(HY_WIDTH),
                  _full_spec(wgate), _full_spec(wa), _full_spec(wb), _full_spec(wc), _full_spec(wo)],
        out_specs=tile(D_MODEL),
        out_shape=jax.ShapeDtypeStruct(x.shape, F32),
        compiler_params=_params("parallel", "parallel"),
        name="merge",
    )(x, mods[0], mods[0], mods[0], _arr(norm_g), ya, hf, hb, lg, yc, *[_arr(w) for w in (wgate, wa, wb, wc, wo)])


def _ffn_kernel(x_ref, sh_ref, sc_ref, gt_ref, g_ref, wg_ref, wu_ref, wd_ref, fg_ref, o_ref, *, final):
    x = x_ref[0]
    h = _norm_mod(x, g_ref[...], sh_ref[0], sc_ref[0]).astype(BF16)
    gate = _dot(h, wg_ref[...])
    act = (gate * _sigmoid(gate) * _dot(h, wu_ref[...])).astype(BF16)
    y = x + gt_ref[0] * _dot(act, wd_ref[...])
    o_ref[0] = _rms(y, fg_ref[...]) if final else y


def _ffn(x, mods, norm_g, wg, wu, wd, final_g, final):
    bx, t, _ = x.shape
    tm = DENSE_TILE
    tile = pl.BlockSpec((1, tm, D_MODEL), lambda b, i: (b, i, 0))
    return pl.pallas_call(
        functools.partial(_ffn_kernel, final=final),
        grid=(bx, t // tm),
        in_specs=[tile, _mod_spec(mods, 3), _mod_spec(mods, 4), _mod_spec(mods, 5), _full_spec(norm_g),
                  _full_spec(wg), _full_spec(wu), _full_spec(wd), _full_spec(final_g)],
        out_specs=tile,
        out_shape=jax.ShapeDtypeStruct(x.shape, F32),
        compiler_params=_params("parallel", "parallel"),
        name="ffn",
    )(x, mods[0], mods[0], mods[0], *[_arr(w) for w in (norm_g, wg, wu, wd)], final_g)


_ROPE_SWAP = np.array([8, 9, 10, 11, 12, 13, 14, 15, 0, 1, 2, 3, 4, 5, 6, 7,
                       24, 25, 26, 27, 28, 29, 30, 31, 16, 17, 18, 19, 20, 21, 22, 23])


def _rope_tables(n):
    cos = np.zeros((n, HEAD_PAD))
    sin = np.zeros((n, HEAD_PAD))
    cos[:, :QK_NOPE + QK_ROPE] = 1.0
    if n % GRID_W == 0 and n > 0:
        pos = np.arange(n)
        seg = QK_ROPE // 2
        inv = 1.0 / (ROPE_BASE ** (np.arange(seg // 2, dtype=np.float64) * 2.0 / seg))
        for s, p in enumerate((pos // GRID_W, pos % GRID_W)):
            ang = p[:, None] * inv[None, :]
            base = QK_NOPE + s * seg
            cos[:, base:base + seg] = np.concatenate([np.cos(ang), np.cos(ang)], axis=-1)
            sin[:, base:base + seg] = np.concatenate([-np.sin(ang), np.sin(ang)], axis=-1)
    return jnp.asarray(cos, F32), jnp.asarray(sin, F32)


def _identity_rope_tables():
    cos = np.zeros((DENSE_TILE, HEAD_PAD))
    cos[:, :QK_NOPE + QK_ROPE] = 1.0
    return jnp.asarray(cos, F32), jnp.zeros((DENSE_TILE, HEAD_PAD), F32)


def _block_diag(w):
    l, g, i, j = w.shape
    return jnp.einsum("lgij,gh->lgihj", w, jnp.eye(g, dtype=w.dtype)).reshape(l, g * i, g * j)


def _prep_weights(p):
    depth = p["w_in"].shape[0]
    w_in = p["w_in"]
    kpe = w_in[..., Q_LORA + KV_LORA:MLA_IN]
    z64 = jnp.zeros((depth, D_MODEL, QK_NOPE), F32)
    z32 = jnp.zeros((depth, D_MODEL, HEAD_PAD - QK_NOPE - QK_ROPE), F32)
    row = lambda a: a[:, None, :]
    out = {
        "w_mla": jnp.concatenate([w_in[..., :Q_LORA + KV_LORA], z64, kpe, z32, z64, kpe[..., _ROPE_SWAP], z32],
                                 axis=-1).astype(BF16),
        "w_lx": w_in[..., IN_SPLITS[0]:IN_SPLITS[1]].astype(BF16),
        "w_lg": w_in[..., IN_SPLITS[1]:IN_SPLITS[2]].astype(BF16),
        "w_hy": w_in[..., IN_SPLITS[2]:IN_SPLITS[3]].astype(BF16),
        "w_gate": w_in[..., IN_SPLITS[3]:].astype(BF16),
    }
    wq = p["w_uq"].reshape(depth, Q_LORA, MLA_HEADS, QK_NOPE + QK_ROPE)
    pad = jnp.zeros((depth, Q_LORA, MLA_HEADS, HEAD_PAD - QK_NOPE - QK_ROPE), F32)
    out["w_q"] = jnp.concatenate([wq, pad], axis=-1).reshape(depth, Q_LORA, -1).astype(BF16)
    out["w_qs"] = jnp.concatenate([jnp.zeros((depth, Q_LORA, MLA_HEADS, QK_NOPE), F32),
                                   wq[..., QK_NOPE:][..., _ROPE_SWAP], pad], axis=-1
                                  ).reshape(depth, Q_LORA, -1).astype(BF16)
    wkv = p["w_ukv"].reshape(depth, KV_LORA, MLA_HEADS, QK_NOPE + V_HEAD)
    out["w_k"] = jnp.concatenate([wkv[..., :QK_NOPE],
                                  jnp.zeros((depth, KV_LORA, MLA_HEADS, HEAD_PAD - QK_NOPE), F32)],
                                 axis=-1).reshape(depth, KV_LORA, -1).astype(BF16)
    out["w_v"] = wkv[..., QK_NOPE:].reshape(depth, KV_LORA, -1).astype(BF16)
    for d, name in enumerate(("f", "b")):
        out["lru_w" + name] = (0.5 * jnp.concatenate([_block_diag(p["lru_wa"][:, d]), _block_diag(p["lru_wx"][:, d])],
                                                     axis=-1)).astype(BF16)
        out["lru_bias" + name] = row(0.5 * jnp.concatenate([p["lru_ba"][:, d], p["lru_bx"][:, d]], axis=-1))
    out["hy_w1"] = jnp.concatenate([p["hy_w1"], jnp.zeros((depth, HY_EMB_PAD - HY_EMB, HY_HID), F32)], axis=1)
    for name in ("w_br_a", "w_br_b", "w_br_c", "w_out", "ffn_w_gate", "ffn_w_up", "ffn_w_down"):
        out[name] = p[name].astype(BF16)
    for name in ("norm1_g", "norm2_g", "q_norm_g", "kv_norm_g", "lru_conv_b", "hy_b1", "hy_freq", "hy_conv_b"):
        out[name] = row(p[name])
    for name in ("lru_conv_w", "lru_lam", "hy_w2", "hy_b2", "hy_w_out", "hy_conv_w"):
        out[name] = p[name]
    out["hy_skip"] = p["hy_skip"][:, :, None, :]
    out["hy_w_out_t"] = p["hy_w_out"].transpose(0, 2, 1)
    out["hy_conv_w_flat"] = p["hy_conv_w"].reshape(-1)
    out["hy_conv_b_flat"] = p["hy_conv_b"].reshape(-1)
    out["hy_skip_flat"] = p["hy_skip"].reshape(-1)
    return out


def _mixers(q, k, v, lx, hy, w, layer, kv_ctx=None, h0=None, branch_out=True):
    bx, n, _ = lx.shape
    at = lambda name: (w[name], layer)
    if h0 is None:
        h0 = jnp.zeros((bx, 2, LRU_WIDTH), F32)
    hf, hb, hlast = _lru(lx, at("lru_conv_w"), at("lru_conv_b"), at("lru_wf"), at("lru_biasf"),
                         at("lru_wb"), at("lru_biasb"), at("lru_lam"), h0)
    if not branch_out:
        return None, hlast
    if kv_ctx is None:
        ya = _flash(q, k, v)
    else:
        ya = _flash(q, kv_ctx[0], kv_ctx[1], k, v)
    z, decay = _hy_tables(n)
    mlp = (at("hy_w1"), at("hy_b1"), at("hy_w2"), at("hy_b2"), at("hy_freq"))
    n1h = n // HY_N2
    if n1h % 8 == 0:
        consts = _cm_consts(n)
        filt_cm = _hy_filters_cm(z, *mlp, at("hy_w_out_t"), decay.T)
        spectra = _hycm_spectra(filt_cm, at("hy_skip_flat"), consts)
        hy_cm = hy.reshape(bx, n1h, HY_N2, 3 * HY_WIDTH).transpose(0, 3, 1, 2)
        cw, cb = at("hy_conv_w_flat"), at("hy_conv_b_flat")
        y1 = _hycm_conv(hy_cm, 0, hy_cm, HY_WIDTH, cw, cb, spectra, 0, consts, True)
        yc = _hycm_conv(y1, 0, hy_cm, 2 * HY_WIDTH, cw, cb, spectra, 1, consts, False)
        yc = yc.transpose(0, 2, 3, 1).reshape(bx, n, HY_WIDTH)
    else:
        consts = _hy_consts(n)
        filt = _hy_filters(z, *mlp, at("hy_w_out"), decay)
        spectra = _hy_spectra(filt, at("hy_skip"), consts)
        cbn = HY_WIDTH // HY_CB
        cw, cb = at("hy_conv_w"), at("hy_conv_b")
        y1 = _hy_conv(hy, 0, hy, cbn, cw, cb, spectra, 0, consts, True)
        yc = _hy_conv(y1, 0, hy, 2 * cbn, cw, cb, spectra, 1, consts, False)
    return (ya, hf, hb, yc), hlast


def _layer(x, xc, mods, w, layer, final_g, last):
    bx, s, _ = x.shape
    sc = xc.shape[1]
    at = lambda name: (w[name], layer)
    mods_l = (mods, layer, None)
    mods_c = (mods, layer, bx)
    flat = lambda a: a.reshape(1, bx * sc, a.shape[-1])
    unflat = lambda a: a.reshape(bx, sc, a.shape[-1])

    proj_w = tuple(at(name) for name in ("norm1_g", "w_mla", "w_lx", "w_lg", "w_hy", "q_norm_g", "kv_norm_g",
                                         "w_q", "w_qs", "w_k", "w_v"))
    qc, kc, vc, lxc, lgc, hyc = [unflat(a) for a in _in_proj(flat(xc), mods_c, proj_w, *_identity_rope_tables())]
    ql, kl, vl, lxl, lgl, hyl = _in_proj(x, mods_l, proj_w, *_rope_tables(s))

    br_c, h_c = _mixers(qc, kc, vc, lxc, hyc, w, layer, branch_out=not last)
    br_l, _ = _mixers(ql, kl, vl, lxl, hyl, w, layer, kv_ctx=(kc, vc), h0=h_c)

    merge_w = tuple(at(name) for name in ("w_gate", "w_br_a", "w_br_b", "w_br_c", "w_out"))
    ffn_w = tuple(at(name) for name in ("ffn_w_gate", "ffn_w_up", "ffn_w_down"))
    x = _merge(x, mods_l, at("norm1_g"), br_l[0], br_l[1], br_l[2], lgl, br_l[3], *merge_w)
    x = _ffn(x, mods_l, at("norm2_g"), *ffn_w, final_g, last)
    if not last:
        xcf = _merge(flat(xc), mods_c, at("norm1_g"), flat(br_c[0]), flat(br_c[1]), flat(br_c[2]), flat(lgc),
                     flat(br_c[3]), *merge_w)
        xc = unflat(_ffn(xcf, mods_c, at("norm2_g"), *ffn_w, final_g, False))
    return x, xc


def kernel(x, c, ctx, c_ctx, ada_w, ada_b, norm1_g, norm2_g, w_in, q_norm_g, w_uq, kv_norm_g, w_ukv,
           lru_conv_w, lru_conv_b, lru_wa, lru_ba, lru_wx, lru_bx, lru_lam,
           hy_conv_w, hy_conv_b, hy_w1, hy_b1, hy_w2, hy_b2, hy_freq, hy_w_out, hy_skip,
           w_br_a, w_br_b, w_br_c, w_out, ffn_w_gate, ffn_w_up, ffn_w_down, final_norm_g):
    depth = ada_w.shape[0]
    bx = x.shape[0]
    assert bx % 2 == 0 and bx + 1 <= 8
    cond = jnp.concatenate([c, c_ctx[None], jnp.zeros((8 - bx - 1, D_MODEL), F32)], axis=0)
    mods = _ada_mods(cond, ada_w, ada_b)[:, :, None, :]
    w = _prep_weights(dict(
        norm1_g=norm1_g, norm2_g=norm2_g, w_in=w_in, q_norm_g=q_norm_g, w_uq=w_uq,
        kv_norm_g=kv_norm_g, w_ukv=w_ukv, lru_conv_w=lru_conv_w, lru_conv_b=lru_conv_b,
        lru_wa=lru_wa, lru_ba=lru_ba, lru_wx=lru_wx, lru_bx=lru_bx, lru_lam=lru_lam,
        hy_conv_w=hy_conv_w, hy_conv_b=hy_conv_b, hy_w1=hy_w1, hy_b1=hy_b1, hy_w2=hy_w2, hy_b2=hy_b2,
        hy_freq=hy_freq, hy_w_out=hy_w_out, hy_skip=hy_skip, w_br_a=w_br_a, w_br_b=w_br_b,
        w_br_c=w_br_c, w_out=w_out, ffn_w_gate=ffn_w_gate, ffn_w_up=ffn_w_up, ffn_w_down=ffn_w_down))
    xc = ctx
    fg = final_norm_g[None]
    for layer in range(depth):
        x, xc = _layer(x, xc, mods, w, layer, fg, layer == depth - 1)
    return x
```

```python
import functools
import math

import numpy as np
import jax
import jax.numpy as jnp
from jax import lax
from jax.experimental import pallas as pl
from jax.experimental.pallas import tpu as pltpu

F32 = jnp.float32
BF16 = jnp.bfloat16

D_MODEL = 1024
GRID_W = 64
EPS = 1e-6

MLA_HEADS = 8
Q_LORA = 384
KV_LORA = 256
QK_NOPE = 64
QK_ROPE = 32
V_HEAD = 64
ROPE_BASE = 10000.0
SM_SCALE = (QK_NOPE + QK_ROPE) ** -0.5
HEAD_PAD = 128
MLA_Z = Q_LORA + KV_LORA + 2 * HEAD_PAD

LRU_WIDTH = 512
LRU_BLOCKS = 8
LRU_CONV = 4
LRU_C = 8.0

HY_WIDTH = 512
HY_ORDER = 2
HY_SHORT = 3
HY_EMB = 33
HY_EMB_PAD = 48
HY_HID = 64
HY_INNER = 2
HY_FAST_DECAY = 0.3
HY_SLOW_DECAY = 1.5
HY_DECAY_TARGET = 1e-2
HY_N2 = 128
HY_CB = 128
HY_GROUP = 4
HY_CH_GROUP = 16

FFN_HID = 2816
N_BRANCH = 3
MLA_IN = Q_LORA + KV_LORA + QK_ROPE
IN_SPLITS = (MLA_IN, MLA_IN + LRU_WIDTH, MLA_IN + 2 * LRU_WIDTH, MLA_IN + 2 * LRU_WIDTH + 3 * HY_WIDTH)

Q_PRESCALE = SM_SCALE * math.log2(math.e)
FLASH_Q_TILE = 1024
FLASH_KEY_CHUNK = 2048
TOKEN_TILE = 256
DENSE_TILE = 512
HALO = 8
VMEM_LIMIT = 56 * 1024 * 1024


def _params(*sem):
    return pltpu.CompilerParams(dimension_semantics=sem, vmem_limit_bytes=VMEM_LIMIT)


def _dot(a, b):
    return jnp.dot(a, b, preferred_element_type=F32)


def _rms(x, g):
    return x * lax.rsqrt(jnp.mean(x * x, axis=-1, keepdims=True) + EPS) * g


def _sigmoid(x):
    return 0.5 * jnp.tanh(0.5 * x) + 0.5


def _norm_mod(x, g, shift, scale):
    return _rms(x, g) * (1.0 + scale) + shift


def _arr(op):
    return op[0] if isinstance(op, tuple) else op


def _full_spec(op):
    once = pl.Buffered(1)
    if isinstance(op, tuple):
        arr, layer = op
        nd = arr.ndim
        return pl.BlockSpec((None,) + arr.shape[1:], lambda *_: (layer,) + (0,) * (nd - 1), pipeline_mode=once)
    nd = op.ndim
    return pl.BlockSpec(op.shape, lambda *_: (0,) * nd, pipeline_mode=once)


def _mod_spec(mods, j):
    arr, layer, row = mods
    if row is None:
        return pl.BlockSpec((None, 1, 1, D_MODEL), lambda b, i: (layer, b, 0, j))
    return pl.BlockSpec((None, 1, 1, D_MODEL), lambda b, i: (layer, row, 0, j))


def _ada_kernel(c_ref, w_ref, b_ref, o_ref):
    c = c_ref[...]
    s = (c * jax.nn.sigmoid(c)).astype(BF16)
    o_ref[0] = _dot(s, w_ref[0].astype(BF16)) + b_ref[0]


def _ada_mods(cond, ada_w, ada_b):
    depth, _, width = ada_w.shape
    tn = 1536
    return pl.pallas_call(
        _ada_kernel,
        grid=(depth, width // tn),
        in_specs=[pl.BlockSpec((8, D_MODEL), lambda l, j: (0, 0)),
                  pl.BlockSpec((1, D_MODEL, tn), lambda l, j: (l, 0, j)),
                  pl.BlockSpec((1, 1, tn), lambda l, j: (l, 0, j))],
        out_specs=pl.BlockSpec((1, 8, tn), lambda l, j: (l, 0, j)),
        out_shape=jax.ShapeDtypeStruct((depth, 8, width), F32),
        compiler_params=_params("arbitrary", "arbitrary"),
        name="ada_mods",
    )(cond, ada_w, ada_b.reshape(depth, 1, width))


def _inproj_kernel(x_ref, sh_ref, sc_ref, g_ref, wm_ref, wx_ref, wg_ref, wh_ref,
                   qg_ref, kvg_ref, wq_ref, wqs_ref, wk_ref, wv_ref, cos_ref, sin_ref,
                   q_ref, k_ref, v_ref, lx_ref, lg_ref, hy_ref):
    h = _norm_mod(x_ref[0], g_ref[...], sh_ref[0], sc_ref[0]).astype(BF16)
    lx_ref[0] = _dot(h, wx_ref[...])
    lg_ref[0] = _dot(h, wg_ref[...])
    hy_ref[0] = _dot(h, wh_ref[...]).astype(hy_ref.dtype)
    z = _dot(h, wm_ref[...])
    nq = _rms(z[:, :Q_LORA], qg_ref[...]).astype(BF16)
    nkv = _rms(z[:, Q_LORA:Q_LORA + KV_LORA], kvg_ref[...]).astype(BF16)
    pe = z[:, Q_LORA + KV_LORA:Q_LORA + KV_LORA + HEAD_PAD]
    pes = z[:, Q_LORA + KV_LORA + HEAD_PAD:]
    cos = cos_ref[...]
    sin = sin_ref[...]
    q = _dot(nq, wq_ref[...])
    qs = _dot(nq, wqs_ref[...])
    kn = _dot(nkv, wk_ref[...])
    k_pe = pe * cos + pes * sin
    for hd in range(MLA_HEADS):
        sl = slice(hd * HEAD_PAD, (hd + 1) * HEAD_PAD)
        q_ref[0, :, sl] = ((q[:, sl] * cos + qs[:, sl] * sin) * Q_PRESCALE).astype(BF16)
        k_ref[0, :, sl] = (kn[:, sl] + k_pe).astype(BF16)
    v_ref[0] = _dot(nkv, wv_ref[...]).astype(BF16)


def _in_proj(x, mods, weights, cos, sin):
    bx, t, _ = x.shape
    tm = DENSE_TILE
    tile = lambda w: pl.BlockSpec((1, tm, w), lambda b, i: (b, i, 0))
    if cos.shape[0] == tm:
        tab = pl.BlockSpec((tm, HEAD_PAD), lambda b, i: (0, 0))
    else:
        tab = pl.BlockSpec((tm, HEAD_PAD), lambda b, i: (i, 0))
    outs = ((MLA_HEADS * HEAD_PAD, BF16), (MLA_HEADS * HEAD_PAD, BF16), (MLA_HEADS * V_HEAD, BF16),
            (LRU_WIDTH, F32), (LRU_WIDTH, F32), (3 * HY_WIDTH, BF16))
    return pl.pallas_call(
        _inproj_kernel,
        grid=(bx, t // tm),
        in_specs=[tile(D_MODEL), _mod_spec(mods, 0), _mod_spec(mods, 1)]
                 + [_full_spec(w) for w in weights] + [tab, tab],
        out_specs=[tile(w) for w, _ in outs],
        out_shape=[jax.ShapeDtypeStruct((bx, t, w), dt) for w, dt in outs],
        compiler_params=_params("parallel", "parallel"),
        name="in_proj",
    )(x, mods[0], mods[0], *[_arr(w) for w in weights], cos, sin)


def _qk(q, k):
    return lax.dot_general(q, k, (((1,), (1,)), ((), ())), preferred_element_type=F32)


def _flash_kernel(*refs, chunks):
    q_ref, o_ref = refs[0], refs[-1]
    kv = refs[1:-1]
    state = [None, None]
    for src, off, size in chunks:
        k_ref, v_ref = kv[2 * src], kv[2 * src + 1]
        v = v_ref[0, off:off + size, :]
        for h in range(2):
            sl = slice(h * HEAD_PAD, (h + 1) * HEAD_PAD)
            s = _qk(q_ref[0, :, sl], k_ref[0, off:off + size, sl])
            m_blk = jnp.max(s, axis=-1, keepdims=True)
            if state[h] is None:
                m = m_blk
                p = jnp.exp2(s - m)
                l = jnp.sum(p, axis=-1, keepdims=True)
                acc = _dot(p.astype(BF16), v)
            else:
                m_old, l, acc = state[h]
                m = jnp.maximum(m_old, m_blk)
                alpha = jnp.exp2(m_old - m)
                p = jnp.exp2(s - m)
                l = alpha * l + jnp.sum(p, axis=-1, keepdims=True)
                acc = alpha * acc + _dot(p.astype(BF16), v)
            state[h] = (m, l, acc)
    outs = [acc / l for _, l, acc in state]
    lane = lax.broadcasted_iota(jnp.int32, outs[0].shape, 1)
    o_ref[0] = jnp.where(lane < V_HEAD, outs[0], outs[1]).astype(o_ref.dtype)


def _flash(q, kc, vc, kl=None, vl=None):
    bx, t, _ = q.shape
    tq = min(FLASH_Q_TILE, t)
    sc = kc.shape[1]
    pairs = MLA_HEADS // 2
    qspec = pl.BlockSpec((1, tq, 2 * HEAD_PAD), lambda b, hp, i: (b, i, hp))
    kspec = lambda n: pl.BlockSpec((1, n, 2 * HEAD_PAD), lambda b, hp, i: (b, 0, hp))
    vspec = lambda n: pl.BlockSpec((1, n, 2 * V_HEAD), lambda b, hp, i: (b, 0, hp))
    in_specs = [qspec, kspec(sc), vspec(sc)]
    args = [q, kc, vc]
    chunks = [(0, 0, sc)]
    if kl is not None:
        sl = kl.shape[1]
        size = min(FLASH_KEY_CHUNK, sl)
        chunks += [(1, off, size) for off in range(0, sl, size)]
        in_specs += [kspec(sl), vspec(sl)]
        args += [kl, vl]
    return pl.pallas_call(
        functools.partial(_flash_kernel, chunks=tuple(chunks)),
        grid=(bx, pairs, t // tq),
        in_specs=in_specs,
        out_specs=pl.BlockSpec((1, tq, 2 * V_HEAD), lambda b, hp, i: (b, i, hp)),
        out_shape=jax.ShapeDtypeStruct((bx, t, MLA_HEADS * V_HEAD), BF16),
        compiler_params=_params("parallel", "parallel", "arbitrary"),
        name="flash",
    )(*args)


def _lru_kernel(xf_ref, xfp_ref, xfn_ref, xb_ref, xbp_ref, xbn_ref, cw_ref, cb_ref,
                wf_ref, bf_ref, wb_ref, bb_ref, lam_ref, h0_ref,
                hf_ref, hb_ref, hl_ref, af_s, bfw_s, ab_s, bbw_s, pad_s, carry_s):
    t = pl.program_id(0)
    nt = pl.num_programs(0)
    bx, tm, _ = xf_ref.shape

    @pl.when(t == 0)
    def _():
        carry_s[...] = h0_ref[...]

    def coeffs(b, x_ref, prev_ref, next_ref, tile, w_ref, bias_ref, lam, a_s, b_s):
        x = x_ref[b]
        pad_s[0:HALO, :] = jnp.where(tile > 0, prev_ref[b], 0.0)
        pad_s[HALO:HALO + tm, :] = x
        pad_s[HALO + tm:, :] = jnp.where(tile < nt - 1, next_ref[b], 0.0)
        xp = pad_s[...]
        rows = tm + 2 * HALO
        tap = lambda k: pltpu.roll(xp, k % rows, axis=0)[HALO:HALO + tm]
        u = (cw_ref[0:1, :] * tap(2) + cw_ref[1:2, :] * tap(1) + cw_ref[2:3, :] * x
             + cw_ref[3:4, :] * tap(-1) + cb_ref[...])
        y = _dot(u.astype(BF16), w_ref[...]) + bias_ref[...]
        t_r = jnp.tanh(y[:, :LRU_WIDTH])
        t_i = jnp.tanh(y[:, LRU_WIDTH:])
        neg = -lam
        softplus = jnp.maximum(neg, 0.0) + jnp.log1p(jnp.exp(-jnp.abs(neg)))
        c = (-0.5 * LRU_C * math.log2(math.e)) * softplus
        a = jnp.exp2(c * t_r + c)
        a_s[b] = a
        var = 1.0 - a * a
        half = 0.5 * (jnp.where(var > 0.0, var * lax.rsqrt(var), 0.0) * u)
        b_s[b] = half * t_i + half

    for b in range(bx):
        coeffs(b, xf_ref, xfp_ref, xfn_ref, t, wf_ref, bf_ref, lam_ref[0:1, :], af_s, bfw_s)
        coeffs(b, xb_ref, xbp_ref, xbn_ref, nt - 1 - t, wb_ref, bb_ref, lam_ref[1:2, :], ab_s, bbw_s)

    def body(j, hs):
        jb = tm - 1 - j
        out = []
        for b in range(bx):
            hf = af_s[b, pl.ds(j, 1), :] * hs[2 * b] + bfw_s[b, pl.ds(j, 1), :]
            hf_ref[b, pl.ds(j, 1), :] = hf
            hb = ab_s[b, pl.ds(jb, 1), :] * hs[2 * b + 1] + bbw_s[b, pl.ds(jb, 1), :]
            hb_ref[b, pl.ds(jb, 1), :] = hb
            out += [hf, hb]
        return tuple(out)

    init = tuple(carry_s[b, d:d + 1, :] for b in range(bx) for d in range(2))
    hs = lax.fori_loop(0, tm, body, init, unroll=8)
    for b in range(bx):
        for d in range(2):
            carry_s[b, d:d + 1, :] = hs[2 * b + d]
    hl_ref[...] = carry_s[...]


def _lru(lx, conv_w, conv_b, wf, biasf, wb, biasb, lam, h0):
    bx, t, w = lx.shape
    tm = TOKEN_TILE
    nt = t // tm
    hb_per = tm // HALO
    last_halo = t // HALO - 1
    fwd = lambda i: (0, i, 0)
    bwd = lambda i: (0, nt - 1 - i, 0)
    prev_of = lambda f: (lambda i: (0, jnp.maximum(f(i)[1] * hb_per - 1, 0), 0))
    next_of = lambda f: (lambda i: (0, jnp.minimum((f(i)[1] + 1) * hb_per, last_halo), 0))
    tile = lambda f: pl.BlockSpec((bx, tm, w), f)
    halo = lambda f: pl.BlockSpec((bx, HALO, w), f)
    state = pl.BlockSpec((bx, 2, w), lambda i: (0, 0, 0))
    return pl.pallas_call(
        _lru_kernel,
        grid=(nt,),
        in_specs=[tile(fwd), halo(prev_of(fwd)), halo(next_of(fwd)),
                  tile(bwd), halo(prev_of(bwd)), halo(next_of(bwd)),
                  _full_spec(conv_w), _full_spec(conv_b), _full_spec(wf), _full_spec(biasf),
                  _full_spec(wb), _full_spec(biasb), _full_spec(lam), state],
        out_specs=[tile(fwd), tile(bwd), state],
        out_shape=[jax.ShapeDtypeStruct((bx, t, w), F32), jax.ShapeDtypeStruct((bx, t, w), F32),
                   jax.ShapeDtypeStruct((bx, 2, w), F32)],
        scratch_shapes=[pltpu.VMEM((bx, tm, w), F32)] * 4
                       + [pltpu.VMEM((tm + 2 * HALO, w), F32), pltpu.VMEM((bx, 2, w), F32)],
        compiler_params=_params("arbitrary"),
        name="rglru",
    )(lx, lx, lx, lx, lx, lx, *[_arr(w) for w in (conv_w, conv_b, wf, biasf, wb, biasb, lam)], h0)


def _split_bf16(x):
    hi = x.astype(BF16)
    return hi, (x - hi.astype(F32)).astype(BF16)


def _dot3(a, b):
    ah, al = _split_bf16(a)
    bh, bl = _split_bf16(b)
    return _dot(ah, bh) + (_dot(ah, bl) + _dot(al, bh))


def _hyfilt_kernel(z_ref, w1_ref, b1_ref, w2_ref, b2_ref, fr_ref, wo_ref, dec_ref, o_ref):
    fr = fr_ref[...]
    h = jnp.sin(fr * (_dot3(z_ref[...], w1_ref[...]) + b1_ref[...]))
    for j in range(HY_INNER):
        h = jnp.sin(fr * (_dot3(h, w2_ref[j]) + b2_ref[j:j + 1, :]))
    dec = dec_ref[...]
    for g in range(2 * HY_ORDER):
        sl = slice(g * HY_WIDTH, (g + 1) * HY_WIDTH)
        o_ref[:, sl] = _dot3(h, wo_ref[:, sl]) * dec


def _hyfilt_cm_kernel(z_ref, w1_ref, b1_ref, w2_ref, b2_ref, fr_ref, wot_ref, dect_ref, o_ref):
    tn = z_ref.shape[0]
    fr = fr_ref[...]
    h = jnp.sin(fr * (_dot3(z_ref[...], w1_ref[...]) + b1_ref[...]))
    for j in range(HY_INNER):
        h = jnp.sin(fr * (_dot3(h, w2_ref[j]) + b2_ref[j:j + 1, :]))
    hh, hl = _split_bf16(h)
    nt = lambda a, b: lax.dot_general(a, b, (((1,), (1,)), ((), ())), preferred_element_type=F32)
    dect = dect_ref[...]
    for g in range(2 * HY_ORDER):
        wh, wl = _split_bf16(wot_ref[g * HY_WIDTH:(g + 1) * HY_WIDTH, :])
        res = (nt(wh, hh) + (nt(wh, hl) + nt(wl, hh))) * dect
        o_ref[g * HY_WIDTH:(g + 1) * HY_WIDTH] = res.reshape(HY_WIDTH, tn // HY_N2, HY_N2)


def _hy_filters_cm(z, w1, b1, w2, b2, freq, w_out_t, decay_t):
    n = z.shape[0]
    tn = 8 * HY_N2
    width = 2 * HY_ORDER * HY_WIDTH
    return pl.pallas_call(
        _hyfilt_cm_kernel,
        grid=(n // tn,),
        in_specs=[pl.BlockSpec((tn, HY_EMB_PAD), lambda i: (i, 0)), _full_spec(w1), _full_spec(b1),
                  _full_spec(w2), _full_spec(b2), _full_spec(freq), _full_spec(w_out_t),
                  pl.BlockSpec((HY_WIDTH, tn), lambda i: (0, i))],
        out_specs=pl.BlockSpec((width, tn // HY_N2, HY_N2), lambda i: (0, i, 0)),
        out_shape=jax.ShapeDtypeStruct((width, n // HY_N2, HY_N2), F32),
        compiler_params=_params("parallel"),
        name="hyena_filters_cm",
    )(z, *[_arr(w) for w in (w1, b1, w2, b2, freq, w_out_t)], decay_t)


def _hy_filters(z, w1, b1, w2, b2, freq, w_out, decay):
    n = z.shape[0]
    tn = min(n, 512)
    width = 2 * HY_ORDER * HY_WIDTH
    return pl.pallas_call(
        _hyfilt_kernel,
        grid=(n // tn,),
        in_specs=[pl.BlockSpec((tn, HY_EMB_PAD), lambda i: (i, 0)), _full_spec(w1), _full_spec(b1),
                  _full_spec(w2), _full_spec(b2), _full_spec(freq), _full_spec(w_out),
                  pl.BlockSpec((tn, HY_WIDTH), lambda i: (i, 0))],
        out_specs=pl.BlockSpec((tn, width), lambda i: (i, 0)),
        out_shape=jax.ShapeDtypeStruct((n, width), F32),
        compiler_params=_params("parallel"),
        name="hyena_filters",
    )(z, *[_arr(w) for w in (w1, b1, w2, b2, freq, w_out)], decay)


def _bitrev(k, bits):
    r = jnp.zeros_like(k)
    for b in range(bits):
        r = r | (((k >> b) & 1) << (bits - 1 - b))
    return r


def _fft_lead_fwd(sre, sim, wr_ref, wi_ref, n1):
    half = n1 // 2
    m = half
    first = True
    while m >= 1:
        shift = int(math.log2(m))
        stride = half // m

        def body(q, c, m=m, shift=shift, stride=stride, first=first):
            grp = q >> shift
            j = q - (grp << shift)
            i0 = (grp << (shift + 1)) + j
            i1 = i0 + m
            wr = wr_ref[j * stride]
            wi = wi_ref[j * stride]
            ar, ai = sre[i0], sim[i0]
            if first:
                dr, di = ar, ai
            else:
                br, bi = sre[i1], sim[i1]
                sre[i0] = ar + br
                sim[i0] = ai + bi
                dr, di = ar - br, ai - bi
            sre[i1] = dr * wr - di * wi
            sim[i1] = dr * wi + di * wr
            return c

        lax.fori_loop(0, half, body, 0)
        first = False
        m //= 2


def _fft_lead_inv(sre, sim, wr_ref, wi_ref, n1):
    half = n1 // 2
    m = 1
    while m <= half:
        shift = int(math.log2(m))
        stride = half // m
        last = m == half

        def body(q, c, m=m, shift=shift, stride=stride, last=last):
            grp = q >> shift
            j = q - (grp << shift)
            i0 = (grp << (shift + 1)) + j
            i1 = i0 + m
            wr = wr_ref[j * stride]
            wi = wi_ref[j * stride]
            ar, ai = sre[i0], sim[i0]
            br, bi = sre[i1], sim[i1]
            tr = br * wr + bi * wi
            ti = bi * wr - br * wi
            sre[i0] = ar + tr
            sim[i0] = ai + ti
            if not last:
                sre[i1] = ar - tr
                sim[i1] = ai - ti
            return c

        lax.fori_loop(0, half, body, 0)
        m *= 2


def _spectrum_loop(sre, sim, f2_s, tw, n1, emit):
    tw0r_ref, tw0i_ref, wgr_ref, wgi_ref = tw
    bits = int(math.log2(n1))
    group = min(HY_GROUP, n1)

    def body(kb, tws):
        ks, xs = [], []
        for g in range(group):
            k1 = kb * group + g
            blk = _bitrev(k1, bits)
            tr, ti = tws[2 * g], tws[2 * g + 1]
            ar, ai = sre[blk], sim[blk]
            xs.append(jnp.concatenate([ar * tr - ai * ti, ar * ti + ai * tr], axis=0).astype(BF16))
            ks.append((k1, blk))
        b2 = _dot(f2_s[...], jnp.concatenate(xs, axis=1))
        emit(ks, b2, tws)
        wgr, wgi = wgr_ref[...], wgi_ref[...]
        nxt = []
        for g in range(group):
            tr, ti = tws[2 * g], tws[2 * g + 1]
            nxt += [tr * wgr - ti * wgi, tr * wgi + ti * wgr]
        return tuple(nxt)

    init = tuple(r[g] for g in range(group) for r in (tw0r_ref, tw0i_ref))
    lax.fori_loop(0, n1 // group, body, init)


def _group_cols(b2, g, cb):
    return b2[:HY_N2, g * cb:(g + 1) * cb], b2[HY_N2:, g * cb:(g + 1) * cb]


def _hyspec_kernel(wr_ref, wi_ref, hf_ref, hb_ref, skip_ref, f2_ref, tw0r_ref, tw0i_ref, wgr_ref, wgi_ref,
                   k_ref, sre, sim, f2_s):
    n = hf_ref.shape[0]
    n1 = 2 * n // HY_N2
    half = n1 // 2
    cb = hf_ref.shape[1]
    f2_s[...] = f2_ref[...].astype(BF16)
    for direction, h_ref in enumerate((hf_ref, hb_ref)):
        sre[0:half] = h_ref[...].reshape(half, HY_N2, cb)
        sim[0:half] = jnp.zeros((half, HY_N2, cb), F32)
        _fft_lead_fwd(sre, sim, wr_ref, wi_ref, n1)

        def emit(ks, b2, tws, direction=direction):
            for g, (k1, _) in enumerate(ks):
                br, bi = _group_cols(b2, g, cb)
                if direction == 0:
                    k_ref[0, k1, 0:HY_N2, :] = br + skip_ref[0]
                    k_ref[0, k1, HY_N2:, :] = bi
                else:
                    k_ref[0, k1, 0:HY_N2, :] = k_ref[0, k1, 0:HY_N2, :] + br
                    k_ref[0, k1, HY_N2:, :] = k_ref[0, k1, HY_N2:, :] - bi

        _spectrum_loop(sre, sim, f2_s, (tw0r_ref, tw0i_ref, wgr_ref, wgi_ref), n1, emit)


def _hy_spectra(filt, skip, consts):
    n = filt.shape[0]
    n1 = 2 * n // HY_N2
    cbn = HY_WIDTH // HY_CB
    smem = pl.BlockSpec(memory_space=pltpu.SMEM)
    col = lambda direction: (lambda o, c: (0, (direction * HY_ORDER + o) * cbn + c))
    tw_names = ("tw0r", "tw0i", "wgr", "wgi")
    return pl.pallas_call(
        _hyspec_kernel,
        grid=(HY_ORDER, cbn),
        in_specs=[smem, smem,
                  pl.BlockSpec((n, HY_CB), col(0)), pl.BlockSpec((n, HY_CB), col(1)),
                  pl.BlockSpec((None, 1, 1, HY_CB), lambda o, c: (skip[1], o, 0, c)),
                  _full_spec(consts["f2"])] + [_full_spec(consts[k]) for k in tw_names],
        out_specs=pl.BlockSpec((1, n1, 2 * HY_N2, HY_CB), lambda o, c: (o, 0, 0, c)),
        out_shape=jax.ShapeDtypeStruct((HY_ORDER, n1, 2 * HY_N2, HY_WIDTH), F32),
        scratch_shapes=[pltpu.VMEM((n1, HY_N2, HY_CB), F32)] * 2 + [pltpu.VMEM((2 * HY_N2, 2 * HY_N2), BF16)],
        compiler_params=_params("parallel", "parallel"),
        name="hyena_spectra",
    )(consts["wr"], consts["wi"], filt, filt, skip[0], consts["f2"], *[consts[k] for k in tw_names])


def _short_conv3(x, pad_s, w_ref, b_ref):
    n = x.shape[0]
    pad_s[HALO:HALO + n, :] = x
    return (w_ref[0:1, :] * pad_s[HALO - 1:HALO - 1 + n, :] + w_ref[1:2, :] * x
            + w_ref[2:3, :] * pad_s[HALO + 1:HALO + 1 + n, :] + b_ref[...])


def _hyconv_kernel(wr_ref, wi_ref, u_ref, g_ref, ucw_ref, ucb_ref, gcw_ref, gcb_ref, k_ref,
                   f2_ref, f2i_ref, tw0r_ref, tw0i_ref, wgr_ref, wgi_ref, o_ref,
                   sre, sim, pad_s, f2_s, f2i_s, *, conv_u):
    n = u_ref.shape[1]
    cb = u_ref.shape[2]
    n1 = 2 * n // HY_N2
    half = n1 // 2
    f2_s[...] = f2_ref[...].astype(BF16)
    f2i_s[...] = f2i_ref[...].astype(BF16)
    margin = jnp.zeros((HALO, cb), F32)
    pad_s[0:HALO, :] = margin
    pad_s[HALO + n:, :] = margin

    for b, s in enumerate((sre, sim)):
        u = u_ref[b].astype(F32)
        if conv_u:
            u = _short_conv3(u, pad_s, ucw_ref, ucb_ref)
        s[0:half] = u.reshape(half, HY_N2, cb)
    _fft_lead_fwd(sre, sim, wr_ref, wi_ref, n1)

    def emit(ks, b2, tws):
        ps = []
        for g, (k1, _) in enumerate(ks):
            br, bi = _group_cols(b2, g, cb)
            kr = k_ref[0, k1, 0:HY_N2, :]
            ki = k_ref[0, k1, HY_N2:, :]
            ps.append(jnp.concatenate([br * kr - bi * ki, br * ki + bi * kr], axis=0).astype(BF16))
        c2 = _dot(f2i_s[...], jnp.concatenate(ps, axis=1))
        for g, (_, blk) in enumerate(ks):
            cr, ci = _group_cols(c2, g, cb)
            tr, ti = tws[2 * g], tws[2 * g + 1]
            sre[blk] = cr * tr + ci * ti
            sim[blk] = ci * tr - cr * ti

    _spectrum_loop(sre, sim, f2_s, (tw0r_ref, tw0i_ref, wgr_ref, wgi_ref), n1, emit)
    _fft_lead_inv(sre, sim, wr_ref, wi_ref, n1)

    for b, s in enumerate((sre, sim)):
        gate = _short_conv3(g_ref[b].astype(F32), pad_s, gcw_ref, gcb_ref)
        o_ref[b] = (gate * s[0:half].reshape(n, cb)).astype(o_ref.dtype)


def _hy_conv(u, u_col, g, g_col, conv_w, conv_b, spectra, order, consts, conv_u):
    bx, n, _ = u.shape
    n1 = 2 * n // HY_N2
    cbn = HY_WIDTH // HY_CB
    smem = pl.BlockSpec(memory_space=pltpu.SMEM)
    data = lambda col: pl.BlockSpec((2, n, HY_CB), lambda c, p, col=col: (p, 0, col + c))
    layer = conv_w[1]
    wrow = lambda rows, col: pl.BlockSpec((None, rows, HY_CB), lambda c, p, col=col: (layer, 0, col + c))
    ucol = u_col if conv_u else g_col
    const_names = ("f2", "f2i", "tw0r", "tw0i", "wgr", "wgi")
    dft = pltpu.VMEM((2 * HY_N2, 2 * HY_N2), BF16)
    return pl.pallas_call(
        functools.partial(_hyconv_kernel, conv_u=conv_u),
        grid=(cbn, bx // 2),
        in_specs=[smem, smem, data(u_col), data(g_col),
                  wrow(HY_SHORT, ucol), wrow(1, ucol), wrow(HY_SHORT, g_col), wrow(1, g_col),
                  pl.BlockSpec((1, n1, 2 * HY_N2, HY_CB), lambda c, p: (order, 0, 0, c))]
                 + [_full_spec(consts[k]) for k in const_names],
        out_specs=pl.BlockSpec((2, n, HY_CB), lambda c, p: (p, 0, c)),
        out_shape=jax.ShapeDtypeStruct((bx, n, HY_WIDTH), BF16),
        scratch_shapes=[pltpu.VMEM((n1, HY_N2, HY_CB), F32)] * 2
                       + [pltpu.VMEM((n + 2 * HALO, HY_CB), F32), dft, dft],
        compiler_params=_params("parallel", "arbitrary"),
        name="hyena_conv",
    )(consts["wr"], consts["wi"], u, g, conv_w[0], conv_b[0], conv_w[0], conv_b[0], spectra,
      *[consts[k] for k in const_names])


def _hy_consts(n):
    big_n = 2 * n
    n1 = big_n // HY_N2
    group = min(HY_GROUP, n1)
    q = np.arange(max(n1 // 2, 1), dtype=np.float64)
    ang1 = 2.0 * np.pi * q / n1
    idx = np.arange(HY_N2, dtype=np.float64)
    ang2 = 2.0 * np.pi * np.outer(idx, idx) / HY_N2
    c, s = np.cos(ang2), np.sin(ang2)
    f2 = np.block([[c, s], [-s, c]])
    f2i = np.block([[c, -s], [s, c]]) / big_n
    lane = np.ones((1, 1, HY_CB))
    ang0 = 2.0 * np.pi * np.arange(group)[:, None, None] * idx[None, :, None] / big_n
    angg = 2.0 * np.pi * group * idx[:, None] / big_n
    return {
        "wr": jnp.asarray(np.cos(ang1), F32), "wi": jnp.asarray(-np.sin(ang1), F32),
        "f2": jnp.asarray(f2, F32), "f2i": jnp.asarray(f2i, F32),
        "tw0r": jnp.asarray(np.cos(ang0) * lane, F32), "tw0i": jnp.asarray(-np.sin(ang0) * lane, F32),
        "wgr": jnp.asarray(np.cos(angg) * lane[0], F32), "wgi": jnp.asarray(-np.sin(angg) * lane[0], F32),
    }


def _hy_tables(n):
    t = np.linspace(0.0, 1.0, n, dtype=np.float32)[:, None].astype(np.float64)
    bands = (HY_EMB - 1) // 2
    w = 2.0 * np.pi * np.arange(n, dtype=np.float64) / n
    f = np.linspace(1e-4, bands - 1, bands, dtype=np.float32).astype(np.float64)
    ang = w[:, None] * f[None, :]
    z = np.concatenate([t, np.cos(ang), -np.sin(ang), np.zeros((n, HY_EMB_PAD - HY_EMB))], axis=-1)
    max_decay = math.log(HY_DECAY_TARGET) / HY_FAST_DECAY
    min_decay = math.log(HY_DECAY_TARGET) / HY_SLOW_DECAY
    deltas = np.abs(np.linspace(min_decay, max_decay, HY_WIDTH, dtype=np.float32).astype(np.float64))
    return jnp.asarray(z, F32), jnp.asarray(np.exp(-t * deltas), F32)


def _cm_short_conv(x, taps, masks):
    first_lane, last_lane, first_row, last_row = masks
    n1h = x.shape[0]
    r = pltpu.roll(x, 1, axis=1)
    prev = jnp.where(first_lane, jnp.where(first_row, 0.0, pltpu.roll(r, 1, axis=0)), r)
    l = pltpu.roll(x, HY_N2 - 1, axis=1)
    nxt = jnp.where(last_lane, jnp.where(last_row, 0.0, pltpu.roll(l, n1h - 1, axis=0)), l)
    return taps[0] * prev + taps[1] * x + taps[2] * nxt + taps[3]


def _cm_masks(n1h):
    lane = lax.broadcasted_iota(jnp.int32, (n1h, HY_N2), 1)
    row = lax.broadcasted_iota(jnp.int32, (n1h, HY_N2), 0)
    return lane == 0, lane == HY_N2 - 1, row == 0, row == n1h - 1


def _cm_taps(cw_ref, cb_ref, layer, ch):
    width = 3 * HY_WIDTH
    base = layer * HY_SHORT * width + ch
    return cw_ref[base], cw_ref[base + width], cw_ref[base + 2 * width], cb_ref[layer * width + ch]


def _cm_twiddle(a2, n1, twr, twi, conj):
    out = []
    for h in range(2):
        ar = a2[:n1, h * HY_N2:(h + 1) * HY_N2]
        ai = a2[n1:, h * HY_N2:(h + 1) * HY_N2]
        if conj:
            out.append((ar * twr + ai * twi, ai * twr - ar * twi))
        else:
            out.append((ar * twr - ai * twi, ar * twi + ai * twr))
    return out


def _hycm_spec_kernel(skip_ref, hf_ref, hb_ref, f1r_ref, f2t_ref, twr_ref, twi_ref, k_ref, f1_s, f2t_s, *, layer):
    order, cblk = pl.program_id(0), pl.program_id(1)
    cb, n1h, _ = hf_ref.shape
    n1 = 2 * n1h
    f1_s[...] = f1r_ref[...].astype(BF16)
    f2t_s[...] = f2t_ref[...].astype(BF16)
    twr, twi = twr_ref[...], twi_ref[...]

    def group(gi, carry):
        base = gi * HY_CH_GROUP
        blocks = []
        for j in range(HY_CH_GROUP):
            x2 = jnp.concatenate([hf_ref[base + j], hb_ref[base + j]], axis=1).astype(BF16)
            for re, im in _cm_twiddle(_dot(f1_s[...], x2), n1, twr, twi, False):
                blocks.append(jnp.concatenate([re, im], axis=1).astype(BF16))
        b2 = _dot(jnp.concatenate(blocks, axis=0), f2t_s[...])
        for j in range(HY_CH_GROUP):
            bf = b2[(2 * j) * n1:(2 * j + 1) * n1]
            bb = b2[(2 * j + 1) * n1:(2 * j + 2) * n1]
            skip = skip_ref[(layer * HY_ORDER + order) * HY_WIDTH + cblk * cb + base + j]
            k_ref[0, base + j] = jnp.concatenate([bf[:, :HY_N2] + bb[:, :HY_N2] + skip,
                                                  bf[:, HY_N2:] - bb[:, HY_N2:]], axis=1)
        return carry

    lax.fori_loop(0, cb // HY_CH_GROUP, group, 0)


def _hycm_spectra(filt_cm, skip, consts):
    _, n1h, _ = filt_cm.shape
    n1 = 2 * n1h
    nblk = HY_WIDTH // HY_CB
    smem = pl.BlockSpec(memory_space=pltpu.SMEM)
    blk = lambda direction: pl.BlockSpec((HY_CB, n1h, HY_N2),
                                         lambda o, c: ((direction * HY_ORDER + o) * nblk + c, 0, 0))
    names = ("f1r", "f2t", "twr", "twi")
    return pl.pallas_call(
        functools.partial(_hycm_spec_kernel, layer=skip[1]),
        grid=(HY_ORDER, nblk),
        in_specs=[smem, blk(0), blk(1)] + [_full_spec(consts[k]) for k in names],
        out_specs=pl.BlockSpec((1, HY_CB, n1, 2 * HY_N2), lambda o, c: (o, c, 0, 0)),
        out_shape=jax.ShapeDtypeStruct((HY_ORDER, HY_WIDTH, n1, 2 * HY_N2), F32),
        scratch_shapes=[pltpu.VMEM((2 * n1, n1h), BF16), pltpu.VMEM((2 * HY_N2, 2 * HY_N2), BF16)],
        compiler_params=_params("parallel", "parallel"),
        name="hyena_spectra_cm",
    )(skip[0], filt_cm, filt_cm, *[consts[k] for k in names])


def _hycm_conv_kernel(cw_ref, cb_ref, u_ref, g_ref, k_ref, f1_ref, f1i_ref, f2t_ref, f2ti_ref, twr_ref, twi_ref,
                      o_ref, f1_s, f1i_s, f2t_s, f2ti_s, *, conv_u, u_ch0, g_ch0, layer):
    cblk = pl.program_id(0)
    _, cb, n1h, _ = u_ref.shape
    n1 = 2 * n1h
    for dst, src in ((f1_s, f1_ref), (f1i_s, f1i_ref), (f2t_s, f2t_ref), (f2ti_s, f2ti_ref)):
        dst[...] = src[...].astype(BF16)
    twr, twi = twr_ref[...], twi_ref[...]
    masks = _cm_masks(n1h)
    group_n = HY_CH_GROUP

    def group(gi, carry):
        base = gi * group_n
        x2s = []
        for j in range(group_n):
            xs = []
            for b in range(2):
                x = u_ref[b, base + j].astype(F32)
                if conv_u:
                    x = _cm_short_conv(x, _cm_taps(cw_ref, cb_ref, layer, u_ch0 + cblk * cb + base + j), masks)
                xs.append(x)
            x2s.append(jnp.concatenate(xs, axis=0).astype(BF16))
        blocks = []
        for j in range(0, group_n, 2):
            a2 = _dot(f1_s[...], jnp.concatenate([x2s[j], x2s[j + 1]], axis=1))
            for re, im in _cm_twiddle(a2, n1, twr, twi, False):
                blocks.append(jnp.concatenate([re, im], axis=1).astype(BF16))
        b2 = _dot(jnp.concatenate(blocks, axis=0), f2t_s[...])
        prods = []
        for j in range(group_n):
            br = b2[j * n1:(j + 1) * n1, :HY_N2]
            bi = b2[j * n1:(j + 1) * n1, HY_N2:]
            kr = k_ref[0, base + j, :, 0:HY_N2]
            ki = k_ref[0, base + j, :, HY_N2:]
            prods.append(jnp.concatenate([br * kr - bi * ki, br * ki + bi * kr], axis=1).astype(BF16))
        c2 = _dot(jnp.concatenate(prods, axis=0), f2ti_s[...])
        cols = []
        for j in range(group_n):
            cr = c2[j * n1:(j + 1) * n1, :HY_N2]
            ci = c2[j * n1:(j + 1) * n1, HY_N2:]
            cols.append(jnp.concatenate([cr * twr + ci * twi, ci * twr - cr * twi], axis=0).astype(BF16))
        for j in range(0, group_n, 2):
            y2 = _dot(f1i_s[...], jnp.concatenate([cols[j], cols[j + 1]], axis=1))
            for h in range(2):
                ch = base + j + h
                taps = _cm_taps(cw_ref, cb_ref, layer, g_ch0 + cblk * cb + ch)
                for b in range(2):
                    y = y2[b * n1h:(b + 1) * n1h, h * HY_N2:(h + 1) * HY_N2]
                    gate = _cm_short_conv(g_ref[b, ch].astype(F32), taps, masks)
                    o_ref[b, ch] = (gate * y).astype(o_ref.dtype)
        return carry

    lax.fori_loop(0, cb // group_n, group, 0)


def _hycm_conv(u, u_ch0, g, g_ch0, conv_w, conv_b, spectra, order, consts, conv_u):
    bx, _, n1h, _ = u.shape
    n1 = 2 * n1h
    nblk = HY_WIDTH // HY_CB
    smem = pl.BlockSpec(memory_space=pltpu.SMEM)
    data = lambda ch0: pl.BlockSpec((2, HY_CB, n1h, HY_N2), lambda c, p, ch0=ch0: (p, ch0 // HY_CB + c, 0, 0))
    names = ("f1", "f1i", "f2t", "f2ti", "twr", "twi")
    dft = pltpu.VMEM((2 * HY_N2, 2 * HY_N2), BF16)
    return pl.pallas_call(
        functools.partial(_hycm_conv_kernel, conv_u=conv_u, u_ch0=u_ch0, g_ch0=g_ch0, layer=conv_w[1]),
        grid=(nblk, bx // 2),
        in_specs=[smem, smem, data(u_ch0), data(g_ch0),
                  pl.BlockSpec((1, HY_CB, n1, 2 * HY_N2), lambda c, p: (order, c, 0, 0))]
                 + [_full_spec(consts[k]) for k in names],
        out_specs=pl.BlockSpec((2, HY_CB, n1h, HY_N2), lambda c, p: (p, c, 0, 0)),
        out_shape=jax.ShapeDtypeStruct((bx, HY_WIDTH, n1h, HY_N2), BF16),
        scratch_shapes=[pltpu.VMEM((2 * n1, n1), BF16), pltpu.VMEM((n1, 2 * n1), BF16), dft, dft],
        compiler_params=_params("parallel", "arbitrary"),
        name="hyena_conv_cm",
    )(conv_w[0], conv_b[0], u, g, spectra, *[consts[k] for k in names])


def _cm_consts(n):
    big_n = 2 * n
    n1h = n // HY_N2
    n1 = 2 * n1h
    a1 = 2.0 * np.pi * np.outer(np.arange(n1), np.arange(n1h)) / n1
    c1, s1 = np.cos(a1), np.sin(a1)
    idx = np.arange(HY_N2, dtype=np.float64)
    a2 = 2.0 * np.pi * np.outer(idx, idx) / HY_N2
    c2, s2 = np.cos(a2), np.sin(a2)
    at = 2.0 * np.pi * np.outer(np.arange(n1), idx) / big_n
    f1 = np.block([[c1, s1], [-s1, c1]])
    return {
        "f1": jnp.asarray(f1, F32), "f1r": jnp.asarray(f1[:, :n1h], F32),
        "f1i": jnp.asarray(np.block([[c1.T, -s1.T], [s1.T, c1.T]]), F32),
        "f2t": jnp.asarray(np.block([[c2, -s2], [s2, c2]]), F32),
        "f2ti": jnp.asarray(np.block([[c2, s2], [-s2, c2]]) / big_n, F32),
        "twr": jnp.asarray(np.cos(at), F32), "twi": jnp.asarray(-np.sin(at), F32),
    }


def _gelu_tanh(x):
    return x * (0.5 * (1.0 + jnp.tanh(math.sqrt(2.0 / math.pi) * (x + 0.044715 * (x * x * x)))))


def _merge_kernel(x_ref, sh_ref, sc_ref, gt_ref, g_ref, ya_ref, hf_ref, hb_ref, lg_ref, yc_ref,
                  wgate_ref, wa_ref, wb_ref, wc_ref, wo_ref, o_ref):
    x = x_ref[0]
    h = _norm_mod(x, g_ref[...], sh_ref[0], sc_ref[0]).astype(BF16)
    yb = ((hf_ref[0] + hb_ref[0]) * _gelu_tanh(lg_ref[0])).astype(BF16)
    y = None
    for j, (br, w_ref) in enumerate(((ya_ref[0], wa_ref), (yb, wb_ref), (yc_ref[0], wc_ref))):
        gate = _sigmoid(_dot(h, wgate_ref[:, j * D_MODEL:(j + 1) * D_MODEL]))
        term = gate * _dot(br, w_ref[...])
        y = term if y is None else y + term
    o_ref[0] = x + gt_ref[0] * _dot(y.astype(BF16), wo_ref[...])


def _merge(x, mods, norm_g, ya, hf, hb, lg, yc, wgate, wa, wb, wc, wo):
    bx, t, _ = x.shape
    tm = DENSE_TILE
    tile = lambda w: pl.BlockSpec((1, tm, w), lambda b, i: (b, i, 0))
    return pl.pallas_call(
        _merge_kernel,
        grid=(bx, t // tm),
        in_specs=[tile(D_MODEL), _mod_spec(mods, 0), _mod_spec(mods, 1), _mod_spec(mods, 2), _full_spec(norm_g),
                  tile(MLA_HEADS * V_HEAD), tile(LRU_WIDTH), tile(LRU_WIDTH), tile(LRU_WIDTH), tile(HY_WIDTH),
                  _full_spec(wgate), _full_spec(wa), _full_spec(wb), _full_spec(wc), _full_spec(wo)],
        out_specs=tile(D_MODEL),
        out_shape=jax.ShapeDtypeStruct(x.shape, F32),
        compiler_params=_params("parallel", "parallel"),
        name="merge",
    )(x, mods[0], mods[0], mods[0], _arr(norm_g), ya, hf, hb, lg, yc, *[_arr(w) for w in (wgate, wa, wb, wc, wo)])


def _ffn_kernel(x_ref, sh_ref, sc_ref, gt_ref, g_ref, wg_ref, wu_ref, wd_ref, fg_ref, o_ref, *, final):
    x = x_ref[0]
    h = _norm_mod(x, g_ref[...], sh_ref[0], sc_ref[0]).astype(BF16)
    gate = _dot(h, wg_ref[...])
    act = (gate * _sigmoid(gate) * _dot(h, wu_ref[...])).astype(BF16)
    y = x + gt_ref[0] * _dot(act, wd_ref[...])
    o_ref[0] = _rms(y, fg_ref[...]) if final else y


def _ffn(x, mods, norm_g, wg, wu, wd, final_g, final):
    bx, t, _ = x.shape
    tm = DENSE_TILE
    tile = pl.BlockSpec((1, tm, D_MODEL), lambda b, i: (b, i, 0))
    return pl.pallas_call(
        functools.partial(_ffn_kernel, final=final),
        grid=(bx, t // tm),
        in_specs=[tile, _mod_spec(mods, 3), _mod_spec(mods, 4), _mod_spec(mods, 5), _full_spec(norm_g),
                  _full_spec(wg), _full_spec(wu), _full_spec(wd), _full_spec(final_g)],
        out_specs=tile,
        out_shape=jax.ShapeDtypeStruct(x.shape, F32),
        compiler_params=_params("parallel", "parallel"),
        name="ffn",
    )(x, mods[0], mods[0], mods[0], *[_arr(w) for w in (norm_g, wg, wu, wd)], final_g)


_ROPE_SWAP = np.array([8, 9, 10, 11, 12, 13, 14, 15, 0, 1, 2, 3, 4, 5, 6, 7,
                       24, 25, 26, 27, 28, 29, 30, 31, 16, 17, 18, 19, 20, 21, 22, 23])


def _rope_tables(n):
    cos = np.zeros((n, HEAD_PAD))
    sin = np.zeros((n, HEAD_PAD))
    cos[:, :QK_NOPE + QK_ROPE] = 1.0
    if n % GRID_W == 0 and n > 0:
        pos = np.arange(n)
        seg = QK_ROPE // 2
        inv = 1.0 / (ROPE_BASE ** (np.arange(seg // 2, dtype=np.float64) * 2.0 / seg))
        for s, p in enumerate((pos // GRID_W, pos % GRID_W)):
            ang = p[:, None] * inv[None, :]
            base = QK_NOPE + s * seg
            cos[:, base:base + seg] = np.concatenate([np.cos(ang), np.cos(ang)], axis=-1)
            sin[:, base:base + seg] = np.concatenate([-np.sin(ang), np.sin(ang)], axis=-1)
    return jnp.asarray(cos, F32), jnp.asarray(sin, F32)


def _identity_rope_tables():
    cos = np.zeros((DENSE_TILE, HEAD_PAD))
    cos[:, :QK_NOPE + QK_ROPE] = 1.0
    return jnp.asarray(cos, F32), jnp.zeros((DENSE_TILE, HEAD_PAD), F32)


def _block_diag(w):
    l, g, i, j = w.shape
    return jnp.einsum("lgij,gh->lgihj", w, jnp.eye(g, dtype=w.dtype)).reshape(l, g * i, g * j)


def _prep_weights(p):
    depth = p["w_in"].shape[0]
    w_in = p["w_in"]
    kpe = w_in[..., Q_LORA + KV_LORA:MLA_IN]
    z64 = jnp.zeros((depth, D_MODEL, QK_NOPE), F32)
    z32 = jnp.zeros((depth, D_MODEL, HEAD_PAD - QK_NOPE - QK_ROPE), F32)
    row = lambda a: a[:, None, :]
    out = {
        "w_mla": jnp.concatenate([w_in[..., :Q_LORA + KV_LORA], z64, kpe, z32, z64, kpe[..., _ROPE_SWAP], z32],
                                 axis=-1).astype(BF16),
        "w_lx": w_in[..., IN_SPLITS[0]:IN_SPLITS[1]].astype(BF16),
        "w_lg": w_in[..., IN_SPLITS[1]:IN_SPLITS[2]].astype(BF16),
        "w_hy": w_in[..., IN_SPLITS[2]:IN_SPLITS[3]].astype(BF16),
        "w_gate": w_in[..., IN_SPLITS[3]:].astype(BF16),
    }
    wq = p["w_uq"].reshape(depth, Q_LORA, MLA_HEADS, QK_NOPE + QK_ROPE)
    pad = jnp.zeros((depth, Q_LORA, MLA_HEADS, HEAD_PAD - QK_NOPE - QK_ROPE), F32)
    out["w_q"] = jnp.concatenate([wq, pad], axis=-1).reshape(depth, Q_LORA, -1).astype(BF16)
    out["w_qs"] = jnp.concatenate([jnp.zeros((depth, Q_LORA, MLA_HEADS, QK_NOPE), F32),
                                   wq[..., QK_NOPE:][..., _ROPE_SWAP], pad], axis=-1
                                  ).reshape(depth, Q_LORA, -1).astype(BF16)
    wkv = p["w_ukv"].reshape(depth, KV_LORA, MLA_HEADS, QK_NOPE + V_HEAD)
    out["w_k"] = jnp.concatenate([wkv[..., :QK_NOPE],
                                  jnp.zeros((depth, KV_LORA, MLA_HEADS, HEAD_PAD - QK_NOPE), F32)],
                                 axis=-1).reshape(depth, KV_LORA, -1).astype(BF16)
    out["w_v"] = wkv[..., QK_NOPE:].reshape(depth, KV_LORA, -1).astype(BF16)
    for d, name in enumerate(("f", "b")):
        out["lru_w" + name] = (0.5 * jnp.concatenate([_block_diag(p["lru_wa"][:, d]), _block_diag(p["lru_wx"][:, d])],
                                                     axis=-1)).astype(BF16)
        out["lru_bias" + name] = row(0.5 * jnp.concatenate([p["lru_ba"][:, d], p["lru_bx"][:, d]], axis=-1))
    out["hy_w1"] = jnp.concatenate([p["hy_w1"], jnp.zeros((depth, HY_EMB_PAD - HY_EMB, HY_HID), F32)], axis=1)
    for name in ("w_br_a", "w_br_b", "w_br_c", "w_out", "ffn_w_gate", "ffn_w_up", "ffn_w_down"):
        out[name] = p[name].astype(BF16)
    for name in ("norm1_g", "norm2_g", "q_norm_g", "kv_norm_g", "lru_conv_b", "hy_b1", "hy_freq", "hy_conv_b"):
        out[name] = row(p[name])
    for name in ("lru_conv_w", "lru_lam", "hy_w2", "hy_b2", "hy_w_out", "hy_conv_w"):
        out[name] = p[name]
    out["hy_skip"] = p["hy_skip"][:, :, None, :]
    out["hy_w_out_t"] = p["hy_w_out"].transpose(0, 2, 1)
    out["hy_conv_w_flat"] = p["hy_conv_w"].reshape(-1)
    out["hy_conv_b_flat"] = p["hy_conv_b"].reshape(-1)
    out["hy_skip_flat"] = p["hy_skip"].reshape(-1)
    return out


def _mixers(q, k, v, lx, hy, w, layer, kv_ctx=None, h0=None, branch_out=True):
    bx, n, _ = lx.shape
    at = lambda name: (w[name], layer)
    if h0 is None:
        h0 = jnp.zeros((bx, 2, LRU_WIDTH), F32)
    hf, hb, hlast = _lru(lx, at("lru_conv_w"), at("lru_conv_b"), at("lru_wf"), at("lru_biasf"),
                         at("lru_wb"), at("lru_biasb"), at("lru_lam"), h0)
    if not branch_out:
        return None, hlast
    if kv_ctx is None:
        ya = _flash(q, k, v)
    else:
        ya = _flash(q, kv_ctx[0], kv_ctx[1], k, v)
    z, decay = _hy_tables(n)
    mlp = (at("hy_w1"), at("hy_b1"), at("hy_w2"), at("hy_b2"), at("hy_freq"))
    n1h = n // HY_N2
    if n1h % 8 == 0:
        consts = _cm_consts(n)
        filt_cm = _hy_filters_cm(z, *mlp, at("hy_w_out_t"), decay.T)
        spectra = _hycm_spectra(filt_cm, at("hy_skip_flat"), consts)
        hy_cm = hy.reshape(bx, n1h, HY_N2, 3 * HY_WIDTH).transpose(0, 3, 1, 2)
        cw, cb = at("hy_conv_w_flat"), at("hy_conv_b_flat")
        y1 = _hycm_conv(hy_cm, 0, hy_cm, HY_WIDTH, cw, cb, spectra, 0, consts, True)
        yc = _hycm_conv(y1, 0, hy_cm, 2 * HY_WIDTH, cw, cb, spectra, 1, consts, False)
        yc = yc.transpose(0, 2, 3, 1).reshape(bx, n, HY_WIDTH)
    else:
        consts = _hy_consts(n)
        filt = _hy_filters(z, *mlp, at("hy_w_out"), decay)
        spectra = _hy_spectra(filt, at("hy_skip"), consts)
        cbn = HY_WIDTH // HY_CB
        cw, cb = at("hy_conv_w"), at("hy_conv_b")
        y1 = _hy_conv(hy, 0, hy, cbn, cw, cb, spectra, 0, consts, True)
        yc = _hy_conv(y1, 0, hy, 2 * cbn, cw, cb, spectra, 1, consts, False)
    return (ya, hf, hb, yc), hlast


def _layer(x, xc, mods, w, layer, final_g, last):
    bx, s, _ = x.shape
    sc = xc.shape[1]
    at = lambda name: (w[name], layer)
    mods_l = (mods, layer, None)
    mods_c = (mods, layer, bx)
    flat = lambda a: a.reshape(1, bx * sc, a.shape[-1])
    unflat = lambda a: a.reshape(bx, sc, a.shape[-1])

    proj_w = tuple(at(name) for name in ("norm1_g", "w_mla", "w_lx", "w_lg", "w_hy", "q_norm_g", "kv_norm_g",
                                         "w_q", "w_qs", "w_k", "w_v"))
    qc, kc, vc, lxc, lgc, hyc = [unflat(a) for a in _in_proj(flat(xc), mods_c, proj_w, *_identity_rope_tables())]
    ql, kl, vl, lxl, lgl, hyl = _in_proj(x, mods_l, proj_w, *_rope_tables(s))

    br_c, h_c = _mixers(qc, kc, vc, lxc, hyc, w, layer, branch_out=not last)
    br_l, _ = _mixers(ql, kl, vl, lxl, hyl, w, layer, kv_ctx=(kc, vc), h0=h_c)

    merge_w = tuple(at(name) for name in ("w_gate", "w_br_a", "w_br_b", "w_br_c", "w_out"))
    ffn_w = tuple(at(name) for name in ("ffn_w_gate", "ffn_w_up", "ffn_w_down"))
    x = _merge(x, mods_l, at("norm1_g"), br_l[0], br_l[1], br_l[2], lgl, br_l[3], *merge_w)
    x = _ffn(x, mods_l, at("norm2_g"), *ffn_w, final_g, last)
    if not last:
        xcf = _merge(flat(xc), mods_c, at("norm1_g"), flat(br_c[0]), flat(br_c[1]), flat(br_c[2]), flat(lgc),
                     flat(br_c[3]), *merge_w)
        xc = unflat(_ffn(xcf, mods_c, at("norm2_g"), *ffn_w, final_g, False))
    return x, xc


def kernel(x, c, ctx, c_ctx, ada_w, ada_b, norm1_g, norm2_g, w_in, q_norm_g, w_uq, kv_norm_g, w_ukv,
           lru_conv_w, lru_conv_b, lru_wa, lru_ba, lru_wx, lru_bx, lru_lam,
           hy_conv_w, hy_conv_b, hy_w1, hy_b1, hy_w2, hy_b2, hy_freq, hy_w_out, hy_skip,
           w_br_a, w_br_b, w_br_c, w_out, ffn_w_gate, ffn_w_up, ffn_w_down, final_norm_g):
    depth = ada_w.shape[0]
    bx = x.shape[0]
    assert bx % 2 == 0 and bx + 1 <= 8
    cond = jnp.concatenate([c, c_ctx[None], jnp.zeros((8 - bx - 1, D_MODEL), F32)], axis=0)
    mods = _ada_mods(cond, ada_w, ada_b)[:, :, None, :]
    w = _prep_weights(dict(
        norm1_g=norm1_g, norm2_g=norm2_g, w_in=w_in, q_norm_g=q_norm_g, w_uq=w_uq,
        kv_norm_g=kv_norm_g, w_ukv=w_ukv, lru_conv_w=lru_conv_w, lru_conv_b=lru_conv_b,
        lru_wa=lru_wa, lru_ba=lru_ba, lru_wx=lru_wx, lru_bx=lru_bx, lru_lam=lru_lam,
        hy_conv_w=hy_conv_w, hy_conv_b=hy_conv_b, hy_w1=hy_w1, hy_b1=hy_b1, hy_w2=hy_w2, hy_b2=hy_b2,
        hy_freq=hy_freq, hy_w_out=hy_w_out, hy_skip=hy_skip, w_br_a=w_br_a, w_br_b=w_br_b,
        w_br_c=w_br_c, w_out=w_out, ffn_w_gate=ffn_w_gate, ffn_w_up=ffn_w_up, ffn_w_down=ffn_w_down))
    xc = ctx
    fg = final_norm_g[None]
    for layer in range(depth):
        x, xc = _layer(x, xc, mods, w, layer, fg, layer == depth - 1)
    return x
```

```python
import functools
import math

import numpy as np
import jax
import jax.numpy as jnp
from jax import lax
from jax.experimental import pallas as pl
from jax.experimental.pallas import tpu as pltpu

F32 = jnp.float32
BF16 = jnp.bfloat16

D_MODEL = 1024
GRID_W = 64
EPS = 1e-6

MLA_HEADS = 8
Q_LORA = 384
KV_LORA = 256
QK_NOPE = 64
QK_ROPE = 32
V_HEAD = 64
ROPE_BASE = 10000.0
SM_SCALE = (QK_NOPE + QK_ROPE) ** -0.5
HEAD_PAD = 128

LRU_WIDTH = 512
LRU_C = 8.0

HY_WIDTH = 512
HY_ORDER = 2
HY_SHORT = 3
HY_EMB = 33
HY_EMB_PAD = 48
HY_HID = 64
HY_INNER = 2
HY_FAST_DECAY = 0.3
HY_SLOW_DECAY = 1.5
HY_DECAY_TARGET = 1e-2
HY_N2 = 128
HY_CB = 128
HY_GROUP = 4
HY_CH_GROUP = 32

MLA_IN = Q_LORA + KV_LORA + QK_ROPE
IN_SPLITS = (MLA_IN, MLA_IN + LRU_WIDTH, MLA_IN + 2 * LRU_WIDTH, MLA_IN + 2 * LRU_WIDTH + 3 * HY_WIDTH)

Q_PRESCALE = SM_SCALE * math.log2(math.e)
FLASH_Q_TILE = 1024
FLASH_KEY_CHUNK = 2048
TOKEN_TILE = 256
DENSE_TILE = 512
HALO = 8
VMEM_LIMIT = 56 * 1024 * 1024


def _params(*sem):
    return pltpu.CompilerParams(dimension_semantics=sem, vmem_limit_bytes=VMEM_LIMIT)


def _dot(a, b):
    return jnp.dot(a, b, preferred_element_type=F32)


def _rms(x, g):
    return x * lax.rsqrt(jnp.mean(x * x, axis=-1, keepdims=True) + EPS) * g


def _sigmoid(x):
    return 0.5 * jnp.tanh(0.5 * x) + 0.5


def _norm_mod(x, g, shift, scale):
    return _rms(x, g) * (1.0 + scale) + shift


def _arr(op):
    return op[0] if isinstance(op, tuple) else op


def _full_spec(op):
    once = pl.Buffered(1)
    if isinstance(op, tuple):
        arr, layer = op
        nd = arr.ndim
        return pl.BlockSpec((None,) + arr.shape[1:], lambda *_: (layer,) + (0,) * (nd - 1), pipeline_mode=once)
    nd = op.ndim
    return pl.BlockSpec(op.shape, lambda *_: (0,) * nd, pipeline_mode=once)


def _mod_spec(mods, j):
    arr, layer, row = mods
    if row is None:
        return pl.BlockSpec((None, 1, 1, D_MODEL), lambda b, i: (layer, b, 0, j))
    return pl.BlockSpec((None, 1, 1, D_MODEL), lambda b, i: (layer, row, 0, j))


def _ada_kernel(c_ref, w_ref, b_ref, o_ref):
    c = c_ref[...]
    s = (c * jax.nn.sigmoid(c)).astype(BF16)
    o_ref[0] = _dot(s, w_ref[0].astype(BF16)) + b_ref[0]


def _ada_mods(cond, ada_w, ada_b):
    depth, _, width = ada_w.shape
    tn = 1536
    return pl.pallas_call(
        _ada_kernel,
        grid=(depth, width // tn),
        in_specs=[pl.BlockSpec((8, D_MODEL), lambda l, j: (0, 0)),
                  pl.BlockSpec((1, D_MODEL, tn), lambda l, j: (l, 0, j)),
                  pl.BlockSpec((1, 1, tn), lambda l, j: (l, 0, j))],
        out_specs=pl.BlockSpec((1, 8, tn), lambda l, j: (l, 0, j)),
        out_shape=jax.ShapeDtypeStruct((depth, 8, width), F32),
        compiler_params=_params("arbitrary", "arbitrary"),
        name="ada_mods",
    )(cond, ada_w, ada_b.reshape(depth, 1, width))


def _inproj_kernel(x_ref, sh_ref, sc_ref, g_ref, wm_ref, wx_ref, wg_ref, wh_ref,
                   qg_ref, kvg_ref, wq_ref, wqs_ref, wk_ref, wv_ref, cos_ref, sin_ref,
                   q_ref, k_ref, v_ref, lx_ref, lg_ref, hy_ref):
    h = _norm_mod(x_ref[0], g_ref[...], sh_ref[0], sc_ref[0]).astype(BF16)
    lx_ref[0] = _dot(h, wx_ref[...])
    lg_ref[0] = _dot(h, wg_ref[...])
    hy_ref[0] = _dot(h, wh_ref[...]).astype(hy_ref.dtype)
    z = _dot(h, wm_ref[...])
    nq = _rms(z[:, :Q_LORA], qg_ref[...]).astype(BF16)
    nkv = _rms(z[:, Q_LORA:Q_LORA + KV_LORA], kvg_ref[...]).astype(BF16)
    pe = z[:, Q_LORA + KV_LORA:Q_LORA + KV_LORA + HEAD_PAD]
    pes = z[:, Q_LORA + KV_LORA + HEAD_PAD:]
    cos = cos_ref[...]
    sin = sin_ref[...]
    q = _dot(nq, wq_ref[...])
    qs = _dot(nq, wqs_ref[...])
    kn = _dot(nkv, wk_ref[...])
    k_pe = pe * cos + pes * sin
    for hd in range(MLA_HEADS):
        sl = slice(hd * HEAD_PAD, (hd + 1) * HEAD_PAD)
        q_ref[0, :, sl] = ((q[:, sl] * cos + qs[:, sl] * sin) * Q_PRESCALE).astype(BF16)
        k_ref[0, :, sl] = (kn[:, sl] + k_pe).astype(BF16)
    v_ref[0] = _dot(nkv, wv_ref[...]).astype(BF16)


def _in_proj(x, mods, weights, cos, sin):
    bx, t, _ = x.shape
    tm = DENSE_TILE
    tile = lambda w: pl.BlockSpec((1, tm, w), lambda b, i: (b, i, 0))
    if cos.shape[0] == tm:
        tab = pl.BlockSpec((tm, HEAD_PAD), lambda b, i: (0, 0))
    else:
        tab = pl.BlockSpec((tm, HEAD_PAD), lambda b, i: (i, 0))
    outs = ((MLA_HEADS * HEAD_PAD, BF16), (MLA_HEADS * HEAD_PAD, BF16), (MLA_HEADS * V_HEAD, BF16),
            (LRU_WIDTH, F32), (LRU_WIDTH, F32), (3 * HY_WIDTH, BF16))
    return pl.pallas_call(
        _inproj_kernel,
        grid=(bx, t // tm),
        in_specs=[tile(D_MODEL), _mod_spec(mods, 0), _mod_spec(mods, 1)]
                 + [_full_spec(w) for w in weights] + [tab, tab],
        out_specs=[tile(w) for w, _ in outs],
        out_shape=[jax.ShapeDtypeStruct((bx, t, w), dt) for w, dt in outs],
        compiler_params=_params("parallel", "parallel"),
        name="in_proj",
    )(x, mods[0], mods[0], *[_arr(w) for w in weights], cos, sin)


def _qk(q, k):
    return lax.dot_general(q, k, (((1,), (1,)), ((), ())), preferred_element_type=F32)


def _flash_kernel(*refs, chunks):
    q_ref, o_ref = refs[0], refs[-1]
    kv = refs[1:-1]
    state = [None, None]
    for src, off, size in chunks:
        k_ref, v_ref = kv[2 * src], kv[2 * src + 1]
        v = v_ref[0, off:off + size, :]
        for h in range(2):
            sl = slice(h * HEAD_PAD, (h + 1) * HEAD_PAD)
            s = _qk(q_ref[0, :, sl], k_ref[0, off:off + size, sl])
            m_blk = jnp.max(s, axis=-1, keepdims=True)
            if state[h] is None:
                m = m_blk
                p = jnp.exp2(s - m)
                l = jnp.sum(p, axis=-1, keepdims=True)
                acc = _dot(p.astype(BF16), v)
            else:
                m_old, l, acc = state[h]
                m = jnp.maximum(m_old, m_blk)
                alpha = jnp.exp2(m_old - m)
                p = jnp.exp2(s - m)
                l = alpha * l + jnp.sum(p, axis=-1, keepdims=True)
                acc = alpha * acc + _dot(p.astype(BF16), v)
            state[h] = (m, l, acc)
    outs = [acc / l for _, l, acc in state]
    lane = lax.broadcasted_iota(jnp.int32, outs[0].shape, 1)
    o_ref[0] = jnp.where(lane < V_HEAD, outs[0], outs[1]).astype(o_ref.dtype)


def _flash(q, kc, vc, kl=None, vl=None):
    bx, t, _ = q.shape
    tq = min(FLASH_Q_TILE, t)
    sc = kc.shape[1]
    pairs = MLA_HEADS // 2
    qspec = pl.BlockSpec((1, tq, 2 * HEAD_PAD), lambda b, hp, i: (b, i, hp))
    kspec = lambda n: pl.BlockSpec((1, n, 2 * HEAD_PAD), lambda b, hp, i: (b, 0, hp))
    vspec = lambda n: pl.BlockSpec((1, n, 2 * V_HEAD), lambda b, hp, i: (b, 0, hp))
    in_specs = [qspec, kspec(sc), vspec(sc)]
    args = [q, kc, vc]
    chunks = [(0, 0, sc)]
    if kl is not None:
        sl = kl.shape[1]
        size = min(FLASH_KEY_CHUNK, sl)
        chunks += [(1, off, size) for off in range(0, sl, size)]
        in_specs += [kspec(sl), vspec(sl)]
        args += [kl, vl]
    return pl.pallas_call(
        functools.partial(_flash_kernel, chunks=tuple(chunks)),
        grid=(bx, pairs, t // tq),
        in_specs=in_specs,
        out_specs=pl.BlockSpec((1, tq, 2 * V_HEAD), lambda b, hp, i: (b, i, hp)),
        out_shape=jax.ShapeDtypeStruct((bx, t, MLA_HEADS * V_HEAD), BF16),
        compiler_params=_params("parallel", "parallel", "arbitrary"),
        name="flash",
    )(*args)


def _lru_kernel(xf_ref, xfp_ref, xfn_ref, xb_ref, xbp_ref, xbn_ref, cw_ref, cb_ref,
                wf_ref, bf_ref, wb_ref, bb_ref, lam_ref, h0_ref,
                hf_ref, hb_ref, hl_ref, af_s, bfw_s, ab_s, bbw_s, pad_s, carry_s):
    t = pl.program_id(0)
    nt = pl.num_programs(0)
    bx, tm, _ = xf_ref.shape

    @pl.when(t == 0)
    def _():
        carry_s[...] = h0_ref[...]

    def coeffs(b, x_ref, prev_ref, next_ref, tile, w_ref, bias_ref, lam, a_s, b_s):
        x = x_ref[b]
        pad_s[0:HALO, :] = jnp.where(tile > 0, prev_ref[b], 0.0)
        pad_s[HALO:HALO + tm, :] = x
        pad_s[HALO + tm:, :] = jnp.where(tile < nt - 1, next_ref[b], 0.0)
        xp = pad_s[...]
        rows = tm + 2 * HALO
        tap = lambda k: pltpu.roll(xp, k % rows, axis=0)[HALO:HALO + tm]
        u = (cw_ref[0:1, :] * tap(2) + cw_ref[1:2, :] * tap(1) + cw_ref[2:3, :] * x
             + cw_ref[3:4, :] * tap(-1) + cb_ref[...])
        y = _dot(u.astype(BF16), w_ref[...]) + bias_ref[...]
        t_r = jnp.tanh(y[:, :LRU_WIDTH])
        t_i = jnp.tanh(y[:, LRU_WIDTH:])
        neg = -lam
        softplus = jnp.maximum(neg, 0.0) + jnp.log1p(jnp.exp(-jnp.abs(neg)))
        c = (-0.5 * LRU_C * math.log2(math.e)) * softplus
        a = jnp.exp2(c * t_r + c)
        a_s[b] = a
        var = 1.0 - a * a
        half = 0.5 * (jnp.where(var > 0.0, var * lax.rsqrt(var), 0.0) * u)
        b_s[b] = half * t_i + half

    for b in range(bx):
        coeffs(b, xf_ref, xfp_ref, xfn_ref, t, wf_ref, bf_ref, lam_ref[0:1, :], af_s, bfw_s)
        coeffs(b, xb_ref, xbp_ref, xbn_ref, nt - 1 - t, wb_ref, bb_ref, lam_ref[1:2, :], ab_s, bbw_s)

    def body(j, hs):
        jb = tm - 1 - j
        out = []
        for b in range(bx):
            hf = af_s[b, pl.ds(j, 1), :] * hs[2 * b] + bfw_s[b, pl.ds(j, 1), :]
            hf_ref[b, pl.ds(j, 1), :] = hf
            hb = ab_s[b, pl.ds(jb, 1), :] * hs[2 * b + 1] + bbw_s[b, pl.ds(jb, 1), :]
            hb_ref[b, pl.ds(jb, 1), :] = hb
            out += [hf, hb]
        return tuple(out)

    init = tuple(carry_s[b, d:d + 1, :] for b in range(bx) for d in range(2))
    hs = lax.fori_loop(0, tm, body, init, unroll=8)
    for b in range(bx):
        for d in range(2):
            carry_s[b, d:d + 1, :] = hs[2 * b + d]
    hl_ref[...] = carry_s[...]


def _lru(lx, conv_w, conv_b, wf, biasf, wb, biasb, lam, h0):
    bx, t, w = lx.shape
    tm = TOKEN_TILE
    nt = t // tm
    hb_per = tm // HALO
    last_halo = t // HALO - 1
    fwd = lambda i: (0, i, 0)
    bwd = lambda i: (0, nt - 1 - i, 0)
    prev_of = lambda f: (lambda i: (0, jnp.maximum(f(i)[1] * hb_per - 1, 0), 0))
    next_of = lambda f: (lambda i: (0, jnp.minimum((f(i)[1] + 1) * hb_per, last_halo), 0))
    tile = lambda f: pl.BlockSpec((bx, tm, w), f)
    halo = lambda f: pl.BlockSpec((bx, HALO, w), f)
    state = pl.BlockSpec((bx, 2, w), lambda i: (0, 0, 0))
    return pl.pallas_call(
        _lru_kernel,
        grid=(nt,),
        in_specs=[tile(fwd), halo(prev_of(fwd)), halo(next_of(fwd)),
                  tile(bwd), halo(prev_of(bwd)), halo(next_of(bwd)),
                  _full_spec(conv_w), _full_spec(conv_b), _full_spec(wf), _full_spec(biasf),
                  _full_spec(wb), _full_spec(biasb), _full_spec(lam), state],
        out_specs=[tile(fwd), tile(bwd), state],
        out_shape=[jax.ShapeDtypeStruct((bx, t, w), F32), jax.ShapeDtypeStruct((bx, t, w), F32),
                   jax.ShapeDtypeStruct((bx, 2, w), F32)],
        scratch_shapes=[pltpu.VMEM((bx, tm, w), F32)] * 4
                       + [pltpu.VMEM((tm + 2 * HALO, w), F32), pltpu.VMEM((bx, 2, w), F32)],
        compiler_params=_params("arbitrary"),
        name="rglru",
    )(lx, lx, lx, lx, lx, lx, *[_arr(w) for w in (conv_w, conv_b, wf, biasf, wb, biasb, lam)], h0)


def _split_bf16(x):
    hi = x.astype(BF16)
    return hi, (x - hi.astype(F32)).astype(BF16)


def _dot3(a, b):
    ah, al = _split_bf16(a)
    bh, bl = _split_bf16(b)
    return _dot(ah, bh) + (_dot(ah, bl) + _dot(al, bh))


def _hyfilt_kernel(z_ref, w1_ref, b1_ref, w2_ref, b2_ref, fr_ref, wo_ref, dec_ref, o_ref):
    fr = fr_ref[...]
    h = jnp.sin(fr * (_dot3(z_ref[...], w1_ref[...]) + b1_ref[...]))
    for j in range(HY_INNER):
        h = jnp.sin(fr * (_dot3(h, w2_ref[j]) + b2_ref[j:j + 1, :]))
    dec = dec_ref[...]
    for g in range(2 * HY_ORDER):
        sl = slice(g * HY_WIDTH, (g + 1) * HY_WIDTH)
        o_ref[:, sl] = _dot3(h, wo_ref[:, sl]) * dec


def _hyfilt_cm_kernel(z_ref, w1_ref, b1_ref, w2_ref, b2_ref, fr_ref, wot_ref, dect_ref, o_ref):
    tn = z_ref.shape[0]
    fr = fr_ref[...]
    h = jnp.sin(fr * (_dot3(z_ref[...], w1_ref[...]) + b1_ref[...]))
    for j in range(HY_INNER):
        h = jnp.sin(fr * (_dot3(h, w2_ref[j]) + b2_ref[j:j + 1, :]))
    hh, hl = _split_bf16(h)
    nt = lambda a, b: lax.dot_general(a, b, (((1,), (1,)), ((), ())), preferred_element_type=F32)
    dect = dect_ref[...]
    for g in range(2 * HY_ORDER):
        wh, wl = _split_bf16(wot_ref[g * HY_WIDTH:(g + 1) * HY_WIDTH, :])
        res = (nt(wh, hh) + (nt(wh, hl) + nt(wl, hh))) * dect
        o_ref[g * HY_WIDTH:(g + 1) * HY_WIDTH] = res.reshape(HY_WIDTH, tn // HY_N2, HY_N2)


def _hy_filters_cm(z, w1, b1, w2, b2, freq, w_out_t, decay_t):
    n = z.shape[0]
    tn = 8 * HY_N2
    width = 2 * HY_ORDER * HY_WIDTH
    return pl.pallas_call(
        _hyfilt_cm_kernel,
        grid=(n // tn,),
        in_specs=[pl.BlockSpec((tn, HY_EMB_PAD), lambda i: (i, 0)), _full_spec(w1), _full_spec(b1),
                  _full_spec(w2), _full_spec(b2), _full_spec(freq), _full_spec(w_out_t),
                  pl.BlockSpec((HY_WIDTH, tn), lambda i: (0, i))],
        out_specs=pl.BlockSpec((width, tn // HY_N2, HY_N2), lambda i: (0, i, 0)),
        out_shape=jax.ShapeDtypeStruct((width, n // HY_N2, HY_N2), F32),
        compiler_params=_params("parallel"),
        name="hyena_filters_cm",
    )(z, *[_arr(w) for w in (w1, b1, w2, b2, freq, w_out_t)], decay_t)


def _hy_filters(z, w1, b1, w2, b2, freq, w_out, decay):
    n = z.shape[0]
    tn = min(n, 512)
    width = 2 * HY_ORDER * HY_WIDTH
    return pl.pallas_call(
        _hyfilt_kernel,
        grid=(n // tn,),
        in_specs=[pl.BlockSpec((tn, HY_EMB_PAD), lambda i: (i, 0)), _full_spec(w1), _full_spec(b1),
                  _full_spec(w2), _full_spec(b2), _full_spec(freq), _full_spec(w_out),
                  pl.BlockSpec((tn, HY_WIDTH), lambda i: (i, 0))],
        out_specs=pl.BlockSpec((tn, width), lambda i: (i, 0)),
        out_shape=jax.ShapeDtypeStruct((n, width), F32),
        compiler_params=_params("parallel"),
        name="hyena_filters",
    )(z, *[_arr(w) for w in (w1, b1, w2, b2, freq, w_out)], decay)


def _bitrev(k, bits):
    r = jnp.zeros_like(k)
    for b in range(bits):
        r = r | (((k >> b) & 1) << (bits - 1 - b))
    return r


def _fft_lead_fwd(sre, sim, wr_ref, wi_ref, n1):
    half = n1 // 2
    m = half
    first = True
    while m >= 1:
        shift = int(math.log2(m))
        stride = half // m

        def body(q, c, m=m, shift=shift, stride=stride, first=first):
            grp = q >> shift
            j = q - (grp << shift)
            i0 = (grp << (shift + 1)) + j
            i1 = i0 + m
            wr = wr_ref[j * stride]
            wi = wi_ref[j * stride]
            ar, ai = sre[i0], sim[i0]
            if first:
                dr, di = ar, ai
            else:
                br, bi = sre[i1], sim[i1]
                sre[i0] = ar + br
                sim[i0] = ai + bi
                dr, di = ar - br, ai - bi
            sre[i1] = dr * wr - di * wi
            sim[i1] = dr * wi + di * wr
            return c

        lax.fori_loop(0, half, body, 0)
        first = False
        m //= 2


def _fft_lead_inv(sre, sim, wr_ref, wi_ref, n1):
    half = n1 // 2
    m = 1
    while m <= half:
        shift = int(math.log2(m))
        stride = half // m
        last = m == half

        def body(q, c, m=m, shift=shift, stride=stride, last=last):
            grp = q >> shift
            j = q - (grp << shift)
            i0 = (grp << (shift + 1)) + j
            i1 = i0 + m
            wr = wr_ref[j * stride]
            wi = wi_ref[j * stride]
            ar, ai = sre[i0], sim[i0]
            br, bi = sre[i1], sim[i1]
            tr = br * wr + bi * wi
            ti = bi * wr - br * wi
            sre[i0] = ar + tr
            sim[i0] = ai + ti
            if not last:
                sre[i1] = ar - tr
                sim[i1] = ai - ti
            return c

        lax.fori_loop(0, half, body, 0)
        m *= 2


def _spectrum_loop(sre, sim, f2_s, tw, n1, emit):
    tw0r_ref, tw0i_ref, wgr_ref, wgi_ref = tw
    bits = int(math.log2(n1))
    group = min(HY_GROUP, n1)

    def body(kb, tws):
        ks, xs = [], []
        for g in range(group):
            k1 = kb * group + g
            blk = _bitrev(k1, bits)
            tr, ti = tws[2 * g], tws[2 * g + 1]
            ar, ai = sre[blk], sim[blk]
            xs.append(jnp.concatenate([ar * tr - ai * ti, ar * ti + ai * tr], axis=0).astype(BF16))
            ks.append((k1, blk))
        b2 = _dot(f2_s[...], jnp.concatenate(xs, axis=1))
        emit(ks, b2, tws)
        wgr, wgi = wgr_ref[...], wgi_ref[...]
        nxt = []
        for g in range(group):
            tr, ti = tws[2 * g], tws[2 * g + 1]
            nxt += [tr * wgr - ti * wgi, tr * wgi + ti * wgr]
        return tuple(nxt)

    init = tuple(r[g] for g in range(group) for r in (tw0r_ref, tw0i_ref))
    lax.fori_loop(0, n1 // group, body, init)


def _group_cols(b2, g, cb):
    return b2[:HY_N2, g * cb:(g + 1) * cb], b2[HY_N2:, g * cb:(g + 1) * cb]


def _hyspec_kernel(wr_ref, wi_ref, hf_ref, hb_ref, skip_ref, f2_ref, tw0r_ref, tw0i_ref, wgr_ref, wgi_ref,
                   k_ref, sre, sim, f2_s):
    n = hf_ref.shape[0]
    n1 = 2 * n // HY_N2
    half = n1 // 2
    cb = hf_ref.shape[1]
    f2_s[...] = f2_ref[...].astype(BF16)
    for direction, h_ref in enumerate((hf_ref, hb_ref)):
        sre[0:half] = h_ref[...].reshape(half, HY_N2, cb)
        sim[0:half] = jnp.zeros((half, HY_N2, cb), F32)
        _fft_lead_fwd(sre, sim, wr_ref, wi_ref, n1)

        def emit(ks, b2, tws, direction=direction):
            for g, (k1, _) in enumerate(ks):
                br, bi = _group_cols(b2, g, cb)
                if direction == 0:
                    k_ref[0, k1, 0:HY_N2, :] = br + skip_ref[0]
                    k_ref[0, k1, HY_N2:, :] = bi
                else:
                    k_ref[0, k1, 0:HY_N2, :] = k_ref[0, k1, 0:HY_N2, :] + br
                    k_ref[0, k1, HY_N2:, :] = k_ref[0, k1, HY_N2:, :] - bi

        _spectrum_loop(sre, sim, f2_s, (tw0r_ref, tw0i_ref, wgr_ref, wgi_ref), n1, emit)


def _hy_spectra(filt, skip, consts):
    n = filt.shape[0]
    n1 = 2 * n // HY_N2
    cbn = HY_WIDTH // HY_CB
    smem = pl.BlockSpec(memory_space=pltpu.SMEM)
    col = lambda direction: (lambda o, c: (0, (direction * HY_ORDER + o) * cbn + c))
    tw_names = ("tw0r", "tw0i", "wgr", "wgi")
    return pl.pallas_call(
        _hyspec_kernel,
        grid=(HY_ORDER, cbn),
        in_specs=[smem, smem,
                  pl.BlockSpec((n, HY_CB), col(0)), pl.BlockSpec((n, HY_CB), col(1)),
                  pl.BlockSpec((None, 1, 1, HY_CB), lambda o, c: (skip[1], o, 0, c)),
                  _full_spec(consts["f2"])] + [_full_spec(consts[k]) for k in tw_names],
        out_specs=pl.BlockSpec((1, n1, 2 * HY_N2, HY_CB), lambda o, c: (o, 0, 0, c)),
        out_shape=jax.ShapeDtypeStruct((HY_ORDER, n1, 2 * HY_N2, HY_WIDTH), F32),
        scratch_shapes=[pltpu.VMEM((n1, HY_N2, HY_CB), F32)] * 2 + [pltpu.VMEM((2 * HY_N2, 2 * HY_N2), BF16)],
        compiler_params=_params("parallel", "parallel"),
        name="hyena_spectra",
    )(consts["wr"], consts["wi"], filt, filt, skip[0], consts["f2"], *[consts[k] for k in tw_names])


def _short_conv3(x, pad_s, w_ref, b_ref):
    n = x.shape[0]
    pad_s[HALO:HALO + n, :] = x
    return (w_ref[0:1, :] * pad_s[HALO - 1:HALO - 1 + n, :] + w_ref[1:2, :] * x
            + w_ref[2:3, :] * pad_s[HALO + 1:HALO + 1 + n, :] + b_ref[...])


def _hyconv_kernel(wr_ref, wi_ref, u_ref, g_ref, ucw_ref, ucb_ref, gcw_ref, gcb_ref, k_ref,
                   f2_ref, f2i_ref, tw0r_ref, tw0i_ref, wgr_ref, wgi_ref, o_ref,
                   sre, sim, pad_s, f2_s, f2i_s, *, conv_u):
    n = u_ref.shape[1]
    cb = u_ref.shape[2]
    n1 = 2 * n // HY_N2
    half = n1 // 2
    f2_s[...] = f2_ref[...].astype(BF16)
    f2i_s[...] = f2i_ref[...].astype(BF16)
    margin = jnp.zeros((HALO, cb), F32)
    pad_s[0:HALO, :] = margin
    pad_s[HALO + n:, :] = margin

    for b, s in enumerate((sre, sim)):
        u = u_ref[b].astype(F32)
        if conv_u:
            u = _short_conv3(u, pad_s, ucw_ref, ucb_ref)
        s[0:half] = u.reshape(half, HY_N2, cb)
    _fft_lead_fwd(sre, sim, wr_ref, wi_ref, n1)

    def emit(ks, b2, tws):
        ps = []
        for g, (k1, _) in enumerate(ks):
            br, bi = _group_cols(b2, g, cb)
            kr = k_ref[0, k1, 0:HY_N2, :]
            ki = k_ref[0, k1, HY_N2:, :]
            ps.append(jnp.concatenate([br * kr - bi * ki, br * ki + bi * kr], axis=0).astype(BF16))
        c2 = _dot(f2i_s[...], jnp.concatenate(ps, axis=1))
        for g, (_, blk) in enumerate(ks):
            cr, ci = _group_cols(c2, g, cb)
            tr, ti = tws[2 * g], tws[2 * g + 1]
            sre[blk] = cr * tr + ci * ti
            sim[blk] = ci * tr - cr * ti

    _spectrum_loop(sre, sim, f2_s, (tw0r_ref, tw0i_ref, wgr_ref, wgi_ref), n1, emit)
    _fft_lead_inv(sre, sim, wr_ref, wi_ref, n1)

    for b, s in enumerate((sre, sim)):
        gate = _short_conv3(g_ref[b].astype(F32), pad_s, gcw_ref, gcb_ref)
        o_ref[b] = (gate * s[0:half].reshape(n, cb)).astype(o_ref.dtype)


def _hy_conv(u, u_col, g, g_col, conv_w, conv_b, spectra, order, consts, conv_u):
    bx, n, _ = u.shape
    n1 = 2 * n // HY_N2
    cbn = HY_WIDTH // HY_CB
    smem = pl.BlockSpec(memory_space=pltpu.SMEM)
    data = lambda col: pl.BlockSpec((2, n, HY_CB), lambda c, p, col=col: (p, 0, col + c))
    layer = conv_w[1]
    wrow = lambda rows, col: pl.BlockSpec((None, rows, HY_CB), lambda c, p, col=col: (layer, 0, col + c))
    ucol = u_col if conv_u else g_col
    const_names = ("f2", "f2i", "tw0r", "tw0i", "wgr", "wgi")
    dft = pltpu.VMEM((2 * HY_N2, 2 * HY_N2), BF16)
    return pl.pallas_call(
        functools.partial(_hyconv_kernel, conv_u=conv_u),
        grid=(cbn, bx // 2),
        in_specs=[smem, smem, data(u_col), data(g_col),
                  wrow(HY_SHORT, ucol), wrow(1, ucol), wrow(HY_SHORT, g_col), wrow(1, g_col),
                  pl.BlockSpec((1, n1, 2 * HY_N2, HY_CB), lambda c, p: (order, 0, 0, c))]
                 + [_full_spec(consts[k]) for k in const_names],
        out_specs=pl.BlockSpec((2, n, HY_CB), lambda c, p: (p, 0, c)),
        out_shape=jax.ShapeDtypeStruct((bx, n, HY_WIDTH), BF16),
        scratch_shapes=[pltpu.VMEM((n1, HY_N2, HY_CB), F32)] * 2
                       + [pltpu.VMEM((n + 2 * HALO, HY_CB), F32), dft, dft],
        compiler_params=_params("parallel", "arbitrary"),
        name="hyena_conv",
    )(consts["wr"], consts["wi"], u, g, conv_w[0], conv_b[0], conv_w[0], conv_b[0], spectra,
      *[consts[k] for k in const_names])


def _hy_consts(n):
    big_n = 2 * n
    n1 = big_n // HY_N2
    group = min(HY_GROUP, n1)
    q = np.arange(max(n1 // 2, 1), dtype=np.float64)
    ang1 = 2.0 * np.pi * q / n1
    idx = np.arange(HY_N2, dtype=np.float64)
    ang2 = 2.0 * np.pi * np.outer(idx, idx) / HY_N2
    c, s = np.cos(ang2), np.sin(ang2)
    f2 = np.block([[c, s], [-s, c]])
    f2i = np.block([[c, -s], [s, c]]) / big_n
    lane = np.ones((1, 1, HY_CB))
    ang0 = 2.0 * np.pi * np.arange(group)[:, None, None] * idx[None, :, None] / big_n
    angg = 2.0 * np.pi * group * idx[:, None] / big_n
    return {
        "wr": jnp.asarray(np.cos(ang1), F32), "wi": jnp.asarray(-np.sin(ang1), F32),
        "f2": jnp.asarray(f2, F32), "f2i": jnp.asarray(f2i, F32),
        "tw0r": jnp.asarray(np.cos(ang0) * lane, F32), "tw0i": jnp.asarray(-np.sin(ang0) * lane, F32),
        "wgr": jnp.asarray(np.cos(angg) * lane[0], F32), "wgi": jnp.asarray(-np.sin(angg) * lane[0], F32),
    }


def _hy_tables(n):
    t = np.linspace(0.0, 1.0, n, dtype=np.float32)[:, None].astype(np.float64)
    bands = (HY_EMB - 1) // 2
    w = 2.0 * np.pi * np.arange(n, dtype=np.float64) / n
    f = np.linspace(1e-4, bands - 1, bands, dtype=np.float32).astype(np.float64)
    ang = w[:, None] * f[None, :]
    z = np.concatenate([t, np.cos(ang), -np.sin(ang), np.zeros((n, HY_EMB_PAD - HY_EMB))], axis=-1)
    max_decay = math.log(HY_DECAY_TARGET) / HY_FAST_DECAY
    min_decay = math.log(HY_DECAY_TARGET) / HY_SLOW_DECAY
    deltas = np.abs(np.linspace(min_decay, max_decay, HY_WIDTH, dtype=np.float32).astype(np.float64))
    return jnp.asarray(z, F32), jnp.asarray(np.exp(-t * deltas), F32)


def _cm_short_conv(x, taps, masks):
    first_lane, last_lane, first_row, last_row = masks
    n1h = x.shape[0]
    r = pltpu.roll(x, 1, axis=1)
    prev = jnp.where(first_lane, jnp.where(first_row, 0.0, pltpu.roll(r, 1, axis=0)), r)
    l = pltpu.roll(x, HY_N2 - 1, axis=1)
    nxt = jnp.where(last_lane, jnp.where(last_row, 0.0, pltpu.roll(l, n1h - 1, axis=0)), l)
    return taps[0] * prev + taps[1] * x + taps[2] * nxt + taps[3]


def _cm_masks(n1h):
    lane = lax.broadcasted_iota(jnp.int32, (n1h, HY_N2), 1)
    row = lax.broadcasted_iota(jnp.int32, (n1h, HY_N2), 0)
    return lane == 0, lane == HY_N2 - 1, row == 0, row == n1h - 1


def _cm_taps(cw_ref, cb_ref, layer, ch):
    width = 3 * HY_WIDTH
    base = layer * HY_SHORT * width + ch
    return cw_ref[base], cw_ref[base + width], cw_ref[base + 2 * width], cb_ref[layer * width + ch]


def _cm_twiddle(a2, n1, twr, twi, conj):
    out = []
    for h in range(2):
        ar = a2[:n1, h * HY_N2:(h + 1) * HY_N2]
        ai = a2[n1:, h * HY_N2:(h + 1) * HY_N2]
        if conj:
            out.append((ar * twr + ai * twi, ai * twr - ar * twi))
        else:
            out.append((ar * twr - ai * twi, ar * twi + ai * twr))
    return out


def _hycm_spec_kernel(skip_ref, hf_ref, hb_ref, f1r_ref, f2t_ref, twr_ref, twi_ref, k_ref, f1_s, f2t_s, *, layer):
    order, cblk = pl.program_id(0), pl.program_id(1)
    cb, n1h, _ = hf_ref.shape
    n1 = 2 * n1h
    f1_s[...] = f1r_ref[...].astype(BF16)
    f2t_s[...] = f2t_ref[...].astype(BF16)
    twr, twi = twr_ref[...], twi_ref[...]

    def group(gi, carry):
        base = gi * HY_CH_GROUP
        blocks = []
        for j in range(HY_CH_GROUP):
            x2 = jnp.concatenate([hf_ref[base + j], hb_ref[base + j]], axis=1).astype(BF16)
            for re, im in _cm_twiddle(_dot(f1_s[...], x2), n1, twr, twi, False):
                blocks.append(jnp.concatenate([re, im], axis=1).astype(BF16))
        b2 = _dot(jnp.concatenate(blocks, axis=0), f2t_s[...])
        for j in range(HY_CH_GROUP):
            bf = b2[(2 * j) * n1:(2 * j + 1) * n1]
            bb = b2[(2 * j + 1) * n1:(2 * j + 2) * n1]
            skip = skip_ref[(layer * HY_ORDER + order) * HY_WIDTH + cblk * cb + base + j]
            k_ref[0, base + j] = jnp.concatenate([bf[:, :HY_N2] + bb[:, :HY_N2] + skip,
                                                  bf[:, HY_N2:] - bb[:, HY_N2:]], axis=1)
        return carry

    lax.fori_loop(0, cb // HY_CH_GROUP, group, 0)


def _hycm_spectra(filt_cm, skip, consts):
    _, n1h, _ = filt_cm.shape
    n1 = 2 * n1h
    nblk = HY_WIDTH // HY_CB
    smem = pl.BlockSpec(memory_space=pltpu.SMEM)
    blk = lambda direction: pl.BlockSpec((HY_CB, n1h, HY_N2),
                                         lambda o, c: ((direction * HY_ORDER + o) * nblk + c, 0, 0))
    names = ("f1r", "f2t", "twr", "twi")
    return pl.pallas_call(
        functools.partial(_hycm_spec_kernel, layer=skip[1]),
        grid=(HY_ORDER, nblk),
        in_specs=[smem, blk(0), blk(1)] + [_full_spec(consts[k]) for k in names],
        out_specs=pl.BlockSpec((1, HY_CB, n1, 2 * HY_N2), lambda o, c: (o, c, 0, 0)),
        out_shape=jax.ShapeDtypeStruct((HY_ORDER, HY_WIDTH, n1, 2 * HY_N2), F32),
        scratch_shapes=[pltpu.VMEM((2 * n1, n1h), BF16), pltpu.VMEM((2 * HY_N2, 2 * HY_N2), BF16)],
        compiler_params=_params("parallel", "parallel"),
        name="hyena_spectra_cm",
    )(skip[0], filt_cm, filt_cm, *[consts[k] for k in names])


def _hycm_conv_kernel(cw_ref, cb_ref, u_ref, g_ref, k_ref, f1_ref, f1i_ref, f2t_ref, f2ti_ref, twr_ref, twi_ref,
                      o_ref, f1_s, f1i_s, f2t_s, f2ti_s, *, conv_u, u_ch0, g_ch0, layer):
    cblk = pl.program_id(0)
    _, cb, n1h, _ = u_ref.shape
    n1 = 2 * n1h
    for dst, src in ((f1_s, f1_ref), (f1i_s, f1i_ref), (f2t_s, f2t_ref), (f2ti_s, f2ti_ref)):
        dst[...] = src[...].astype(BF16)
    twr, twi = twr_ref[...], twi_ref[...]
    masks = _cm_masks(n1h)
    group_n = HY_CH_GROUP

    def group(gi, carry):
        base = gi * group_n
        x2s = []
        for j in range(group_n):
            xs = []
            for b in range(2):
                x = u_ref[b, base + j].astype(F32)
                if conv_u:
                    x = _cm_short_conv(x, _cm_taps(cw_ref, cb_ref, layer, u_ch0 + cblk * cb + base + j), masks)
                xs.append(x)
            x2s.append(jnp.concatenate(xs, axis=0).astype(BF16))
        blocks = []
        for j in range(0, group_n, 2):
            a2 = _dot(f1_s[...], jnp.concatenate([x2s[j], x2s[j + 1]], axis=1))
            for re, im in _cm_twiddle(a2, n1, twr, twi, False):
                blocks.append(jnp.concatenate([re, im], axis=1).astype(BF16))
        b2 = _dot(jnp.concatenate(blocks, axis=0), f2t_s[...])
        prods = []
        for j in range(group_n):
            br = b2[j * n1:(j + 1) * n1, :HY_N2]
            bi = b2[j * n1:(j + 1) * n1, HY_N2:]
            kr = k_ref[0, base + j, :, 0:HY_N2]
            ki = k_ref[0, base + j, :, HY_N2:]
            prods.append(jnp.concatenate([br * kr - bi * ki, br * ki + bi * kr], axis=1).astype(BF16))
        c2 = _dot(jnp.concatenate(prods, axis=0), f2ti_s[...])
        cols = []
        for j in range(group_n):
            cr = c2[j * n1:(j + 1) * n1, :HY_N2]
            ci = c2[j * n1:(j + 1) * n1, HY_N2:]
            cols.append(jnp.concatenate([cr * twr + ci * twi, ci * twr - cr * twi], axis=0).astype(BF16))
        for j in range(0, group_n, 2):
            y2 = _dot(f1i_s[...], jnp.concatenate([cols[j], cols[j + 1]], axis=1))
            for h in range(2):
                ch = base + j + h
                taps = _cm_taps(cw_ref, cb_ref, layer, g_ch0 + cblk * cb + ch)
                for b in range(2):
                    y = y2[b * n1h:(b + 1) * n1h, h * HY_N2:(h + 1) * HY_N2]
                    gate = _cm_short_conv(g_ref[b, ch].astype(F32), taps, masks)
                    o_ref[b, ch] = (gate * y).astype(o_ref.dtype)
        return carry

    lax.fori_loop(0, cb // group_n, group, 0)


def _hycm_conv(u, u_ch0, g, g_ch0, conv_w, conv_b, spectra, order, consts, conv_u):
    bx, _, n1h, _ = u.shape
    n1 = 2 * n1h
    nblk = HY_WIDTH // HY_CB
    smem = pl.BlockSpec(memory_space=pltpu.SMEM)
    data = lambda ch0: pl.BlockSpec((2, HY_CB, n1h, HY_N2), lambda c, p, ch0=ch0: (p, ch0 // HY_CB + c, 0, 0))
    names = ("f1", "f1i", "f2t", "f2ti", "twr", "twi")
    dft = pltpu.VMEM((2 * HY_N2, 2 * HY_N2), BF16)
    return pl.pallas_call(
        functools.partial(_hycm_conv_kernel, conv_u=conv_u, u_ch0=u_ch0, g_ch0=g_ch0, layer=conv_w[1]),
        grid=(nblk, bx // 2),
        in_specs=[smem, smem, data(u_ch0), data(g_ch0),
                  pl.BlockSpec((1, HY_CB, n1, 2 * HY_N2), lambda c, p: (order, c, 0, 0))]
                 + [_full_spec(consts[k]) for k in names],
        out_specs=pl.BlockSpec((2, HY_CB, n1h, HY_N2), lambda c, p: (p, c, 0, 0)),
        out_shape=jax.ShapeDtypeStruct((bx, HY_WIDTH, n1h, HY_N2), BF16),
        scratch_shapes=[pltpu.VMEM((2 * n1, n1), BF16), pltpu.VMEM((n1, 2 * n1), BF16), dft, dft],
        compiler_params=_params("parallel", "arbitrary"),
        name="hyena_conv_cm",
    )(conv_w[0], conv_b[0], u, g, spectra, *[consts[k] for k in names])


def _cm_consts(n):
    big_n = 2 * n
    n1h = n // HY_N2
    n1 = 2 * n1h
    a1 = 2.0 * np.pi * np.outer(np.arange(n1), np.arange(n1h)) / n1
    c1, s1 = np.cos(a1), np.sin(a1)
    idx = np.arange(HY_N2, dtype=np.float64)
    a2 = 2.0 * np.pi * np.outer(idx, idx) / HY_N2
    c2, s2 = np.cos(a2), np.sin(a2)
    at = 2.0 * np.pi * np.outer(np.arange(n1), idx) / big_n
    f1 = np.block([[c1, s1], [-s1, c1]])
    return {
        "f1": jnp.asarray(f1, F32), "f1r": jnp.asarray(f1[:, :n1h], F32),
        "f1i": jnp.asarray(np.block([[c1.T, -s1.T], [s1.T, c1.T]]), F32),
        "f2t": jnp.asarray(np.block([[c2, -s2], [s2, c2]]), F32),
        "f2ti": jnp.asarray(np.block([[c2, s2], [-s2, c2]]) / big_n, F32),
        "twr": jnp.asarray(np.cos(at), F32), "twi": jnp.asarray(-np.sin(at), F32),
    }


def _gelu_tanh(x):
    return x * (0.5 * (1.0 + jnp.tanh(math.sqrt(2.0 / math.pi) * (x + 0.044715 * (x * x * x)))))


def _merge_kernel(x_ref, sh_ref, sc_ref, gt_ref, g_ref, ya_ref, hf_ref, hb_ref, lg_ref, yc_ref,
                  wgate_ref, wa_ref, wb_ref, wc_ref, wo_ref, o_ref):
    x = x_ref[0]
    h = _norm_mod(x, g_ref[...], sh_ref[0], sc_ref[0]).astype(BF16)
    yb = ((hf_ref[0] + hb_ref[0]) * _gelu_tanh(lg_ref[0])).astype(BF16)
    y = None
    for j, (br, w_ref) in enumerate(((ya_ref[0], wa_ref), (yb, wb_ref), (yc_ref[0], wc_ref))):
        gate = _sigmoid(_dot(h, wgate_ref[:, j * D_MODEL:(j + 1) * D_MODEL]))
        term = gate * _dot(br, w_ref[...])
        y = term if y is None else y + term
    o_ref[0] = x + gt_ref[0] * _dot(y.astype(BF16), wo_ref[...])


def _merge(x, mods, norm_g, ya, hf, hb, lg, yc, wgate, wa, wb, wc, wo):
    bx, t, _ = x.shape
    tm = DENSE_TILE
    tile = lambda w: pl.BlockSpec((1, tm, w), lambda b, i: (b, i, 0))
    return pl.pallas_call(
        _merge_kernel,
        grid=(bx, t // tm),
        in_specs=[tile(D_MODEL), _mod_spec(mods, 0), _mod_spec(mods, 1), _mod_spec(mods, 2), _full_spec(norm_g),
                  tile(MLA_HEADS * V_HEAD), tile(LRU_WIDTH), tile(LRU_WIDTH), tile(LRU_WIDTH), tile(HY_WIDTH),
                  _full_spec(wgate), _full_spec(wa), _full_spec(wb), _full_spec(wc), _full_spec(wo)],
        out_specs=tile(D_MODEL),
        out_shape=jax.ShapeDtypeStruct(x.shape, F32),
        compiler_params=_params("parallel", "parallel"),
        name="merge",
    )(x, mods[0], mods[0], mods[0], _arr(norm_g), ya, hf, hb, lg, yc, *[_arr(w) for w in (wgate, wa, wb, wc, wo)])


def _ffn_kernel(x_ref, sh_ref, sc_ref, gt_ref, g_ref, wg_ref, wu_ref, wd_ref, fg_ref, o_ref, *, final):
    x = x_ref[0]
    h = _norm_mod(x, g_ref[...], sh_ref[0], sc_ref[0]).astype(BF16)
    gate = _dot(h, wg_ref[...])
    act = (gate * _sigmoid(gate) * _dot(h, wu_ref[...])).astype(BF16)
    y = x + gt_ref[0] * _dot(act, wd_ref[...])
    o_ref[0] = _rms(y, fg_ref[...]) if final else y


def _ffn(x, mods, norm_g, wg, wu, wd, final_g, final):
    bx, t, _ = x.shape
    tm = DENSE_TILE
    tile = pl.BlockSpec((1, tm, D_MODEL), lambda b, i: (b, i, 0))
    return pl.pallas_call(
        functools.partial(_ffn_kernel, final=final),
        grid=(bx, t // tm),
        in_specs=[tile, _mod_spec(mods, 3), _mod_spec(mods, 4), _mod_spec(mods, 5), _full_spec(norm_g),
                  _full_spec(wg), _full_spec(wu), _full_spec(wd), _full_spec(final_g)],
        out_specs=tile,
        out_shape=jax.ShapeDtypeStruct(x.shape, F32),
        compiler_params=_params("parallel", "parallel"),
        name="ffn",
    )(x, mods[0], mods[0], mods[0], *[_arr(w) for w in (norm_g, wg, wu, wd)], final_g)


_ROPE_SWAP = np.array([8, 9, 10, 11, 12, 13, 14, 15, 0, 1, 2, 3, 4, 5, 6, 7,
                       24, 25, 26, 27, 28, 29, 30, 31, 16, 17, 18, 19, 20, 21, 22, 23])


def _rope_tables(n):
    cos = np.zeros((n, HEAD_PAD))
    sin = np.zeros((n, HEAD_PAD))
    cos[:, :QK_NOPE + QK_ROPE] = 1.0
    if n % GRID_W == 0 and n > 0:
        pos = np.arange(n)
        seg = QK_ROPE // 2
        inv = 1.0 / (ROPE_BASE ** (np.arange(seg // 2, dtype=np.float64) * 2.0 / seg))
        for s, p in enumerate((pos // GRID_W, pos % GRID_W)):
            ang = p[:, None] * inv[None, :]
            base = QK_NOPE + s * seg
            cos[:, base:base + seg] = np.concatenate([np.cos(ang), np.cos(ang)], axis=-1)
            sin[:, base:base + seg] = np.concatenate([-np.sin(ang), np.sin(ang)], axis=-1)
    return jnp.asarray(cos, F32), jnp.asarray(sin, F32)


def _identity_rope_tables():
    cos = np.zeros((DENSE_TILE, HEAD_PAD))
    cos[:, :QK_NOPE + QK_ROPE] = 1.0
    return jnp.asarray(cos, F32), jnp.zeros((DENSE_TILE, HEAD_PAD), F32)


def _block_diag(w):
    l, g, i, j = w.shape
    return jnp.einsum("lgij,gh->lgihj", w, jnp.eye(g, dtype=w.dtype)).reshape(l, g * i, g * j)


def _prep_weights(p):
    depth = p["w_in"].shape[0]
    w_in = p["w_in"]
    kpe = w_in[..., Q_LORA + KV_LORA:MLA_IN]
    z64 = jnp.zeros((depth, D_MODEL, QK_NOPE), F32)
    z32 = jnp.zeros((depth, D_MODEL, HEAD_PAD - QK_NOPE - QK_ROPE), F32)
    row = lambda a: a[:, None, :]
    out = {
        "w_mla": jnp.concatenate([w_in[..., :Q_LORA + KV_LORA], z64, kpe, z32, z64, kpe[..., _ROPE_SWAP], z32],
                                 axis=-1).astype(BF16),
        "w_lx": w_in[..., IN_SPLITS[0]:IN_SPLITS[1]].astype(BF16),
        "w_lg": w_in[..., IN_SPLITS[1]:IN_SPLITS[2]].astype(BF16),
        "w_hy": w_in[..., IN_SPLITS[2]:IN_SPLITS[3]].astype(BF16),
        "w_gate": w_in[..., IN_SPLITS[3]:].astype(BF16),
    }
    wq = p["w_uq"].reshape(depth, Q_LORA, MLA_HEADS, QK_NOPE + QK_ROPE)
    pad = jnp.zeros((depth, Q_LORA, MLA_HEADS, HEAD_PAD - QK_NOPE - QK_ROPE), F32)
    out["w_q"] = jnp.concatenate([wq, pad], axis=-1).reshape(depth, Q_LORA, -1).astype(BF16)
    out["w_qs"] = jnp.concatenate([jnp.zeros((depth, Q_LORA, MLA_HEADS, QK_NOPE), F32),
                                   wq[..., QK_NOPE:][..., _ROPE_SWAP], pad], axis=-1
                                  ).reshape(depth, Q_LORA, -1).astype(BF16)
    wkv = p["w_ukv"].reshape(depth, KV_LORA, MLA_HEADS, QK_NOPE + V_HEAD)
    out["w_k"] = jnp.concatenate([wkv[..., :QK_NOPE],
                                  jnp.zeros((depth, KV_LORA, MLA_HEADS, HEAD_PAD - QK_NOPE), F32)],
                                 axis=-1).reshape(depth, KV_LORA, -1).astype(BF16)
    out["w_v"] = wkv[..., QK_NOPE:].reshape(depth, KV_LORA, -1).astype(BF16)
    for d, name in enumerate(("f", "b")):
        out["lru_w" + name] = (0.5 * jnp.concatenate([_block_diag(p["lru_wa"][:, d]), _block_diag(p["lru_wx"][:, d])],
                                                     axis=-1)).astype(BF16)
        out["lru_bias" + name] = row(0.5 * jnp.concatenate([p["lru_ba"][:, d], p["lru_bx"][:, d]], axis=-1))
    out["hy_w1"] = jnp.concatenate([p["hy_w1"], jnp.zeros((depth, HY_EMB_PAD - HY_EMB, HY_HID), F32)], axis=1)
    for name in ("w_br_a", "w_br_b", "w_br_c", "w_out", "ffn_w_gate", "ffn_w_up", "ffn_w_down"):
        out[name] = p[name].astype(BF16)
    for name in ("norm1_g", "norm2_g", "q_norm_g", "kv_norm_g", "lru_conv_b", "hy_b1", "hy_freq", "hy_conv_b"):
        out[name] = row(p[name])
    for name in ("lru_conv_w", "lru_lam", "hy_w2", "hy_b2", "hy_w_out", "hy_conv_w"):
        out[name] = p[name]
    out["hy_skip"] = p["hy_skip"][:, :, None, :]
    out["hy_w_out_t"] = p["hy_w_out"].transpose(0, 2, 1)
    out["hy_conv_w_flat"] = p["hy_conv_w"].reshape(-1)
    out["hy_conv_b_flat"] = p["hy_conv_b"].reshape(-1)
    out["hy_skip_flat"] = p["hy_skip"].reshape(-1)
    return out


def _mixers(q, k, v, lx, hy, w, layer, kv_ctx=None, h0=None, branch_out=True):
    bx, n, _ = lx.shape
    at = lambda name: (w[name], layer)
    if h0 is None:
        h0 = jnp.zeros((bx, 2, LRU_WIDTH), F32)
    hf, hb, hlast = _lru(lx, at("lru_conv_w"), at("lru_conv_b"), at("lru_wf"), at("lru_biasf"),
                         at("lru_wb"), at("lru_biasb"), at("lru_lam"), h0)
    if not branch_out:
        return None, hlast
    if kv_ctx is None:
        ya = _flash(q, k, v)
    else:
        ya = _flash(q, kv_ctx[0], kv_ctx[1], k, v)
    z, decay = _hy_tables(n)
    mlp = (at("hy_w1"), at("hy_b1"), at("hy_w2"), at("hy_b2"), at("hy_freq"))
    n1h = n // HY_N2
    if n1h % 8 == 0:
        consts = _cm_consts(n)
        filt_cm = _hy_filters_cm(z, *mlp, at("hy_w_out_t"), decay.T)
        spectra = _hycm_spectra(filt_cm, at("hy_skip_flat"), consts)
        hy_cm = hy.reshape(bx, n1h, HY_N2, 3 * HY_WIDTH).transpose(0, 3, 1, 2)
        cw, cb = at("hy_conv_w_flat"), at("hy_conv_b_flat")
        y1 = _hycm_conv(hy_cm, 0, hy_cm, HY_WIDTH, cw, cb, spectra, 0, consts, True)
        yc = _hycm_conv(y1, 0, hy_cm, 2 * HY_WIDTH, cw, cb, spectra, 1, consts, False)
        yc = yc.transpose(0, 2, 3, 1).reshape(bx, n, HY_WIDTH)
    else:
        consts = _hy_consts(n)
        filt = _hy_filters(z, *mlp, at("hy_w_out"), decay)
        spectra = _hy_spectra(filt, at("hy_skip"), consts)
        cbn = HY_WIDTH // HY_CB
        cw, cb = at("hy_conv_w"), at("hy_conv_b")
        y1 = _hy_conv(hy, 0, hy, cbn, cw, cb, spectra, 0, consts, True)
        yc = _hy_conv(y1, 0, hy, 2 * cbn, cw, cb, spectra, 1, consts, False)
    return (ya, hf, hb, yc), hlast


def _layer(x, xc, mods, w, layer, final_g, last):
    bx, s, _ = x.shape
    sc = xc.shape[1]
    at = lambda name: (w[name], layer)
    mods_l = (mods, layer, None)
    mods_c = (mods, layer, bx)
    flat = lambda a: a.reshape(1, bx * sc, a.shape[-1])
    unflat = lambda a: a.reshape(bx, sc, a.shape[-1])

    proj_w = tuple(at(name) for name in ("norm1_g", "w_mla", "w_lx", "w_lg", "w_hy", "q_norm_g", "kv_norm_g",
                                         "w_q", "w_qs", "w_k", "w_v"))
    qc, kc, vc, lxc, lgc, hyc = [unflat(a) for a in _in_proj(flat(xc), mods_c, proj_w, *_identity_rope_tables())]
    ql, kl, vl, lxl, lgl, hyl = _in_proj(x, mods_l, proj_w, *_rope_tables(s))

    br_c, h_c = _mixers(qc, kc, vc, lxc, hyc, w, layer, branch_out=not last)
    br_l, _ = _mixers(ql, kl, vl, lxl, hyl, w, layer, kv_ctx=(kc, vc), h0=h_c)

    merge_w = tuple(at(name) for name in ("w_gate", "w_br_a", "w_br_b", "w_br_c", "w_out"))
    ffn_w = tuple(at(name) for name in ("ffn_w_gate", "ffn_w_up", "ffn_w_down"))
    x = _merge(x, mods_l, at("norm1_g"), br_l[0], br_l[1], br_l[2], lgl, br_l[3], *merge_w)
    x = _ffn(x, mods_l, at("norm2_g"), *ffn_w, final_g, last)
    if not last:
        xcf = _merge(flat(xc), mods_c, at("norm1_g"), flat(br_c[0]), flat(br_c[1]), flat(br_c[2]), flat(lgc),
                     flat(br_c[3]), *merge_w)
        xc = unflat(_ffn(xcf, mods_c, at("norm2_g"), *ffn_w, final_g, False))
    return x, xc


def kernel(x, c, ctx, c_ctx, ada_w, ada_b, norm1_g, norm2_g, w_in, q_norm_g, w_uq, kv_norm_g, w_ukv,
           lru_conv_w, lru_conv_b, lru_wa, lru_ba, lru_wx, lru_bx, lru_lam,
           hy_conv_w, hy_conv_b, hy_w1, hy_b1, hy_w2, hy_b2, hy_freq, hy_w_out, hy_skip,
           w_br_a, w_br_b, w_br_c, w_out, ffn_w_gate, ffn_w_up, ffn_w_down, final_norm_g):
    depth = ada_w.shape[0]
    bx = x.shape[0]
    assert bx % 2 == 0 and bx + 1 <= 8
    cond = jnp.concatenate([c, c_ctx[None], jnp.zeros((8 - bx - 1, D_MODEL), F32)], axis=0)
    mods = _ada_mods(cond, ada_w, ada_b)[:, :, None, :]
    w = _prep_weights(dict(
        norm1_g=norm1_g, norm2_g=norm2_g, w_in=w_in, q_norm_g=q_norm_g, w_uq=w_uq,
        kv_norm_g=kv_norm_g, w_ukv=w_ukv, lru_conv_w=lru_conv_w, lru_conv_b=lru_conv_b,
        lru_wa=lru_wa, lru_ba=lru_ba, lru_wx=lru_wx, lru_bx=lru_bx, lru_lam=lru_lam,
        hy_conv_w=hy_conv_w, hy_conv_b=hy_conv_b, hy_w1=hy_w1, hy_b1=hy_b1, hy_w2=hy_w2, hy_b2=hy_b2,
        hy_freq=hy_freq, hy_w_out=hy_w_out, hy_skip=hy_skip, w_br_a=w_br_a, w_br_b=w_br_b,
        w_br_c=w_br_c, w_out=w_out, ffn_w_gate=ffn_w_gate, ffn_w_up=ffn_w_up, ffn_w_down=ffn_w_down))
    xc = ctx
    fg = final_norm_g[None]
    for layer in range(depth):
        x, xc = _layer(x, xc, mods, w, layer, fg, layer == depth - 1)
    return x
```
